```python
import math
import jax, jax.numpy as jnp
from jax import lax
import numpy as np

D_MODEL = 1024
BATCH = 8
SEQ = 2048
DEPTH = 1
DEC_BATCH = 32
DEC_SEQ = 1
PAST_LEN = 16384
PAGE_SIZE = 128

LRU_WIDTH = D_MODEL // 2
LRU_BLOCKS = 8
LRU_BLOCK_W = LRU_WIDTH // LRU_BLOCKS
CONV_W = 4
LRU_C = 8.0
ATT_WIDTH = D_MODEL // 2
N_HEADS = 8
HEAD_DIM = ATT_WIDTH // N_HEADS
MOBA_BLOCK = 256
MOBA_TOPK = 3
Q_CHUNK = 32
REL_BUCKETS = 32
REL_MAX_DIST = 128
MIX_WIDTH = LRU_WIDTH + ATT_WIDTH
IN_WIDTH = 2 * LRU_WIDTH + 4 * ATT_WIDTH
RMS_EPS = 1e-6
NEG_INF = -1e30

kernel_name = "hymba_rglru_moba_adaln_step"


def rms_norm(x, g):
    xf = x.astype(jnp.float32)
    y = xf * lax.rsqrt(jnp.mean(xf * xf, axis=-1, keepdims=True) + RMS_EPS)
    return y.astype(x.dtype) * g


def t5_bucket(dist):
    n = jnp.maximum(dist, 0)
    max_exact = REL_BUCKETS // 2
    nf = jnp.maximum(n, 1).astype(jnp.float32)
    large = max_exact + (jnp.log(nf / max_exact) / math.log(REL_MAX_DIST / max_exact)
                         * (REL_BUCKETS - max_exact)).astype(jnp.int32)
    large = jnp.minimum(large, REL_BUCKETS - 1)
    return jnp.where(n < max_exact, n, large)


def rg_lru(u, h0, w_rgate, b_rgate, w_igate, b_igate, lru_a_param):
    B, S, W = u.shape
    ub = u.reshape(B, S, LRU_BLOCKS, LRU_BLOCK_W)
    r = jax.nn.sigmoid(jnp.einsum('bsnc,ncd->bsnd', ub, w_rgate).reshape(B, S, W) + b_rgate)
    i = jax.nn.sigmoid(jnp.einsum('bsnc,ncd->bsnd', ub, w_igate).reshape(B, S, W) + b_igate)
    log_a = (-LRU_C * r.astype(jnp.float32)) * jax.nn.softplus(-lru_a_param.astype(jnp.float32))
    a = jnp.exp(log_a)
    bx = jnp.sqrt(-jnp.expm1(2.0 * log_a)) * (i * u).astype(jnp.float32)

    def step(h, ab):
        a_t, b_t = ab
        h = a_t * h + b_t
        return h, h

    h_last, hs = lax.scan(step, h0.astype(jnp.float32), (a.swapaxes(0, 1), bx.swapaxes(0, 1)))
    return hs.swapaxes(0, 1).astype(u.dtype), h_last.astype(u.dtype)


def moba_chunk(q, q_pos, k_blocks, v_blocks, k_means, rel_bias):
    B, H, Q, hd = q.shape
    nb = k_blocks.shape[2]
    topk = min(MOBA_TOPK, nb)
    own = q_pos // MOBA_BLOCK
    gate = jnp.einsum('bhqd,bhnd->bhqn', q.astype(jnp.float32), k_means)
    past = jnp.arange(nb, dtype=jnp.int32)[None, :] < own[:, None]
    gate = jnp.where(past[None, None], gate, NEG_INF)
    top_val, top_idx = lax.top_k(gate, topk)
    valid = top_val > 0.5 * NEG_INF
    own_b = jnp.broadcast_to(own[None, None, :, None], (B, H, Q, 1)).astype(jnp.int32)
    idx = jnp.concatenate([top_idx.astype(jnp.int32), own_b], axis=-1)
    valid_all = jnp.concatenate([valid, jnp.ones((B, H, Q, 1), bool)], axis=-1)
    bi = jnp.arange(B)[:, None, None, None]
    hi = jnp.arange(H)[None, :, None, None]
    k_sel = k_blocks[bi, hi, idx]
    v_sel = v_blocks[bi, hi, idx]
    kpos = idx[..., None] * MOBA_BLOCK + jnp.arange(MOBA_BLOCK, dtype=jnp.int32)
    dist = q_pos[None, None, :, None, None] - kpos
    bias = rel_bias[t5_bucket(dist), hi[..., None]].astype(jnp.float32)
    allowed = valid_all[..., None] & (dist >= 0)
    logits = jnp.einsum('bhqd,bhqskd->bhqsk', q, k_sel).astype(jnp.float32) * (HEAD_DIM ** -0.5) + bias
    logits = jnp.where(allowed, logits, NEG_INF)
    p = jax.nn.softmax(logits.reshape(B, H, Q, -1), axis=-1).reshape(logits.shape).astype(v_sel.dtype)
    return jnp.einsum('bhqsk,bhqskd->bhqd', p, v_sel)


def moba_attention(q, k, v, k_past, v_past, rel_bias):
    B, S, H, hd = q.shape
    P = k_past.shape[1]
    T = P + S
    k_all = jnp.concatenate([k_past.astype(k.dtype), k], axis=1).transpose(0, 2, 1, 3)
    v_all = jnp.concatenate([v_past.astype(v.dtype), v], axis=1).transpose(0, 2, 1, 3)
    nb = -(-T // MOBA_BLOCK)
    pad = nb * MOBA_BLOCK - T
    k_blocks = jnp.pad(k_all, ((0, 0), (0, 0), (0, pad), (0, 0))).reshape(B, H, nb, MOBA_BLOCK, hd)
    v_blocks = jnp.pad(v_all, ((0, 0), (0, 0), (0, pad), (0, 0))).reshape(B, H, nb, MOBA_BLOCK, hd)
    k_means = jnp.mean(k_blocks.astype(jnp.float32), axis=3)
    q_pos = P + jnp.arange(S, dtype=jnp.int32)
    qc = Q_CHUNK if S % Q_CHUNK == 0 else S
    nch = S // qc
    q_chunks = jnp.moveaxis(q.transpose(0, 2, 1, 3).reshape(B, H, nch, qc, hd), 2, 0)
    pos_chunks = q_pos.reshape(nch, qc)
    outs = lax.map(lambda qp: moba_chunk(qp[0], qp[1], k_blocks, v_blocks, k_means, rel_bias),
                   (q_chunks, pos_chunks))
    return jnp.moveaxis(outs, 0, 2).reshape(B, H, S, hd).transpose(0, 2, 1, 3).reshape(B, S, H * hd)


def mixer_layer(x, c, conv_buf, h0, k_past, v_past, w_ada, b_ada, g_norm, w_in, conv_w, conv_b,
                w_rgate, b_rgate, w_igate, b_igate, lru_a_param, rel_bias, w_out):
    B, S, _ = x.shape
    shift, scale, gate = jnp.split(jax.nn.silu(c) @ w_ada + b_ada, 3, axis=-1)
    xn = rms_norm(x, g_norm) * (1 + scale[:, None]) + shift[:, None]
    proj = xn @ w_in
    cuts = [LRU_WIDTH, 2 * LRU_WIDTH, 2 * LRU_WIDTH + ATT_WIDTH,
            2 * LRU_WIDTH + 2 * ATT_WIDTH, 2 * LRU_WIDTH + 3 * ATT_WIDTH]
    u, g_lru, q, k, v, g_att = jnp.split(proj, cuts, axis=-1)
    u_pad = jnp.concatenate([conv_buf.astype(u.dtype), u], axis=1)
    u_conv = conv_b + sum(u_pad[:, t:t + S] * conv_w[t] for t in range(CONV_W))
    conv_new = u_pad[:, S:]
    y_lru, h_last = rg_lru(u_conv, h0, w_rgate, b_rgate, w_igate, b_igate, lru_a_param)
    q = q.reshape(B, S, N_HEADS, HEAD_DIM)
    k = k.reshape(B, S, N_HEADS, HEAD_DIM)
    v = v.reshape(B, S, N_HEADS, HEAD_DIM)
    y_att = moba_attention(q, k, v, k_past, v_past, rel_bias)
    mix = jnp.concatenate([y_lru * jax.nn.silu(g_lru), y_att * jax.nn.silu(g_att)], axis=-1)
    out = x + gate[:, None] * (mix @ w_out)
    return out, k, v, h_last, conv_new


def setup_inputs(seed: int = 0) -> dict:
    key = jax.random.key(seed)
    ks = jax.random.split(key, 24)
    n_pages = PAST_LEN // PAGE_SIZE
    n_used = DEC_BATCH * n_pages
    n_phys = n_used + max(1, n_used // 4)

    def nrm(k, shape, s):
        return jax.random.normal(k, shape, jnp.float32) * s

    a0 = jax.random.uniform(ks[19], (LRU_WIDTH,), jnp.float32, 0.9, 0.999) ** (1.0 / LRU_C)
    return {
        "x_prompt": nrm(ks[0], (BATCH, SEQ, D_MODEL), 1.0),
        "x_sample": nrm(ks[1], (DEC_BATCH, DEC_SEQ, D_MODEL), 1.0),
        "cache_k": nrm(ks[2], (n_phys, PAGE_SIZE, N_HEADS, HEAD_DIM), 1.0),
        "cache_v": nrm(ks[3], (n_phys, PAGE_SIZE, N_HEADS, HEAD_DIM), 1.0),
        "state_lru_h": nrm(ks[4], (DEC_BATCH, LRU_WIDTH), 0.5),
        "state_lru_conv": nrm(ks[5], (DEC_BATCH, CONV_W - 1, LRU_WIDTH), 1.0),
        "page_table": jax.random.permutation(ks[6], n_phys)[:n_used].reshape(DEC_BATCH, n_pages).astype(jnp.int32),
        "c_prompt": nrm(ks[7], (BATCH, D_MODEL), 1.0),
        "c_sample": nrm(ks[8], (DEC_BATCH, D_MODEL), 1.0),
        "w_ada": nrm(ks[9], (D_MODEL, 3 * D_MODEL), 0.5 * D_MODEL ** -0.5),
        "b_ada": nrm(ks[10], (3 * D_MODEL,), 0.01),
        "g_norm": 1.0 + nrm(ks[11], (D_MODEL,), 0.01),
        "w_in": nrm(ks[12], (D_MODEL, IN_WIDTH), D_MODEL ** -0.5),
        "conv_w": nrm(ks[13], (CONV_W, LRU_WIDTH), CONV_W ** -0.5),
        "conv_b": nrm(ks[14], (LRU_WIDTH,), 0.01),
        "w_rgate": nrm(ks[15], (LRU_BLOCKS, LRU_BLOCK_W, LRU_BLOCK_W), LRU_BLOCK_W ** -0.5),
        "b_rgate": nrm(ks[16], (LRU_WIDTH,), 0.01),
        "w_igate": nrm(ks[17], (LRU_BLOCKS, LRU_BLOCK_W, LRU_BLOCK_W), LRU_BLOCK_W ** -0.5),
        "b_igate": nrm(ks[18], (LRU_WIDTH,), 0.01),
        "lru_a_param": jnp.log(a0) - jnp.log1p(-a0),
        "rel_bias": nrm(ks[20], (REL_BUCKETS, N_HEADS), 0.5),
        "w_out": nrm(ks[21], (MIX_WIDTH, D_MODEL), MIX_WIDTH ** -0.5),
        "g_final": 1.0 + nrm(ks[22], (D_MODEL,), 0.01),
    }


def reference(x_prompt, x_sample, cache_k, cache_v, state_lru_h, state_lru_conv, page_table,
              c_prompt, c_sample, w_ada, b_ada, g_norm, w_in, conv_w, conv_b, w_rgate, b_rgate,
              w_igate, b_igate, lru_a_param, rel_bias, w_out, g_final):
    B = x_prompt.shape[0]
    DB = x_sample.shape[0]
    n_pages = page_table.shape[1]
    dt = x_prompt.dtype
    zero_conv = jnp.zeros((B, CONV_W - 1, LRU_WIDTH), dt)
    zero_h = jnp.zeros((B, LRU_WIDTH), dt)
    empty_kv = jnp.zeros((B, 0, N_HEADS, HEAD_DIM), dt)
    hp = x_prompt
    for _ in range(DEPTH):
        hp, k_prompt, v_prompt, lru_h_prompt, lru_conv_prompt = mixer_layer(
            hp, c_prompt, zero_conv, zero_h, empty_kv, empty_kv, w_ada, b_ada, g_norm, w_in,
            conv_w, conv_b, w_rgate, b_rgate, w_igate, b_igate, lru_a_param, rel_bias, w_out)
    y_prompt = rms_norm(hp, g_final)
    k_past = cache_k[page_table].reshape(DB, n_pages * PAGE_SIZE, N_HEADS, HEAD_DIM)
    v_past = cache_v[page_table].reshape(DB, n_pages * PAGE_SIZE, N_HEADS, HEAD_DIM)
    hs = x_sample
    for _ in range(DEPTH):
        hs, k_sample, v_sample, lru_h_sample, lru_conv_sample = mixer_layer(
            hs, c_sample, state_lru_conv, state_lru_h, k_past, v_past, w_ada, b_ada, g_norm, w_in,
            conv_w, conv_b, w_rgate, b_rgate, w_igate, b_igate, lru_a_param, rel_bias, w_out)
    y_sample = rms_norm(hs, g_final)
    return (y_prompt, y_sample, k_prompt, v_prompt, lru_h_prompt, lru_conv_prompt,
            k_sample, v_sample, lru_h_sample, lru_conv_sample)
```

```python
import functools
import math

import jax
import jax.numpy as jnp
from jax import lax
from jax.experimental import pallas as pl
from jax.experimental.pallas import tpu as pltpu

F32 = jnp.float32
BF16 = jnp.bfloat16
I32 = jnp.int32

N_HEADS = 8
LRU_BLOCKS = 8
CONV_W = 4
LRU_C = 8.0
MOBA_BLOCK = 256
MOBA_TOPK = 3
REL_BUCKETS = 32
REL_MAX_DIST = 128
RMS_EPS = 1e-6
NEG_INF = -1e30
HIGHEST = lax.Precision.HIGHEST

LANES = 128
SUBLANES = 8
VMEM_LIMIT_BYTES = 56 * 1024 * 1024

ROW_TILE = 512
LRU_CHUNK = 512
PAGES_PER_CHUNK = 8
K_SLOTS = 4


def _sigmoid(x):
    return 1.0 / (1.0 + jnp.exp(-x))


def _silu(x):
    return x * _sigmoid(x)


def _softplus(z):
    return jnp.maximum(z, 0.0) + jnp.log1p(jnp.exp(-jnp.abs(z)))


def _t5_bucket(dist):
    n = jnp.maximum(dist, 0)
    max_exact = REL_BUCKETS // 2
    nf = jnp.maximum(n, 1).astype(F32)
    large = max_exact + (jnp.log(nf / max_exact) / math.log(REL_MAX_DIST / max_exact)
                         * (REL_BUCKETS - max_exact)).astype(I32)
    large = jnp.minimum(large, REL_BUCKETS - 1)
    return jnp.where(n < max_exact, n, large)


def _params(*sem):
    return pltpu.CompilerParams(dimension_semantics=sem, vmem_limit_bytes=VMEM_LIMIT_BYTES)


def _adaln_kernel(c_ref, w_ref, b_ref, o_ref):
    s = _silu(c_ref[...]).astype(BF16)
    o_ref[...] = jnp.dot(s, w_ref[...].astype(BF16), preferred_element_type=F32) + b_ref[...]


def _adaln(c, w_ada, b_ada):
    n, d = c.shape
    return pl.pallas_call(
        _adaln_kernel,
        out_shape=jax.ShapeDtypeStruct((n, 3 * d), F32),
        grid=(3,),
        in_specs=[pl.BlockSpec((n, d), lambda j: (0, 0)),
                  pl.BlockSpec((d, d), lambda j: (0, j)),
                  pl.BlockSpec((1, d), lambda j: (0, j))],
        out_specs=pl.BlockSpec((n, d), lambda j: (0, j)),
        compiler_params=_params("arbitrary"),
        name="adaln",
    )(c, w_ada, b_ada.reshape(1, 3 * d))


def _modulated_norm(x, mod, g, d):
    ms = jnp.mean(x * x, axis=-1, keepdims=True)
    xn = x * lax.rsqrt(ms + RMS_EPS) * g
    return xn * (1.0 + mod[:, d:2 * d]) + mod[:, 0:d]


def _inproj_kernel(x_ref, mod_ref, g_ref, wrow_ref, wkvt_ref,
                   u_ref, gl_ref, q_ref, ga_ref, kt_ref, vt_ref, *, d, w):
    xm = _modulated_norm(x_ref[...], mod_ref[...], g_ref[...], d).astype(BF16)
    pr = jnp.dot(xm, wrow_ref[...], preferred_element_type=F32)
    u_ref[...] = pr[:, 0:w]
    gl_ref[...] = pr[:, w:2 * w]
    q_ref[...] = pr[:, 2 * w:3 * w]
    ga_ref[...] = pr[:, 3 * w:4 * w]
    pt = lax.dot_general(wkvt_ref[...], xm, (((1,), (1,)), ((), ())),
                         preferred_element_type=F32)
    kt_ref[...] = pt[0:w]
    vt_ref[...] = pt[w:2 * w]


def _inproj(x, mod3, g_norm, w_row, w_kvt):
    b, s, d = x.shape
    w = w_row.shape[1] // 4
    tm = ROW_TILE
    row = pl.BlockSpec((None, tm, w), lambda i, t: (i, t, 0))
    col = pl.BlockSpec((None, w, tm), lambda i, t: (i, 0, t))
    return pl.pallas_call(
        functools.partial(_inproj_kernel, d=d, w=w),
        out_shape=[jax.ShapeDtypeStruct((b, s, w), F32)] * 4
        + [jax.ShapeDtypeStruct((b, w, s), F32)] * 2,
        grid=(b, s // tm),
        in_specs=[pl.BlockSpec((None, tm, d), lambda i, t: (i, t, 0)),
                  pl.BlockSpec((None, 1, 3 * d), lambda i, t: (i, 0, 0)),
                  pl.BlockSpec((1, d), lambda i, t: (0, 0)),
                  pl.BlockSpec((d, 4 * w), lambda i, t: (0, 0)),
                  pl.BlockSpec((2 * w, d), lambda i, t: (0, 0))],
        out_specs=[row, row, row, row, col, col],
        compiler_params=_params("arbitrary", "arbitrary"),
        name="inproj",
    )(x, mod3, g_norm, w_row, w_kvt)


def _lru_gates(u_conv, wg_ref, bg_ref, ap_ref, w):
    g2 = jnp.dot(u_conv.astype(BF16), wg_ref[...], preferred_element_type=F32) + bg_ref[...]
    r = _sigmoid(g2[:, 0:w])
    i = _sigmoid(g2[:, w:2 * w])
    log_a = (-LRU_C * r) * _softplus(-ap_ref[...])
    a = jnp.exp(log_a)
    bx = jnp.sqrt(-jnp.tanh(log_a) * (a * a + 1.0)) * (i * u_conv)
    return a, bx


def _lru_kernel(u_ref, gl_ref, cw_ref, cb_ref, wg_ref, bg_ref, ap_ref,
                mix_ref, hl_ref, tail_ref, a_scr, b_scr, hs_scr, h_scr, tail_scr, *, tc, w):
    c = pl.program_id(1)

    @pl.when(c == 0)
    def _():
        h_scr[...] = jnp.zeros_like(h_scr)
        tail_scr[...] = jnp.zeros_like(tail_scr)

    u = u_ref[...]
    ext = jnp.concatenate([tail_scr[...], u], axis=0)
    cw = cw_ref[...]
    u_conv = cb_ref[...] + (ext[5:5 + tc] * cw[0:1] + ext[6:6 + tc] * cw[1:2]
                            + ext[7:7 + tc] * cw[2:3] + u * cw[3:4])
    tail_scr[...] = u[tc - SUBLANES:tc]
    a, bx = _lru_gates(u_conv, wg_ref, bg_ref, ap_ref, w)
    a_scr[...] = a
    b_scr[...] = bx

    row = lax.broadcasted_iota(I32, (SUBLANES, w), 0)

    def tile(t, h):
        r0 = pl.multiple_of(t * SUBLANES, SUBLANES)
        at = a_scr[pl.ds(r0, SUBLANES), :]
        bt = b_scr[pl.ds(r0, SUBLANES), :]
        for dd in (1, 2, 4):
            keep = row >= dd
            bt = jnp.where(keep, at * pltpu.roll(bt, dd, 0) + bt, bt)
            at = jnp.where(keep, at * pltpu.roll(at, dd, 0), at)
        hs = at * h + bt
        hs_scr[pl.ds(r0, SUBLANES), :] = hs
        return jnp.broadcast_to(hs[SUBLANES - 1:SUBLANES, :], (SUBLANES, w))

    h = lax.fori_loop(0, tc // SUBLANES, tile, h_scr[...], unroll=4)
    h_scr[...] = h
    hl_ref[...] = h
    tail_ref[...] = u[tc - SUBLANES:tc]
    mix_ref[...] = (hs_scr[...] * _silu(gl_ref[...])).astype(mix_ref.dtype)


def _lru_prompt(u, gl, conv_w, conv_b, wg, bg, a_param):
    b, s, w = u.shape
    tc = LRU_CHUNK
    const = lambda shape: pl.BlockSpec(shape, lambda i, t: (0,) * len(shape))
    return pl.pallas_call(
        functools.partial(_lru_kernel, tc=tc, w=w),
        out_shape=[jax.ShapeDtypeStruct((b, s, w), BF16),
                   jax.ShapeDtypeStruct((b, SUBLANES, w), F32),
                   jax.ShapeDtypeStruct((b, SUBLANES, w), F32)],
        grid=(b, s // tc),
        in_specs=[pl.BlockSpec((None, tc, w), lambda i, t: (i, t, 0)),
                  pl.BlockSpec((None, tc, w), lambda i, t: (i, t, 0)),
                  const((CONV_W, w)), const((1, w)), const((w, 2 * w)), const((1, 2 * w)),
                  const((1, w))],
        out_specs=[pl.BlockSpec((None, tc, w), lambda i, t: (i, t, 0)),
                   pl.BlockSpec((None, SUBLANES, w), lambda i, t: (i, 0, 0)),
                   pl.BlockSpec((None, SUBLANES, w), lambda i, t: (i, 0, 0))],
        scratch_shapes=[pltpu.VMEM((tc, w), F32), pltpu.VMEM((tc, w), F32),
                        pltpu.VMEM((tc, w), F32), pltpu.VMEM((SUBLANES, w), F32),
                        pltpu.VMEM((SUBLANES, w), F32)],
        compiler_params=_params("arbitrary", "arbitrary"),
        name="lru_prompt",
    )(u, gl, conv_w, conv_b, wg, bg, a_param)


def _attn_kernel(rb_ref, q_ref, kt_ref, vt_ref, ga_ref, avg_ref, bkt_ref, o_ref,
                 bias_scr, km_scr, lg_scr, m_scr, *, hd):
    p = pl.program_id(0)
    b = pl.program_id(1)
    j = pl.program_id(2)
    blk = MOBA_BLOCK
    qi = lax.broadcasted_iota(I32, (blk, blk), 0)
    ki = lax.broadcasted_iota(I32, (blk, blk), 1)

    @pl.when((b == 0) & (j == 0))
    def _():
        for hh in range(2):
            h = 2 * p + hh
            for which in range(2):
                bk = bkt_ref[which]
                tab = jnp.zeros((blk, blk), F32)
                for k in range(REL_BUCKETS):
                    tab = jnp.where(bk == k, rb_ref[k, h], tab)
                if which == 0:
                    tab = jnp.where(qi >= ki, tab, NEG_INF)
                bias_scr[hh, which] = tab

    @pl.when(j == 0)
    def _():
        km_scr[...] = jnp.dot(kt_ref[...], avg_ref[...], precision=HIGHEST,
                              preferred_element_type=F32)

    lane = lax.broadcasted_iota(I32, (blk, LANES), 1)
    qs = q_ref[...] * (hd ** -0.5)
    out = jnp.zeros((blk, LANES), F32)
    for hh in range(2):
        head_lanes = (lane >= hd * hh) & (lane < hd * (hh + 1))
        qm = jnp.where(head_lanes, qs, 0.0)
        qmb = qm.astype(BF16)
        bias_far = rb_ref[REL_BUCKETS - 1, 2 * p + hh]

        gate = jnp.dot(qm, km_scr[...], precision=HIGHEST, preferred_element_type=F32)
        past = lane < j
        g = jnp.where(past, gate, NEG_INF)
        cnt = jnp.zeros((blk, LANES), I32)
        for m in range(7):
            gm = g[:, m:m + 1]
            beats = (gm > g) | ((gm == g) & (lane > m))
            cnt = cnt + jnp.where(beats, 1, 0)
        sel = jnp.where(past & (cnt < MOBA_TOPK) & (g > 0.5 * NEG_INF), 1.0, 0.0)

        def sel_col(n):
            return jnp.max(jnp.where(lane == n, sel, 0.0), axis=1, keepdims=True) > 0.5

        def scores(n):
            off = pl.multiple_of(n * blk, blk)
            kt = kt_ref[:, pl.ds(off, blk)].astype(BF16)
            return jnp.dot(qmb, kt, preferred_element_type=F32)

        lg = scores(j) + bias_scr[hh, 0]
        lg_scr[j] = lg
        m_scr[...] = jnp.max(lg, axis=1, keepdims=True)

        @pl.when(j >= 1)
        def _():
            n = j - 1
            lg = scores(n) + bias_scr[hh, 1] + jnp.where(sel_col(n), 0.0, NEG_INF)
            lg_scr[n] = lg
            m_scr[...] = jnp.maximum(m_scr[...], jnp.max(lg, axis=1, keepdims=True))

        def far(n, carry):
            lg = scores(n) + jnp.where(sel_col(n), bias_far, NEG_INF)
            lg_scr[n] = lg
            m_scr[...] = jnp.maximum(m_scr[...], jnp.max(lg, axis=1, keepdims=True))
            return carry

        lax.fori_loop(0, jnp.maximum(j - 1, 0), far, 0)

        mrow = m_scr[...]

        def pv(n, carry):
            acc, den = carry
            off = pl.multiple_of(n * blk, blk)
            pn = jnp.exp(lg_scr[n] - mrow)
            vt = vt_ref[:, pl.ds(off, blk)].astype(BF16)
            acc = acc + lax.dot_general(pn.astype(BF16), vt, (((1,), (1,)), ((), ())),
                                        preferred_element_type=F32)
            return acc, den + jnp.sum(pn, axis=1, keepdims=True)

        acc, den = lax.fori_loop(0, j + 1, pv, (jnp.zeros((blk, LANES), F32),
                                                jnp.zeros((blk, 1), F32)))
        out = jnp.where(head_lanes, acc / den, out)

    o_ref[...] = (out * _silu(ga_ref[...])).astype(o_ref.dtype)


def _attn_prompt(q, kt, vt, ga, rel_bias):
    b, s, w = q.shape
    hd = w // N_HEADS
    blk = MOBA_BLOCK
    nb = s // blk
    npair = w // LANES
    t = jnp.arange(s, dtype=I32)
    avg = jnp.where((t[:, None] // blk) == jnp.arange(LANES, dtype=I32)[None, :],
                    1.0 / blk, 0.0).astype(F32)
    dq = jnp.arange(blk, dtype=I32)
    d_own = dq[:, None] - dq[None, :]
    bkt = jnp.stack([_t5_bucket(d_own), _t5_bucket(d_own + blk)])
    grid_spec = pltpu.PrefetchScalarGridSpec(
        num_scalar_prefetch=0,
        grid=(npair, b, nb),
        in_specs=[pl.BlockSpec(memory_space=pltpu.SMEM),
                  pl.BlockSpec((None, blk, LANES), lambda p, i, j: (i, j, p)),
                  pl.BlockSpec((None, LANES, s), lambda p, i, j: (i, p, 0)),
                  pl.BlockSpec((None, LANES, s), lambda p, i, j: (i, p, 0)),
                  pl.BlockSpec((None, blk, LANES), lambda p, i, j: (i, j, p)),
                  pl.BlockSpec((s, LANES), lambda p, i, j: (0, 0)),
                  pl.BlockSpec((2, blk, blk), lambda p, i, j: (0, 0, 0))],
        out_specs=pl.BlockSpec((None, blk, LANES), lambda p, i, j: (i, j, p)),
        scratch_shapes=[pltpu.VMEM((2, 2, blk, blk), F32), pltpu.VMEM((LANES, LANES), F32),
                        pltpu.VMEM((nb, blk, blk), F32), pltpu.VMEM((blk, 1), F32)])
    return pl.pallas_call(
        functools.partial(_attn_kernel, hd=hd),
        out_shape=jax.ShapeDtypeStruct((b, s, w), BF16),
        grid_spec=grid_spec,
        compiler_params=_params("arbitrary", "arbitrary", "arbitrary"),
        name="attn_prompt",
    )(rel_bias, q, kt, vt, ga, avg, bkt)


def _outproj_kernel(x_ref, mod_ref, ml_ref, ma_ref, wo_ref, gf_ref, y_ref, *, d, w):
    acc = jnp.dot(ml_ref[...].astype(BF16), wo_ref[0:w, :], preferred_element_type=F32)
    acc = acc + jnp.dot(ma_ref[...].astype(BF16), wo_ref[w:2 * w, :], preferred_element_type=F32)
    out = x_ref[...] + mod_ref[:, 2 * d:3 * d] * acc
    ms = jnp.mean(out * out, axis=-1, keepdims=True)
    y_ref[...] = out * lax.rsqrt(ms + RMS_EPS) * gf_ref[...]


def _outproj_prompt(x, mod3, mix_l, mix_a, w_out, g_final):
    b, s, d = x.shape
    w = mix_l.shape[-1]
    tm = ROW_TILE
    return pl.pallas_call(
        functools.partial(_outproj_kernel, d=d, w=w),
        out_shape=jax.ShapeDtypeStruct((b, s, d), F32),
        grid=(b, s // tm),
        in_specs=[pl.BlockSpec((None, tm, d), lambda i, t: (i, t, 0)),
                  pl.BlockSpec((None, 1, 3 * d), lambda i, t: (i, 0, 0)),
                  pl.BlockSpec((None, tm, w), lambda i, t: (i, t, 0)),
                  pl.BlockSpec((None, tm, w), lambda i, t: (i, t, 0)),
                  pl.BlockSpec((2 * w, d), lambda i, t: (0, 0)),
                  pl.BlockSpec((1, d), lambda i, t: (0, 0))],
        out_specs=pl.BlockSpec((None, tm, d), lambda i, t: (i, t, 0)),
        compiler_params=_params("arbitrary", "arbitrary"),
        name="outproj_prompt",
    )(x, mod3, mix_l, mix_a, w_out, g_final)


def _sample_proj_kernel(x_ref, mod_ref, g_ref, win_ref, wqkt_ref, cbuf_ref, h0_ref, cw_ref, cb_ref,
                        wg_ref, bg_ref, ap_ref,
                        qt_ref, kt_ref, k_ref, v_ref, h_ref, cnew_ref, ml_ref, ga_ref, *, d, w):
    n = x_ref.shape[0]
    xm = _modulated_norm(x_ref[...], mod_ref[...], g_ref[...], d).astype(BF16)
    pr = jnp.dot(xm, win_ref[...], preferred_element_type=F32)
    u = pr[:, 0:w]
    k_ref[...] = pr[:, 3 * w:4 * w]
    v_ref[...] = pr[:, 4 * w:5 * w]
    ga_ref[...] = pr[:, 5 * w:6 * w]
    xpad = jnp.concatenate([xm, jnp.zeros((LANES - n, d), BF16)], axis=0)
    pt = lax.dot_general(wqkt_ref[...], xpad, (((1,), (1,)), ((), ())),
                         preferred_element_type=F32)
    qt_ref[...] = pt[0:w]
    kt_ref[...] = pt[w:2 * w]
    cw = cw_ref[...]
    u_conv = cb_ref[...] + (cbuf_ref[0] * cw[0:1] + cbuf_ref[1] * cw[1:2]
                            + cbuf_ref[2] * cw[2:3] + u * cw[3:4])
    a, bx = _lru_gates(u_conv, wg_ref, bg_ref, ap_ref, w)
    h = a * h0_ref[...] + bx
    h_ref[...] = h
    cnew_ref[0] = cbuf_ref[1]
    cnew_ref[1] = cbuf_ref[2]
    cnew_ref[2] = u
    ml_ref[...] = h * _silu(pr[:, w:2 * w])


def _sample_proj(x, mod, g_norm, w_in_bf, w_qkt, cbuf, h0, conv_w, conv_b, wg, bg, a_param):
    n, d = x.shape
    w = h0.shape[1]
    row = jax.ShapeDtypeStruct((n, w), F32)
    col = jax.ShapeDtypeStruct((w, LANES), F32)
    return pl.pallas_call(
        functools.partial(_sample_proj_kernel, d=d, w=w),
        out_shape=[col, col, row, row, row, jax.ShapeDtypeStruct((CONV_W - 1, n, w), F32), row, row],
        compiler_params=pltpu.CompilerParams(vmem_limit_bytes=VMEM_LIMIT_BYTES),
        name="sample_proj",
    )(x, mod, g_norm, w_in_bf, w_qkt, cbuf, h0, conv_w, conv_b, wg, bg, a_param)


def _score_kernel(pt_ref, qt_ref, kc_ref, s_ref, idx_ref, val_ref,
                  kbuf, sem, qb_scr, g_scr, *, n_pages, n_seq, hd):
    b = pl.program_id(0)
    cp = PAGES_PER_CHUNK
    n_chunks = n_pages // cp
    total = n_seq * n_chunks
    bpc = cp // 2

    def page_copy(g, jj, slot):
        return pltpu.make_async_copy(kc_ref.at[pt_ref[g * cp + jj]], kbuf.at[slot, jj], sem.at[slot])

    def start_chunk(g):
        slot = lax.rem(g, K_SLOTS)
        for jj in range(cp):
            page_copy(g, jj, slot).start()

    def wait_chunk(g):
        slot = lax.rem(g, K_SLOTS)
        for jj in range(cp):
            page_copy(g, jj, slot).wait()

    @pl.when(b == 0)
    def _():
        for g in range(K_SLOTS - 1):
            start_chunk(g)

    lane = lax.broadcasted_iota(I32, qt_ref.shape, 1)
    qcol = jnp.sum(jnp.where(lane == b, qt_ref[...], 0.0), axis=1, keepdims=True)
    qb_scr[...] = jnp.broadcast_to(qcol, qb_scr.shape)

    def chunk(c, carry):
        g = b * n_chunks + c

        @pl.when(g + K_SLOTS - 1 < total)
        def _():
            start_chunk(g + K_SLOTS - 1)

        wait_chunk(g)
        slot = lax.rem(g, K_SLOTS)
        for bb in range(bpc):
            n = c * bpc + bb
            tot = None
            for pp in range(2):
                prod = kbuf[slot, 2 * bb + pp] * qb_scr[...]
                sp = jnp.sum(prod.reshape(N_HEADS, hd, LANES), axis=1)
                s_ref[n, :, pp * LANES:(pp + 1) * LANES] = sp
                tot = sp if tot is None else tot + sp
            bsum = jnp.sum(tot, axis=1, keepdims=True)
            g_scr[n] = jnp.broadcast_to(bsum, (N_HEADS, LANES))
        return carry

    lax.fori_loop(0, n_chunks, chunk, 0)

    gate = g_scr[...] * (1.0 / MOBA_BLOCK)
    nidx = lax.broadcasted_iota(I32, gate.shape, 0)
    nb = gate.shape[0]
    for r in range(MOBA_TOPK):
        mx = jnp.max(gate, axis=0)
        am = jnp.min(jnp.where(gate == mx[None], nidx, nb), axis=0)
        idx_ref[r] = am
        val_ref[r] = mx
        gate = jnp.where(nidx == am[None], -jnp.inf, gate)


def _score(pt_flat, qt, kcache_t, n_seq, n_pages):
    w = qt.shape[0]
    hd = w // N_HEADS
    nb = n_pages // 2
    grid_spec = pltpu.PrefetchScalarGridSpec(
        num_scalar_prefetch=1,
        grid=(n_seq,),
        in_specs=[pl.BlockSpec((w, LANES), lambda b, pt: (0, 0)),
                  pl.BlockSpec(memory_space=pl.ANY)],
        out_specs=[pl.BlockSpec((None, nb, N_HEADS, MOBA_BLOCK), lambda b, pt: (b, 0, 0, 0)),
                   pl.BlockSpec((None, MOBA_TOPK, N_HEADS, LANES), lambda b, pt: (b, 0, 0, 0)),
                   pl.BlockSpec((None, MOBA_TOPK, N_HEADS, LANES), lambda b, pt: (b, 0, 0, 0))],
        scratch_shapes=[pltpu.VMEM((K_SLOTS, PAGES_PER_CHUNK, w, LANES), F32),
                        pltpu.SemaphoreType.DMA((K_SLOTS,)),
                        pltpu.VMEM((w, LANES), F32),
                        pltpu.VMEM((nb, N_HEADS, LANES), F32)])
    return pl.pallas_call(
        functools.partial(_score_kernel, n_pages=n_pages, n_seq=n_seq, hd=hd),
        out_shape=[jax.ShapeDtypeStruct((n_seq, nb, N_HEADS, MOBA_BLOCK), F32),
                   jax.ShapeDtypeStruct((n_seq, MOBA_TOPK, N_HEADS, LANES), I32),
                   jax.ShapeDtypeStruct((n_seq, MOBA_TOPK, N_HEADS, LANES), F32)],
        grid_spec=grid_spec,
        compiler_params=_params("arbitrary"),
        name="decode_score",
    )(pt_flat, qt, kcache_t)


def _decode_kernel(pt_ref, ix_ref, s_ref, idxv_ref, valv_ref, qt_ref, kt_ref, vnew_ref, rbt_ref,
                   bkt_ref, vc_ref, y_ref, vbuf, sem, *, n_pages, n_seq, hd):
    b = pl.program_id(0)
    nsel = MOBA_TOPK
    blk = MOBA_BLOCK
    nb = n_pages // 2

    def tile_copy(bb, r, h, pp, slot):
        page = pt_ref[bb * n_pages + 2 * ix_ref[(bb * nsel + r) * N_HEADS + h] + pp]
        return pltpu.make_async_copy(
            vc_ref.at[page, h], vbuf.at[slot, h, :, pl.ds((2 * r + pp) * LANES, LANES)], sem.at[slot])

    def for_tiles(bb, fn):
        slot = lax.rem(bb, 2)
        for r in range(nsel):
            for h in range(N_HEADS):
                for pp in range(2):
                    fn(tile_copy(bb, r, h, pp, slot))

    @pl.when(b == 0)
    def _():
        for_tiles(0, lambda cp: cp.start())

    @pl.when(b + 1 < n_seq)
    def _():
        for_tiles(b + 1, lambda cp: cp.start())

    sub = lax.broadcasted_iota(I32, (N_HEADS, blk), 0)
    rbt = rbt_ref[...]
    bias_last = jnp.zeros((N_HEADS, blk), F32)
    for k in range(REL_BUCKETS):
        bias_last = jnp.where(bkt_ref[...] == k, rbt[:, k:k + 1], bias_last)
    bias_far = rbt[:, REL_BUCKETS - 1:REL_BUCKETS]

    logits = []
    for r in range(nsel):
        lr = jnp.zeros((N_HEADS, blk), F32)
        for h in range(N_HEADS):
            n = ix_ref[(b * nsel + r) * N_HEADS + h]
            lr = jnp.where(sub == h, s_ref[n], lr)
        idv = idxv_ref[r][:, 0:1]
        bias = jnp.where(idv == nb - 1, bias_last, bias_far)
        valid = valv_ref[r][:, 0:1] > 0.5 * NEG_INF
        logits.append(jnp.where(valid, lr * (hd ** -0.5) + bias, NEG_INF))

    lane = lax.broadcasted_iota(I32, (N_HEADS, LANES), 1)
    prod = (qt_ref[...] * kt_ref[...]).reshape(N_HEADS, hd, LANES)
    own = jnp.sum(jnp.where(lane == b, jnp.sum(prod, axis=1), 0.0), axis=1, keepdims=True)
    l_own = own * (hd ** -0.5) + rbt[:, 0:1]

    m = l_own
    for lr in logits:
        m = jnp.maximum(m, jnp.max(lr, axis=1, keepdims=True))
    p_own = jnp.exp(l_own - m)
    den = p_own
    probs = []
    for lr in logits:
        pr = jnp.exp(lr - m)
        den = den + jnp.sum(pr, axis=1, keepdims=True)
        probs.append(pr)
    pcat = jnp.concatenate(probs, axis=1)

    for_tiles(b, lambda cp: cp.wait())
    slot = lax.rem(b, 2)
    rows = []
    for h in range(N_HEADS):
        vt = vbuf[slot, h].astype(BF16)
        rows.append(lax.dot_general(pcat[h:h + 1, :].astype(BF16), vt, (((1,), (1,)), ((), ())),
                                    preferred_element_type=F32))
    acc = jnp.concatenate(rows, axis=0)
    y_ref[...] = (acc + p_own * vnew_ref[...]) / den


def _decode(pt_flat, idx_flat, s_all, idxv, valv, qt, kt, v_new, rbt, bkt_last, vcache_t, n_seq, n_pages):
    w = qt.shape[0]
    hd = w // N_HEADS
    nb = n_pages // 2
    zero4 = lambda b, pt, ix: (b, 0, 0, 0)
    grid_spec = pltpu.PrefetchScalarGridSpec(
        num_scalar_prefetch=2,
        grid=(n_seq,),
        in_specs=[pl.BlockSpec((None, nb, N_HEADS, MOBA_BLOCK), zero4),
                  pl.BlockSpec((None, MOBA_TOPK, N_HEADS, LANES), zero4),
                  pl.BlockSpec((None, MOBA_TOPK, N_HEADS, LANES), zero4),
                  pl.BlockSpec((w, LANES), lambda b, pt, ix: (0, 0)),
                  pl.BlockSpec((w, LANES), lambda b, pt, ix: (0, 0)),
                  pl.BlockSpec((None, N_HEADS, hd), lambda b, pt, ix: (b, 0, 0)),
                  pl.BlockSpec((N_HEADS, REL_BUCKETS), lambda b, pt, ix: (0, 0)),
                  pl.BlockSpec((1, MOBA_BLOCK), lambda b, pt, ix: (0, 0)),
                  pl.BlockSpec(memory_space=pl.ANY)],
        out_specs=pl.BlockSpec((None, N_HEADS, hd), lambda b, pt, ix: (b, 0, 0)),
        scratch_shapes=[pltpu.VMEM((2, N_HEADS, hd, MOBA_TOPK * MOBA_BLOCK), F32),
                        pltpu.SemaphoreType.DMA((2,))])
    return pl.pallas_call(
        functools.partial(_decode_kernel, n_pages=n_pages, n_seq=n_seq, hd=hd),
        out_shape=jax.ShapeDtypeStruct((n_seq, N_HEADS, hd), F32),
        grid_spec=grid_spec,
        compiler_params=_params("arbitrary"),
        name="decode_attn",
    )(pt_flat, idx_flat, s_all, idxv, valv, qt, kt, v_new, rbt, bkt_last, vcache_t)


def _outproj_sample_kernel(x_ref, mod_ref, ml_ref, ya_ref, ga_ref, wo_ref, gf_ref, y_ref, *, d, w):
    ma = ya_ref[...] * _silu(ga_ref[...])
    acc = jnp.dot(ml_ref[...].astype(BF16), wo_ref[0:w, :], preferred_element_type=F32)
    acc = acc + jnp.dot(ma.astype(BF16), wo_ref[w:2 * w, :], preferred_element_type=F32)
    out = x_ref[...] + mod_ref[:, 2 * d:3 * d] * acc
    ms = jnp.mean(out * out, axis=-1, keepdims=True)
    y_ref[...] = out * lax.rsqrt(ms + RMS_EPS) * gf_ref[...]


def _outproj_sample(x, mod, mix_l, y_att, ga, w_out, g_final):
    n, d = x.shape
    w = mix_l.shape[1]
    return pl.pallas_call(
        functools.partial(_outproj_sample_kernel, d=d, w=w),
        out_shape=jax.ShapeDtypeStruct((n, d), F32),
        compiler_params=pltpu.CompilerParams(vmem_limit_bytes=VMEM_LIMIT_BYTES),
        name="outproj_sample",
    )(x, mod, mix_l, y_att, ga, w_out, g_final)


def _block_diag(wb):
    n, c, dd = wb.shape
    return jnp.einsum("ncd,nm->ncmd", wb, jnp.eye(n, dtype=wb.dtype)).reshape(n * c, n * dd)


def kernel(x_prompt, x_sample, cache_k, cache_v, state_lru_h, state_lru_conv, page_table, c_prompt, c_sample, w_ada, b_ada, g_norm, w_in, conv_w, conv_b, w_rgate, b_rgate, w_igate, b_igate, lru_a_param, rel_bias, w_out, g_final):
    bp, s, d = x_prompt.shape
    ns = x_sample.shape[0]
    w = state_lru_h.shape[1]
    hd = w // N_HEADS
    n_pages = page_table.shape[1]
    page = cache_k.shape[1]
    past = n_pages * page
    assert x_sample.shape[1] == 1 and 2 * page == MOBA_BLOCK and past % MOBA_BLOCK == 0
    assert s % MOBA_BLOCK == 0 and s % ROW_TILE == 0 and s % LRU_CHUNK == 0
    assert w % LANES == 0 and ns <= LANES and n_pages % PAGES_PER_CHUNK == 0
    assert MOBA_BLOCK + 1 >= REL_MAX_DIST

    w_in_bf = w_in.astype(BF16)
    w_row = jnp.concatenate([w_in_bf[:, 0:3 * w], w_in_bf[:, 5 * w:6 * w]], axis=1)
    w_kvt = w_in_bf[:, 3 * w:5 * w].T
    w_qkt = w_in_bf[:, 2 * w:4 * w].T
    w_out_bf = w_out.astype(BF16)
    wg = jnp.concatenate([_block_diag(w_rgate), _block_diag(w_igate)], axis=1).astype(BF16)
    bg = jnp.concatenate([b_rgate, b_igate]).reshape(1, 2 * w)
    g_norm2 = g_norm.reshape(1, d)
    g_final2 = g_final.reshape(1, d)
    conv_b2 = conv_b.reshape(1, w)
    a_param2 = lru_a_param.reshape(1, w)

    mod = _adaln(jnp.concatenate([c_prompt, c_sample], axis=0), w_ada, b_ada)
    mod_p = mod[0:bp].reshape(bp, 1, 3 * d)
    mod_s = mod[bp:bp + ns]

    u, gl, q, ga, kt, vt = _inproj(x_prompt, mod_p, g_norm2, w_row, w_kvt)
    mix_l, h_last, tail = _lru_prompt(u, gl, conv_w, conv_b2, wg, bg, a_param2)
    mix_a = _attn_prompt(q, kt, vt, ga, rel_bias)
    y_prompt = _outproj_prompt(x_prompt, mod_p, mix_l, mix_a, w_out_bf, g_final2)
    k_prompt = kt.reshape(bp, N_HEADS, hd, s).transpose(0, 3, 1, 2)
    v_prompt = vt.reshape(bp, N_HEADS, hd, s).transpose(0, 3, 1, 2)
    lru_h_prompt = h_last[:, 0, :]
    lru_conv_prompt = tail[:, SUBLANES - (CONV_W - 1):, :]

    xs = x_sample.reshape(ns, d)
    cbuf = state_lru_conv.transpose(1, 0, 2)
    qt, ktn, k_s, v_s, h_s, cnew, mix_ls, ga_s = _sample_proj(
        xs, mod_s, g_norm2, w_in_bf, w_qkt, cbuf, state_lru_h, conv_w, conv_b2, wg, bg, a_param2)
    pt_flat = page_table.reshape(-1)
    n_phys = cache_k.shape[0]
    kc_t = cache_k.transpose(0, 2, 3, 1).reshape(n_phys, w, page)
    vc_t = cache_v.transpose(0, 2, 3, 1)
    s_all, idxv, valv = _score(pt_flat, qt, kc_t, ns, n_pages)
    idx_flat = idxv[:, :, :, 0].reshape(-1)
    nb = n_pages // 2
    pos = (nb - 1) * MOBA_BLOCK + jnp.arange(MOBA_BLOCK, dtype=I32)
    bkt_last = _t5_bucket(past - pos).reshape(1, MOBA_BLOCK)
    y_att = _decode(pt_flat, idx_flat, s_all, idxv, valv, qt, ktn, v_s.reshape(ns, N_HEADS, hd),
                    rel_bias.T, bkt_last, vc_t, ns, n_pages)
    y_s = _outproj_sample(xs, mod_s, mix_ls, y_att.reshape(ns, w), ga_s, w_out_bf, g_final2)

    return (y_prompt, y_s.reshape(ns, 1, d), k_prompt, v_prompt, lru_h_prompt, lru_conv_prompt,
            k_s.reshape(ns, 1, N_HEADS, hd), v_s.reshape(ns, 1, N_HEADS, hd), h_s,
            cnew.transpose(1, 0, 2))
```

```python
import functools
import math

import jax
import jax.numpy as jnp
from jax import lax
from jax.experimental import pallas as pl
from jax.experimental.pallas import tpu as pltpu

F32 = jnp.float32
BF16 = jnp.bfloat16
I32 = jnp.int32

N_HEADS = 8
LRU_BLOCKS = 8
CONV_W = 4
LRU_C = 8.0
MOBA_BLOCK = 256
MOBA_TOPK = 3
REL_BUCKETS = 32
REL_MAX_DIST = 128
RMS_EPS = 1e-6
NEG_INF = -1e30
HIGHEST = lax.Precision.HIGHEST
LOG2E = math.log2(math.e)

LANES = 128
SUBLANES = 8
VMEM_LIMIT_BYTES = 56 * 1024 * 1024

ROW_TILE = 512
LRU_CHUNK = 512
PAGES_PER_CHUNK = 8
K_SLOTS = 4


def _sigmoid(x):
    return 0.5 * jnp.tanh(0.5 * x) + 0.5


def _silu(x):
    return x * _sigmoid(x)


def _softplus(z):
    return jnp.maximum(z, 0.0) + jnp.log1p(jnp.exp(-jnp.abs(z)))


def _t5_bucket(dist):
    n = jnp.maximum(dist, 0)
    max_exact = REL_BUCKETS // 2
    nf = jnp.maximum(n, 1).astype(F32)
    large = max_exact + (jnp.log(nf / max_exact) / math.log(REL_MAX_DIST / max_exact)
                         * (REL_BUCKETS - max_exact)).astype(I32)
    large = jnp.minimum(large, REL_BUCKETS - 1)
    return jnp.where(n < max_exact, n, large)


def _params(*sem):
    return pltpu.CompilerParams(dimension_semantics=sem, vmem_limit_bytes=VMEM_LIMIT_BYTES)


def _adaln_kernel(c_ref, w_ref, b_ref, o_ref):
    s = _silu(c_ref[...]).astype(BF16)
    o_ref[...] = jnp.dot(s, w_ref[...].astype(BF16), preferred_element_type=F32) + b_ref[...]


def _adaln(c, w_ada, b_ada):
    n, d = c.shape
    return pl.pallas_call(
        _adaln_kernel,
        out_shape=jax.ShapeDtypeStruct((n, 3 * d), F32),
        grid=(3,),
        in_specs=[pl.BlockSpec((n, d), lambda j: (0, 0)),
                  pl.BlockSpec((d, d), lambda j: (0, j)),
                  pl.BlockSpec((1, d), lambda j: (0, j))],
        out_specs=pl.BlockSpec((n, d), lambda j: (0, j)),
        compiler_params=_params("arbitrary"),
        name="adaln",
    )(c, w_ada, b_ada.reshape(1, 3 * d))


def _modulated_norm(x, mod, g, d):
    ms = jnp.mean(x * x, axis=-1, keepdims=True)
    xn = x * lax.rsqrt(ms + RMS_EPS) * g
    return xn * (1.0 + mod[:, d:2 * d]) + mod[:, 0:d]


def _inproj_kernel(x_ref, mod_ref, g_ref, wrow_ref, wkvt_ref,
                   u_ref, gl_ref, q_ref, ga_ref, kt_ref, vt_ref, *, d, w):
    xm = _modulated_norm(x_ref[...], mod_ref[...], g_ref[...], d).astype(BF16)
    pr = jnp.dot(xm, wrow_ref[...], preferred_element_type=F32)
    u_ref[...] = pr[:, 0:w]
    gl_ref[...] = pr[:, w:2 * w]
    q_ref[...] = pr[:, 2 * w:3 * w]
    ga_ref[...] = pr[:, 3 * w:4 * w]
    pt = lax.dot_general(wkvt_ref[...], xm, (((1,), (1,)), ((), ())),
                         preferred_element_type=F32)
    kt_ref[...] = pt[0:w]
    vt_ref[...] = pt[w:2 * w]


def _inproj(x, mod3, g_norm, w_row, w_kvt):
    b, s, d = x.shape
    w = w_row.shape[1] // 4
    tm = ROW_TILE
    row = pl.BlockSpec((None, tm, w), lambda i, t: (i, t, 0))
    col = pl.BlockSpec((None, w, tm), lambda i, t: (i, 0, t))
    return pl.pallas_call(
        functools.partial(_inproj_kernel, d=d, w=w),
        out_shape=[jax.ShapeDtypeStruct((b, s, w), F32)] * 4
        + [jax.ShapeDtypeStruct((b, w, s), F32)] * 2,
        grid=(b, s // tm),
        in_specs=[pl.BlockSpec((None, tm, d), lambda i, t: (i, t, 0)),
                  pl.BlockSpec((None, 1, 3 * d), lambda i, t: (i, 0, 0)),
                  pl.BlockSpec((1, d), lambda i, t: (0, 0)),
                  pl.BlockSpec((d, 4 * w), lambda i, t: (0, 0)),
                  pl.BlockSpec((2 * w, d), lambda i, t: (0, 0))],
        out_specs=[row, row, row, row, col, col],
        compiler_params=_params("arbitrary", "arbitrary"),
        name="inproj",
    )(x, mod3, g_norm, w_row, w_kvt)


def _lru_gates(u_conv, wg_ref, bg_ref, ap_ref, w):
    g2 = jnp.dot(u_conv.astype(BF16), wg_ref[...], preferred_element_type=F32) + bg_ref[...]
    r = _sigmoid(g2[:, 0:w])
    i = _sigmoid(g2[:, w:2 * w])
    log_a = (-LRU_C * r) * _softplus(-ap_ref[...])
    a = jnp.exp(log_a)
    bx = jnp.sqrt(-jnp.tanh(log_a) * (a * a + 1.0)) * (i * u_conv)
    return a, bx


def _lru_kernel(u_ref, gl_ref, cw_ref, cb_ref, wg_ref, bg_ref, ap_ref,
                mix_ref, hl_ref, tail_ref, a_scr, b_scr, hs_scr, h_scr, tail_scr, *, tc, w):
    c = pl.program_id(1)

    @pl.when(c == 0)
    def _():
        h_scr[...] = jnp.zeros_like(h_scr)
        tail_scr[...] = jnp.zeros_like(tail_scr)

    u = u_ref[...]
    ext = jnp.concatenate([tail_scr[...], u], axis=0)
    cw = cw_ref[...]
    u_conv = cb_ref[...] + (ext[5:5 + tc] * cw[0:1] + ext[6:6 + tc] * cw[1:2]
                            + ext[7:7 + tc] * cw[2:3] + u * cw[3:4])
    tail_scr[...] = u[tc - SUBLANES:tc]
    a, bx = _lru_gates(u_conv, wg_ref, bg_ref, ap_ref, w)
    a_scr[...] = a
    b_scr[...] = bx

    row = lax.broadcasted_iota(I32, (SUBLANES, w), 0)

    def tile(t, h):
        r0 = pl.multiple_of(t * SUBLANES, SUBLANES)
        at = a_scr[pl.ds(r0, SUBLANES), :]
        bt = b_scr[pl.ds(r0, SUBLANES), :]
        for dd in (1, 2, 4):
            keep = row >= dd
            bt = jnp.where(keep, at * pltpu.roll(bt, dd, 0) + bt, bt)
            at = jnp.where(keep, at * pltpu.roll(at, dd, 0), at)
        hs = at * h + bt
        hs_scr[pl.ds(r0, SUBLANES), :] = hs
        return jnp.broadcast_to(hs[SUBLANES - 1:SUBLANES, :], (SUBLANES, w))

    h = lax.fori_loop(0, tc // SUBLANES, tile, h_scr[...], unroll=4)
    h_scr[...] = h
    hl_ref[...] = h
    tail_ref[...] = u[tc - SUBLANES:tc]
    mix_ref[...] = (hs_scr[...] * _silu(gl_ref[...])).astype(mix_ref.dtype)


def _lru_prompt(u, gl, conv_w, conv_b, wg, bg, a_param):
    b, s, w = u.shape
    tc = LRU_CHUNK
    const = lambda shape: pl.BlockSpec(shape, lambda i, t: (0,) * len(shape))
    return pl.pallas_call(
        functools.partial(_lru_kernel, tc=tc, w=w),
        out_shape=[jax.ShapeDtypeStruct((b, s, w), BF16),
                   jax.ShapeDtypeStruct((b, SUBLANES, w), F32),
                   jax.ShapeDtypeStruct((b, SUBLANES, w), F32)],
        grid=(b, s // tc),
        in_specs=[pl.BlockSpec((None, tc, w), lambda i, t: (i, t, 0)),
                  pl.BlockSpec((None, tc, w), lambda i, t: (i, t, 0)),
                  const((CONV_W, w)), const((1, w)), const((w, 2 * w)), const((1, 2 * w)),
                  const((1, w))],
        out_specs=[pl.BlockSpec((None, tc, w), lambda i, t: (i, t, 0)),
                   pl.BlockSpec((None, SUBLANES, w), lambda i, t: (i, 0, 0)),
                   pl.BlockSpec((None, SUBLANES, w), lambda i, t: (i, 0, 0))],
        scratch_shapes=[pltpu.VMEM((tc, w), F32), pltpu.VMEM((tc, w), F32),
                        pltpu.VMEM((tc, w), F32), pltpu.VMEM((SUBLANES, w), F32),
                        pltpu.VMEM((SUBLANES, w), F32)],
        compiler_params=_params("arbitrary", "arbitrary"),
        name="lru_prompt",
    )(u, gl, conv_w, conv_b, wg, bg, a_param)


_NT = (((1,), (1,)), ((), ()))


def _attn_kernel(rb_ref, q_ref, kt_ref, vt_ref, ga_ref, bkt_ref, o_ref,
                 bias_scr, kaug_scr, vb_scr, mpad_scr, qm_scr, lg_scr, *, hd, nb):
    p = pl.program_id(0)
    b = pl.program_id(1)
    blk = MOBA_BLOCK
    s = nb * blk

    @pl.when(b == 0)
    def _():
        qi = lax.broadcasted_iota(I32, (blk, blk), 0)
        ki = lax.broadcasted_iota(I32, (blk, blk), 1)
        for hh in range(2):
            h = 2 * p + hh
            far = rb_ref[REL_BUCKETS - 1, h]
            for which in range(2):
                bk = bkt_ref[which]
                tab = jnp.zeros((blk, blk), F32)
                for k in range(REL_BUCKETS):
                    tab = jnp.where(bk == k, (rb_ref[k, h] - far) * LOG2E, tab)
                if which == 0:
                    tab = jnp.where(qi >= ki, tab, NEG_INF)
                bias_scr[hh, which] = tab

    n_i = lax.broadcasted_iota(I32, (SUBLANES, s), 0)
    j_i = lax.broadcasted_iota(I32, (SUBLANES, s), 1) // blk
    ind = jnp.where(n_i == j_i, 1.0, 0.0)
    kt = kt_ref[...]
    km8 = lax.dot_general(ind.astype(BF16), kt.astype(BF16), _NT,
                          preferred_element_type=F32) * (1.0 / blk)
    row_v = lax.broadcasted_iota(I32, (LANES, s), 0)
    vt = vt_ref[...]
    vb_scr[0] = jnp.where(row_v < hd, vt, 1.0).astype(BF16)
    vb_scr[1] = jnp.where(row_v >= hd, vt, 1.0).astype(BF16)
    qs = q_ref[...] * (hd ** -0.5 * LOG2E)
    lane_q = lax.broadcasted_iota(I32, (s, LANES), 1)
    zeros56 = jnp.zeros((hd - SUBLANES, s), F32)
    eye = jnp.where(lax.broadcasted_iota(I32, (blk, blk), 0) == lax.broadcasted_iota(I32, (blk, blk), 1),
                    1.0, 0.0).astype(BF16)

    for hh in range(2):
        head_lanes = (lane_q >= hd * hh) & (lane_q < hd * (hh + 1))
        qm = jnp.where(head_lanes, qs, 0.0).astype(BF16)
        qm_scr[hh] = qm
        gt = lax.dot_general(km8.astype(BF16), qm, _NT, preferred_element_type=F32)
        past = n_i < j_i
        g = jnp.where(past, gt, NEG_INF)
        cnt = jnp.zeros((SUBLANES, s), I32)
        for m in range(nb - 1):
            gm = g[m:m + 1, :]
            cnt = cnt + jnp.where((gm > g) | ((gm == g) & (n_i > m)), 1, 0)
        keep = (past & (cnt < MOBA_TOPK) & (g > 0.5 * NEG_INF * (hd ** -0.5 * LOG2E))) | (n_i >= j_i)
        mask_t = jnp.where(keep, 0.0, NEG_INF)
        if hh == 0:
            kaug = jnp.concatenate([kt[0:hd], ind, zeros56], axis=0)
            mpad = jnp.concatenate([jnp.zeros((hd, s), F32), mask_t, zeros56], axis=0)
        else:
            kaug = jnp.concatenate([ind, zeros56, kt[hd:2 * hd]], axis=0)
            mpad = jnp.concatenate([mask_t, zeros56, jnp.zeros((hd, s), F32)], axis=0)
        kaug_scr[hh] = kaug.astype(BF16)
        mpad_scr[hh] = mpad.astype(BF16)

    lane = lax.broadcasted_iota(I32, (blk, LANES), 1)
    step = 0
    for j in reversed(range(nb)):
        rows = slice(j * blk, (j + 1) * blk)
        accs = []
        for hh in range(2):
            par = step % 2
            step += 1
            mf = lax.dot_general(eye, mpad_scr[hh, :, rows], _NT, preferred_element_type=F32)
            qa = qm_scr[hh, rows, :] + mf.astype(BF16)
            macc = None
            for n in range(j + 1):
                sc = jnp.dot(qa, kaug_scr[hh, :, n * blk:(n + 1) * blk],
                             preferred_element_type=F32)
                if n == j:
                    sc = sc + bias_scr[hh, 0]
                elif n == j - 1:
                    sc = sc + bias_scr[hh, 1]
                lg_scr[par, n] = sc
                mx = jnp.maximum(sc[:, 0:LANES], sc[:, LANES:2 * LANES])
                macc = mx if macc is None else jnp.maximum(macc, mx)
            mrow = jnp.max(macc, axis=1, keepdims=True)
            acc = None
            for n in range(j + 1):
                pn = jnp.exp2(lg_scr[par, n] - mrow).astype(BF16)
                pv = lax.dot_general(pn, vb_scr[hh, :, n * blk:(n + 1) * blk], _NT,
                                     preferred_element_type=F32)
                acc = pv if acc is None else acc + pv
            accs.append(acc)
        first = lane < hd
        num = jnp.where(first, accs[0], accs[1])
        den = pltpu.roll(jnp.where(first, accs[1], accs[0]), hd, 1)
        o_ref[rows, :] = (num / den * _silu(ga_ref[rows, :])).astype(o_ref.dtype)


def _attn_prompt(q, kt, vt, ga, rel_bias):
    b, s, w = q.shape
    hd = w // N_HEADS
    blk = MOBA_BLOCK
    nb = s // blk
    npair = w // LANES
    assert nb <= SUBLANES and LANES == 2 * hd
    dq = jnp.arange(blk, dtype=I32)
    d_own = dq[:, None] - dq[None, :]
    bkt = jnp.stack([_t5_bucket(d_own), _t5_bucket(d_own + blk)])
    grid_spec = pltpu.PrefetchScalarGridSpec(
        num_scalar_prefetch=0,
        grid=(npair, b),
        in_specs=[pl.BlockSpec(memory_space=pltpu.SMEM),
                  pl.BlockSpec((None, s, LANES), lambda p, i: (i, 0, p)),
                  pl.BlockSpec((None, LANES, s), lambda p, i: (i, p, 0)),
                  pl.BlockSpec((None, LANES, s), lambda p, i: (i, p, 0)),
                  pl.BlockSpec((None, s, LANES), lambda p, i: (i, 0, p)),
                  pl.BlockSpec((2, blk, blk), lambda p, i: (0, 0, 0))],
        out_specs=pl.BlockSpec((None, s, LANES), lambda p, i: (i, 0, p)),
        scratch_shapes=[pltpu.VMEM((2, 2, blk, blk), F32),
                        pltpu.VMEM((2, LANES, s), BF16),
                        pltpu.VMEM((2, LANES, s), BF16),
                        pltpu.VMEM((2, LANES, s), BF16),
                        pltpu.VMEM((2, s, LANES), BF16),
                        pltpu.VMEM((2, nb, blk, blk), F32)])
    return pl.pallas_call(
        functools.partial(_attn_kernel, hd=hd, nb=nb),
        out_shape=jax.ShapeDtypeStruct((b, s, w), BF16),
        grid_spec=grid_spec,
        compiler_params=_params("arbitrary", "arbitrary"),
        name="attn_prompt",
    )(rel_bias, q, kt, vt, ga, bkt)


def _outproj_kernel(x_ref, mod_ref, ml_ref, ma_ref, wo_ref, gf_ref, y_ref, *, d, w):
    acc = jnp.dot(ml_ref[...].astype(BF16), wo_ref[0:w, :], preferred_element_type=F32)
    acc = acc + jnp.dot(ma_ref[...].astype(BF16), wo_ref[w:2 * w, :], preferred_element_type=F32)
    out = x_ref[...] + mod_ref[:, 2 * d:3 * d] * acc
    ms = jnp.mean(out * out, axis=-1, keepdims=True)
    y_ref[...] = out * lax.rsqrt(ms + RMS_EPS) * gf_ref[...]


def _outproj_prompt(x, mod3, mix_l, mix_a, w_out, g_final):
    b, s, d = x.shape
    w = mix_l.shape[-1]
    tm = ROW_TILE
    return pl.pallas_call(
        functools.partial(_outproj_kernel, d=d, w=w),
        out_shape=jax.ShapeDtypeStruct((b, s, d), F32),
        grid=(b, s // tm),
        in_specs=[pl.BlockSpec((None, tm, d), lambda i, t: (i, t, 0)),
                  pl.BlockSpec((None, 1, 3 * d), lambda i, t: (i, 0, 0)),
                  pl.BlockSpec((None, tm, w), lambda i, t: (i, t, 0)),
                  pl.BlockSpec((None, tm, w), lambda i, t: (i, t, 0)),
                  pl.BlockSpec((2 * w, d), lambda i, t: (0, 0)),
                  pl.BlockSpec((1, d), lambda i, t: (0, 0))],
        out_specs=pl.BlockSpec((None, tm, d), lambda i, t: (i, t, 0)),
        compiler_params=_params("arbitrary", "arbitrary"),
        name="outproj_prompt",
    )(x, mod3, mix_l, mix_a, w_out, g_final)


def _sample_proj_kernel(x_ref, mod_ref, g_ref, win_ref, wqkt_ref, cbuf_ref, h0_ref, cw_ref, cb_ref,
                        wg_ref, bg_ref, ap_ref,
                        qt_ref, kt_ref, k_ref, v_ref, h_ref, cnew_ref, ml_ref, ga_ref, *, d, w):
    n = x_ref.shape[0]
    xm = _modulated_norm(x_ref[...], mod_ref[...], g_ref[...], d).astype(BF16)
    pr = jnp.dot(xm, win_ref[...], preferred_element_type=F32)
    u = pr[:, 0:w]
    k_ref[...] = pr[:, 3 * w:4 * w]
    v_ref[...] = pr[:, 4 * w:5 * w]
    ga_ref[...] = pr[:, 5 * w:6 * w]
    xpad = jnp.concatenate([xm, jnp.zeros((LANES - n, d), BF16)], axis=0)
    pt = lax.dot_general(wqkt_ref[...], xpad, (((1,), (1,)), ((), ())),
                         preferred_element_type=F32)
    qt_ref[...] = pt[0:w]
    kt_ref[...] = pt[w:2 * w]
    cw = cw_ref[...]
    u_conv = cb_ref[...] + (cbuf_ref[0] * cw[0:1] + cbuf_ref[1] * cw[1:2]
                            + cbuf_ref[2] * cw[2:3] + u * cw[3:4])
    a, bx = _lru_gates(u_conv, wg_ref, bg_ref, ap_ref, w)
    h = a * h0_ref[...] + bx
    h_ref[...] = h
    cnew_ref[0] = cbuf_ref[1]
    cnew_ref[1] = cbuf_ref[2]
    cnew_ref[2] = u
    ml_ref[...] = h * _silu(pr[:, w:2 * w])


def _sample_proj(x, mod, g_norm, w_in_bf, w_qkt, cbuf, h0, conv_w, conv_b, wg, bg, a_param):
    n, d = x.shape
    w = h0.shape[1]
    row = jax.ShapeDtypeStruct((n, w), F32)
    col = jax.ShapeDtypeStruct((w, LANES), F32)
    return pl.pallas_call(
        functools.partial(_sample_proj_kernel, d=d, w=w),
        out_shape=[col, col, row, row, row, jax.ShapeDtypeStruct((CONV_W - 1, n, w), F32), row, row],
        compiler_params=pltpu.CompilerParams(vmem_limit_bytes=VMEM_LIMIT_BYTES),
        name="sample_proj",
    )(x, mod, g_norm, w_in_bf, w_qkt, cbuf, h0, conv_w, conv_b, wg, bg, a_param)


def _score_kernel(pt_ref, qt_ref, kc_ref, s_ref, idx_ref, val_ref,
                  kbuf, sem, qb_scr, g_scr, *, n_pages, n_seq, hd):
    b = pl.program_id(0)
    cp = PAGES_PER_CHUNK
    n_chunks = n_pages // cp
    total = n_seq * n_chunks
    bpc = cp // 2

    def page_copy(g, jj, slot):
        return pltpu.make_async_copy(kc_ref.at[pt_ref[g * cp + jj]], kbuf.at[slot, jj], sem.at[slot])

    def start_chunk(g):
        slot = lax.rem(g, K_SLOTS)
        for jj in range(cp):
            page_copy(g, jj, slot).start()

    def wait_chunk(g):
        slot = lax.rem(g, K_SLOTS)
        for jj in range(cp):
            page_copy(g, jj, slot).wait()

    @pl.when(b == 0)
    def _():
        for g in range(K_SLOTS - 1):
            start_chunk(g)

    lane = lax.broadcasted_iota(I32, qt_ref.shape, 1)
    qcol = jnp.sum(jnp.where(lane == b, qt_ref[...], 0.0), axis=1, keepdims=True)
    qb_scr[...] = jnp.broadcast_to(qcol, qb_scr.shape)

    def chunk(c, carry):
        g = b * n_chunks + c

        @pl.when(g + K_SLOTS - 1 < total)
        def _():
            start_chunk(g + K_SLOTS - 1)

        wait_chunk(g)
        slot = lax.rem(g, K_SLOTS)
        for bb in range(bpc):
            n = c * bpc + bb
            tot = None
            for pp in range(2):
                prod = kbuf[slot, 2 * bb + pp] * qb_scr[...]
                sp = jnp.sum(prod.reshape(N_HEADS, hd, LANES), axis=1)
                s_ref[n, :, pp * LANES:(pp + 1) * LANES] = sp
                tot = sp if tot is None else tot + sp
            bsum = jnp.sum(tot, axis=1, keepdims=True)
            g_scr[n] = jnp.broadcast_to(bsum, (N_HEADS, LANES))
        return carry

    lax.fori_loop(0, n_chunks, chunk, 0)

    gate = g_scr[...] * (1.0 / MOBA_BLOCK)
    nidx = lax.broadcasted_iota(I32, gate.shape, 0)
    nb = gate.shape[0]
    for r in range(MOBA_TOPK):
        mx = jnp.max(gate, axis=0)
        am = jnp.min(jnp.where(gate == mx[None], nidx, nb), axis=0)
        idx_ref[r] = am
        val_ref[r] = mx
        gate = jnp.where(nidx == am[None], -jnp.inf, gate)


def _score(pt_flat, qt, kcache_t, n_seq, n_pages):
    w = qt.shape[0]
    hd = w // N_HEADS
    nb = n_pages // 2
    grid_spec = pltpu.PrefetchScalarGridSpec(
        num_scalar_prefetch=1,
        grid=(n_seq,),
        in_specs=[pl.BlockSpec((w, LANES), lambda b, pt: (0, 0)),
                  pl.BlockSpec(memory_space=pl.ANY)],
        out_specs=[pl.BlockSpec((None, nb, N_HEADS, MOBA_BLOCK), lambda b, pt: (b, 0, 0, 0)),
                   pl.BlockSpec((None, MOBA_TOPK, N_HEADS, LANES), lambda b, pt: (b, 0, 0, 0)),
                   pl.BlockSpec((None, MOBA_TOPK, N_HEADS, LANES), lambda b, pt: (b, 0, 0, 0))],
        scratch_shapes=[pltpu.VMEM((K_SLOTS, PAGES_PER_CHUNK, w, LANES), F32),
                        pltpu.SemaphoreType.DMA((K_SLOTS,)),
                        pltpu.VMEM((w, LANES), F32),
                        pltpu.VMEM((nb, N_HEADS, LANES), F32)])
    return pl.pallas_call(
        functools.partial(_score_kernel, n_pages=n_pages, n_seq=n_seq, hd=hd),
        out_shape=[jax.ShapeDtypeStruct((n_seq, nb, N_HEADS, MOBA_BLOCK), F32),
                   jax.ShapeDtypeStruct((n_seq, MOBA_TOPK, N_HEADS, LANES), I32),
                   jax.ShapeDtypeStruct((n_seq, MOBA_TOPK, N_HEADS, LANES), F32)],
        grid_spec=grid_spec,
        compiler_params=_params("arbitrary"),
        name="decode_score",
    )(pt_flat, qt, kcache_t)


def _decode_kernel(pt_ref, ix_ref, s_ref, idxv_ref, valv_ref, qt_ref, kt_ref, vnew_ref, rbt_ref,
                   bkt_ref, vc_ref, y_ref, vbuf, sem, *, n_pages, n_seq, hd):
    b = pl.program_id(0)
    nsel = MOBA_TOPK
    blk = MOBA_BLOCK
    nb = n_pages // 2

    def tile_copy(bb, r, h, pp, slot):
        page = pt_ref[bb * n_pages + 2 * ix_ref[(bb * nsel + r) * N_HEADS + h] + pp]
        return pltpu.make_async_copy(
            vc_ref.at[page, h], vbuf.at[slot, h, :, pl.ds((2 * r + pp) * LANES, LANES)], sem.at[slot])

    def for_tiles(bb, fn):
        slot = lax.rem(bb, 2)
        for r in range(nsel):
            for h in range(N_HEADS):
                for pp in range(2):
                    fn(tile_copy(bb, r, h, pp, slot))

    @pl.when(b == 0)
    def _():
        for_tiles(0, lambda cp: cp.start())

    @pl.when(b + 1 < n_seq)
    def _():
        for_tiles(b + 1, lambda cp: cp.start())

    sub = lax.broadcasted_iota(I32, (N_HEADS, blk), 0)
    rbt = rbt_ref[...]
    bias_last = jnp.zeros((N_HEADS, blk), F32)
    for k in range(REL_BUCKETS):
        bias_last = jnp.where(bkt_ref[...] == k, rbt[:, k:k + 1], bias_last)
    bias_far = rbt[:, REL_BUCKETS - 1:REL_BUCKETS]

    logits = []
    for r in range(nsel):
        lr = jnp.zeros((N_HEADS, blk), F32)
        for h in range(N_HEADS):
            n = ix_ref[(b * nsel + r) * N_HEADS + h]
            lr = jnp.where(sub == h, s_ref[n], lr)
        idv = idxv_ref[r][:, 0:1]
        bias = jnp.where(idv == nb - 1, bias_last, bias_far)
        valid = valv_ref[r][:, 0:1] > 0.5 * NEG_INF
        logits.append(jnp.where(valid, lr * (hd ** -0.5) + bias, NEG_INF))

    lane = lax.broadcasted_iota(I32, (N_HEADS, LANES), 1)
    prod = (qt_ref[...] * kt_ref[...]).reshape(N_HEADS, hd, LANES)
    own = jnp.sum(jnp.where(lane == b, jnp.sum(prod, axis=1), 0.0), axis=1, keepdims=True)
    l_own = own * (hd ** -0.5) + rbt[:, 0:1]

    m = l_own
    for lr in logits:
        m = jnp.maximum(m, jnp.max(lr, axis=1, keepdims=True))
    p_own = jnp.exp(l_own - m)
    den = p_own
    probs = []
    for lr in logits:
        pr = jnp.exp(lr - m)
        den = den + jnp.sum(pr, axis=1, keepdims=True)
        probs.append(pr)
    pcat = jnp.concatenate(probs, axis=1)

    for_tiles(b, lambda cp: cp.wait())
    slot = lax.rem(b, 2)
    rows = []
    for h in range(N_HEADS):
        vt = vbuf[slot, h].astype(BF16)
        rows.append(lax.dot_general(pcat[h:h + 1, :].astype(BF16), vt, (((1,), (1,)), ((), ())),
                                    preferred_element_type=F32))
    acc = jnp.concatenate(rows, axis=0)
    y_ref[...] = (acc + p_own * vnew_ref[...]) / den


def _decode(pt_flat, idx_flat, s_all, idxv, valv, qt, kt, v_new, rbt, bkt_last, vcache_t, n_seq, n_pages):
    w = qt.shape[0]
    hd = w // N_HEADS
    nb = n_pages // 2
    zero4 = lambda b, pt, ix: (b, 0, 0, 0)
    grid_spec = pltpu.PrefetchScalarGridSpec(
        num_scalar_prefetch=2,
        grid=(n_seq,),
        in_specs=[pl.BlockSpec((None, nb, N_HEADS, MOBA_BLOCK), zero4),
                  pl.BlockSpec((None, MOBA_TOPK, N_HEADS, LANES), zero4),
                  pl.BlockSpec((None, MOBA_TOPK, N_HEADS, LANES), zero4),
                  pl.BlockSpec((w, LANES), lambda b, pt, ix: (0, 0)),
                  pl.BlockSpec((w, LANES), lambda b, pt, ix: (0, 0)),
                  pl.BlockSpec((None, N_HEADS, hd), lambda b, pt, ix: (b, 0, 0)),
                  pl.BlockSpec((N_HEADS, REL_BUCKETS), lambda b, pt, ix: (0, 0)),
                  pl.BlockSpec((1, MOBA_BLOCK), lambda b, pt, ix: (0, 0)),
                  pl.BlockSpec(memory_space=pl.ANY)],
        out_specs=pl.BlockSpec((None, N_HEADS, hd), lambda b, pt, ix: (b, 0, 0)),
        scratch_shapes=[pltpu.VMEM((2, N_HEADS, hd, MOBA_TOPK * MOBA_BLOCK), F32),
                        pltpu.SemaphoreType.DMA((2,))])
    return pl.pallas_call(
        functools.partial(_decode_kernel, n_pages=n_pages, n_seq=n_seq, hd=hd),
        out_shape=jax.ShapeDtypeStruct((n_seq, N_HEADS, hd), F32),
        grid_spec=grid_spec,
        compiler_params=_params("arbitrary"),
        name="decode_attn",
    )(pt_flat, idx_flat, s_all, idxv, valv, qt, kt, v_new, rbt, bkt_last, vcache_t)


def _outproj_sample_kernel(x_ref, mod_ref, ml_ref, ya_ref, ga_ref, wo_ref, gf_ref, y_ref, *, d, w):
    ma = ya_ref[...] * _silu(ga_ref[...])
    acc = jnp.dot(ml_ref[...].astype(BF16), wo_ref[0:w, :], preferred_element_type=F32)
    acc = acc + jnp.dot(ma.astype(BF16), wo_ref[w:2 * w, :], preferred_element_type=F32)
    out = x_ref[...] + mod_ref[:, 2 * d:3 * d] * acc
    ms = jnp.mean(out * out, axis=-1, keepdims=True)
    y_ref[...] = out * lax.rsqrt(ms + RMS_EPS) * gf_ref[...]


def _outproj_sample(x, mod, mix_l, y_att, ga, w_out, g_final):
    n, d = x.shape
    w = mix_l.shape[1]
    return pl.pallas_call(
        functools.partial(_outproj_sample_kernel, d=d, w=w),
        out_shape=jax.ShapeDtypeStruct((n, d), F32),
        compiler_params=pltpu.CompilerParams(vmem_limit_bytes=VMEM_LIMIT_BYTES),
        name="outproj_sample",
    )(x, mod, mix_l, y_att, ga, w_out, g_final)


def _block_diag(wb):
    n, c, dd = wb.shape
    return jnp.einsum("ncd,nm->ncmd", wb, jnp.eye(n, dtype=wb.dtype)).reshape(n * c, n * dd)


def kernel(x_prompt, x_sample, cache_k, cache_v, state_lru_h, state_lru_conv, page_table, c_prompt, c_sample, w_ada, b_ada, g_norm, w_in, conv_w, conv_b, w_rgate, b_rgate, w_igate, b_igate, lru_a_param, rel_bias, w_out, g_final):
    bp, s, d = x_prompt.shape
    ns = x_sample.shape[0]
    w = state_lru_h.shape[1]
    hd = w // N_HEADS
    n_pages = page_table.shape[1]
    page = cache_k.shape[1]
    past = n_pages * page
    assert x_sample.shape[1] == 1 and 2 * page == MOBA_BLOCK and past % MOBA_BLOCK == 0
    assert s % MOBA_BLOCK == 0 and s % ROW_TILE == 0 and s % LRU_CHUNK == 0
    assert w % LANES == 0 and ns <= LANES and n_pages % PAGES_PER_CHUNK == 0
    assert MOBA_BLOCK + 1 >= REL_MAX_DIST

    w_in_bf = w_in.astype(BF16)
    w_row = jnp.concatenate([w_in_bf[:, 0:3 * w], w_in_bf[:, 5 * w:6 * w]], axis=1)
    w_kvt = w_in_bf[:, 3 * w:5 * w].T
    w_qkt = w_in_bf[:, 2 * w:4 * w].T
    w_out_bf = w_out.astype(BF16)
    wg = jnp.concatenate([_block_diag(w_rgate), _block_diag(w_igate)], axis=1).astype(BF16)
    bg = jnp.concatenate([b_rgate, b_igate]).reshape(1, 2 * w)
    g_norm2 = g_norm.reshape(1, d)
    g_final2 = g_final.reshape(1, d)
    conv_b2 = conv_b.reshape(1, w)
    a_param2 = lru_a_param.reshape(1, w)

    mod = _adaln(jnp.concatenate([c_prompt, c_sample], axis=0), w_ada, b_ada)
    mod_p = mod[0:bp].reshape(bp, 1, 3 * d)
    mod_s = mod[bp:bp + ns]

    u, gl, q, ga, kt, vt = _inproj(x_prompt, mod_p, g_norm2, w_row, w_kvt)
    mix_l, h_last, tail = _lru_prompt(u, gl, conv_w, conv_b2, wg, bg, a_param2)
    mix_a = _attn_prompt(q, kt, vt, ga, rel_bias)
    y_prompt = _outproj_prompt(x_prompt, mod_p, mix_l, mix_a, w_out_bf, g_final2)
    k_prompt = kt.reshape(bp, N_HEADS, hd, s).transpose(0, 3, 1, 2)
    v_prompt = vt.reshape(bp, N_HEADS, hd, s).transpose(0, 3, 1, 2)
    lru_h_prompt = h_last[:, 0, :]
    lru_conv_prompt = tail[:, SUBLANES - (CONV_W - 1):, :]

    xs = x_sample.reshape(ns, d)
    cbuf = state_lru_conv.transpose(1, 0, 2)
    qt, ktn, k_s, v_s, h_s, cnew, mix_ls, ga_s = _sample_proj(
        xs, mod_s, g_norm2, w_in_bf, w_qkt, cbuf, state_lru_h, conv_w, conv_b2, wg, bg, a_param2)
    pt_flat = page_table.reshape(-1)
    n_phys = cache_k.shape[0]
    kc_t = cache_k.transpose(0, 2, 3, 1).reshape(n_phys, w, page)
    vc_t = cache_v.transpose(0, 2, 3, 1)
    s_all, idxv, valv = _score(pt_flat, qt, kc_t, ns, n_pages)
    idx_flat = idxv[:, :, :, 0].reshape(-1)
    nb = n_pages // 2
    pos = (nb - 1) * MOBA_BLOCK + jnp.arange(MOBA_BLOCK, dtype=I32)
    bkt_last = _t5_bucket(past - pos).reshape(1, MOBA_BLOCK)
    y_att = _decode(pt_flat, idx_flat, s_all, idxv, valv, qt, ktn, v_s.reshape(ns, N_HEADS, hd),
                    rel_bias.T, bkt_last, vc_t, ns, n_pages)
    y_s = _outproj_sample(xs, mod_s, mix_ls, y_att.reshape(ns, w), ga_s, w_out_bf, g_final2)

    return (y_prompt, y_s.reshape(ns, 1, d), k_prompt, v_prompt, lru_h_prompt, lru_conv_prompt,
            k_s.reshape(ns, 1, N_HEADS, hd), v_s.reshape(ns, 1, N_HEADS, hd), h_s,
            cnew.transpose(1, 0, 2))
```

```python
import functools
import math

import jax
import jax.numpy as jnp
from jax import lax
from jax.experimental import pallas as pl
from jax.experimental.pallas import tpu as pltpu

F32 = jnp.float32
BF16 = jnp.bfloat16
I32 = jnp.int32

N_HEADS = 8
LRU_BLOCKS = 8
CONV_W = 4
LRU_C = 8.0
MOBA_BLOCK = 256
MOBA_TOPK = 3
REL_BUCKETS = 32
REL_MAX_DIST = 128
RMS_EPS = 1e-6
NEG_INF = -1e30
HIGHEST = lax.Precision.HIGHEST
LOG2E = math.log2(math.e)

LANES = 128
SUBLANES = 8
VMEM_LIMIT_BYTES = 56 * 1024 * 1024

ROW_TILE = 512
LRU_CHUNK = 512
PAGES_PER_GROUP = 64
K_SLOTS = 2
SCORE_PAGES = 16


def _sigmoid(x):
    return 0.5 * jnp.tanh(0.5 * x) + 0.5


def _silu(x):
    return x * _sigmoid(x)


def _softplus(z):
    return jnp.maximum(z, 0.0) + jnp.log1p(jnp.exp(-jnp.abs(z)))


def _t5_bucket(dist):
    n = jnp.maximum(dist, 0)
    max_exact = REL_BUCKETS // 2
    nf = jnp.maximum(n, 1).astype(F32)
    large = max_exact + (jnp.log(nf / max_exact) / math.log(REL_MAX_DIST / max_exact)
                         * (REL_BUCKETS - max_exact)).astype(I32)
    large = jnp.minimum(large, REL_BUCKETS - 1)
    return jnp.where(n < max_exact, n, large)


def _params(*sem):
    return pltpu.CompilerParams(dimension_semantics=sem, vmem_limit_bytes=VMEM_LIMIT_BYTES)


def _adaln_kernel(c_ref, w_ref, b_ref, o_ref):
    s = _silu(c_ref[...]).astype(BF16)
    o_ref[...] = jnp.dot(s, w_ref[...].astype(BF16), preferred_element_type=F32) + b_ref[...]


def _adaln(c, w_ada, b_ada):
    n, d = c.shape
    return pl.pallas_call(
        _adaln_kernel,
        out_shape=jax.ShapeDtypeStruct((n, 3 * d), F32),
        grid=(3,),
        in_specs=[pl.BlockSpec((n, d), lambda j: (0, 0)),
                  pl.BlockSpec((d, d), lambda j: (0, j)),
                  pl.BlockSpec((1, d), lambda j: (0, j))],
        out_specs=pl.BlockSpec((n, d), lambda j: (0, j)),
        compiler_params=_params("arbitrary"),
        name="adaln",
    )(c, w_ada, b_ada.reshape(1, 3 * d))


def _modulated_norm(x, mod, g, d):
    ms = jnp.mean(x * x, axis=-1, keepdims=True)
    xn = x * lax.rsqrt(ms + RMS_EPS) * g
    return xn * (1.0 + mod[:, d:2 * d]) + mod[:, 0:d]


def _inproj_kernel(x_ref, mod_ref, g_ref, wrow_ref, wkvt_ref,
                   u_ref, gl_ref, q_ref, ga_ref, kt_ref, vt_ref, *, d, w):
    xm = _modulated_norm(x_ref[...], mod_ref[...], g_ref[...], d).astype(BF16)
    pr = jnp.dot(xm, wrow_ref[...], preferred_element_type=F32)
    u_ref[...] = pr[:, 0:w]
    gl_ref[...] = pr[:, w:2 * w]
    q_ref[...] = pr[:, 2 * w:3 * w]
    ga_ref[...] = pr[:, 3 * w:4 * w]
    pt = lax.dot_general(wkvt_ref[...], xm, (((1,), (1,)), ((), ())),
                         preferred_element_type=F32)
    kt_ref[...] = pt[0:w]
    vt_ref[...] = pt[w:2 * w]


def _inproj(x, mod3, g_norm, w_row, w_kvt):
    b, s, d = x.shape
    w = w_row.shape[1] // 4
    tm = ROW_TILE
    row = pl.BlockSpec((None, tm, w), lambda i, t: (i, t, 0))
    col = pl.BlockSpec((None, w, tm), lambda i, t: (i, 0, t))
    return pl.pallas_call(
        functools.partial(_inproj_kernel, d=d, w=w),
        out_shape=[jax.ShapeDtypeStruct((b, s, w), F32)] * 4
        + [jax.ShapeDtypeStruct((b, w, s), F32)] * 2,
        grid=(b, s // tm),
        in_specs=[pl.BlockSpec((None, tm, d), lambda i, t: (i, t, 0)),
                  pl.BlockSpec((None, 1, 3 * d), lambda i, t: (i, 0, 0)),
                  pl.BlockSpec((1, d), lambda i, t: (0, 0)),
                  pl.BlockSpec((d, 4 * w), lambda i, t: (0, 0)),
                  pl.BlockSpec((2 * w, d), lambda i, t: (0, 0))],
        out_specs=[row, row, row, row, col, col],
        compiler_params=_params("arbitrary", "arbitrary"),
        name="inproj",
    )(x, mod3, g_norm, w_row, w_kvt)


def _lru_gates(u_conv, wg_ref, bg_ref, ap_ref, w):
    g2 = jnp.dot(u_conv.astype(BF16), wg_ref[...], preferred_element_type=F32) + bg_ref[...]
    r = _sigmoid(g2[:, 0:w])
    i = _sigmoid(g2[:, w:2 * w])
    log_a = (-LRU_C * r) * _softplus(-ap_ref[...])
    a = jnp.exp(log_a)
    bx = jnp.sqrt(-jnp.tanh(log_a) * (a * a + 1.0)) * (i * u_conv)
    return a, bx


def _lru_kernel(u_ref, gl_ref, cw_ref, cb_ref, wg_ref, bg_ref, ap_ref,
                mix_ref, hl_ref, tail_ref, a_scr, b_scr, hs_scr, h_scr, tail_scr, *, tc, w):
    c = pl.program_id(1)

    @pl.when(c == 0)
    def _():
        h_scr[...] = jnp.zeros_like(h_scr)
        tail_scr[...] = jnp.zeros_like(tail_scr)

    u = u_ref[...]
    ext = jnp.concatenate([tail_scr[...], u], axis=0)
    cw = cw_ref[...]
    u_conv = cb_ref[...] + (ext[5:5 + tc] * cw[0:1] + ext[6:6 + tc] * cw[1:2]
                            + ext[7:7 + tc] * cw[2:3] + u * cw[3:4])
    tail_scr[...] = u[tc - SUBLANES:tc]
    a, bx = _lru_gates(u_conv, wg_ref, bg_ref, ap_ref, w)
    a_scr[...] = a
    b_scr[...] = bx

    row = lax.broadcasted_iota(I32, (SUBLANES, w), 0)

    def tile(t, h):
        r0 = pl.multiple_of(t * SUBLANES, SUBLANES)
        at = a_scr[pl.ds(r0, SUBLANES), :]
        bt = b_scr[pl.ds(r0, SUBLANES), :]
        for dd in (1, 2, 4):
            keep = row >= dd
            bt = jnp.where(keep, at * pltpu.roll(bt, dd, 0) + bt, bt)
            at = jnp.where(keep, at * pltpu.roll(at, dd, 0), at)
        hs = at * h + bt
        hs_scr[pl.ds(r0, SUBLANES), :] = hs
        return jnp.broadcast_to(hs[SUBLANES - 1:SUBLANES, :], (SUBLANES, w))

    h = lax.fori_loop(0, tc // SUBLANES, tile, h_scr[...], unroll=4)
    h_scr[...] = h
    hl_ref[...] = h
    tail_ref[...] = u[tc - SUBLANES:tc]
    mix_ref[...] = (hs_scr[...] * _silu(gl_ref[...])).astype(mix_ref.dtype)


def _lru_prompt(u, gl, conv_w, conv_b, wg, bg, a_param):
    b, s, w = u.shape
    tc = LRU_CHUNK
    const = lambda shape: pl.BlockSpec(shape, lambda i, t: (0,) * len(shape))
    return pl.pallas_call(
        functools.partial(_lru_kernel, tc=tc, w=w),
        out_shape=[jax.ShapeDtypeStruct((b, s, w), BF16),
                   jax.ShapeDtypeStruct((b, SUBLANES, w), F32),
                   jax.ShapeDtypeStruct((b, SUBLANES, w), F32)],
        grid=(b, s // tc),
        in_specs=[pl.BlockSpec((None, tc, w), lambda i, t: (i, t, 0)),
                  pl.BlockSpec((None, tc, w), lambda i, t: (i, t, 0)),
                  const((CONV_W, w)), const((1, w)), const((w, 2 * w)), const((1, 2 * w)),
                  const((1, w))],
        out_specs=[pl.BlockSpec((None, tc, w), lambda i, t: (i, t, 0)),
                   pl.BlockSpec((None, SUBLANES, w), lambda i, t: (i, 0, 0)),
                   pl.BlockSpec((None, SUBLANES, w), lambda i, t: (i, 0, 0))],
        scratch_shapes=[pltpu.VMEM((tc, w), F32), pltpu.VMEM((tc, w), F32),
                        pltpu.VMEM((tc, w), F32), pltpu.VMEM((SUBLANES, w), F32),
                        pltpu.VMEM((SUBLANES, w), F32)],
        compiler_params=_params("arbitrary", "arbitrary"),
        name="lru_prompt",
    )(u, gl, conv_w, conv_b, wg, bg, a_param)


_NT = (((1,), (1,)), ((), ()))


def _attn_score_kernel(pt_ref, rb_ref, q_ref, kt_ref, vt_ref, ga_ref, bkt_ref, qt_ref, kc_ref,
                       o_ref, s_ref, idx_ref, val_ref,
                       bias_scr, kaug_scr, vb_scr, mpad_scr, qm_scr, lg_scr, kbuf, sem, qb_scr, g_scr,
                       *, hd, nb, n_pages, n_seq):
    p = pl.program_id(0)
    b = pl.program_id(1)
    blk = MOBA_BLOCK
    s = nb * blk
    pg = PAGES_PER_GROUP
    gps = n_pages // pg
    total = n_seq * gps
    seq = p * pl.num_programs(1) + b

    def page_copy(g, jj, slot):
        return pltpu.make_async_copy(kc_ref.at[pt_ref[g * pg + jj]], kbuf.at[slot, jj], sem.at[slot])

    def start_group(g):
        slot = lax.rem(g, K_SLOTS)
        for jj in range(pg):
            page_copy(g, jj, slot).start()

    def wait_group(g):
        slot = lax.rem(g, K_SLOTS)
        for jj in range(pg):
            page_copy(g, jj, slot).wait()

    @pl.when(seq == 0)
    def _():
        for g in range(K_SLOTS):
            start_group(g)

    lane_t = lax.broadcasted_iota(I32, qt_ref.shape, 1)
    qcol = jnp.sum(jnp.where(lane_t == seq, qt_ref[...], 0.0), axis=1, keepdims=True)
    qb_scr[...] = jnp.broadcast_to(qcol, qb_scr.shape)

    @pl.when(b == 0)
    def _():
        qi = lax.broadcasted_iota(I32, (blk, blk), 0)
        ki = lax.broadcasted_iota(I32, (blk, blk), 1)
        for hh in range(2):
            h = 2 * p + hh
            far = rb_ref[REL_BUCKETS - 1, h]
            for which in range(2):
                bk = bkt_ref[which]
                tab = jnp.zeros((blk, blk), F32)
                for k in range(REL_BUCKETS):
                    tab = jnp.where(bk == k, (rb_ref[k, h] - far) * LOG2E, tab)
                if which == 0:
                    tab = jnp.where(qi >= ki, tab, NEG_INF)
                bias_scr[hh, which] = tab

    n_i = lax.broadcasted_iota(I32, (SUBLANES, s), 0)
    j_i = lax.broadcasted_iota(I32, (SUBLANES, s), 1) // blk
    ind = jnp.where(n_i == j_i, 1.0, 0.0)
    kt = kt_ref[...]
    km8 = lax.dot_general(ind.astype(BF16), kt.astype(BF16), _NT,
                          preferred_element_type=F32) * (1.0 / blk)
    row_v = lax.broadcasted_iota(I32, (LANES, s), 0)
    vt = vt_ref[...]
    vb_scr[0] = jnp.where(row_v < hd, vt, 1.0).astype(BF16)
    vb_scr[1] = jnp.where(row_v >= hd, vt, 1.0).astype(BF16)
    qs = q_ref[...] * (hd ** -0.5 * LOG2E)
    lane_q = lax.broadcasted_iota(I32, (s, LANES), 1)
    zeros56 = jnp.zeros((hd - SUBLANES, s), F32)
    eye = jnp.where(lax.broadcasted_iota(I32, (blk, blk), 0) == lax.broadcasted_iota(I32, (blk, blk), 1),
                    1.0, 0.0).astype(BF16)

    for hh in range(2):
        head_lanes = (lane_q >= hd * hh) & (lane_q < hd * (hh + 1))
        qm = jnp.where(head_lanes, qs, 0.0).astype(BF16)
        qm_scr[hh] = qm
        gt = lax.dot_general(km8.astype(BF16), qm, _NT, preferred_element_type=F32)
        past = n_i < j_i
        g = jnp.where(past, gt, NEG_INF)
        cnt = jnp.zeros((SUBLANES, s), I32)
        for m in range(nb - 1):
            gm = g[m:m + 1, :]
            cnt = cnt + jnp.where((gm > g) | ((gm == g) & (n_i > m)), 1, 0)
        keep = (past & (cnt < MOBA_TOPK) & (g > 0.5 * NEG_INF * (hd ** -0.5 * LOG2E))) | (n_i >= j_i)
        mask_t = jnp.where(keep, 0.0, NEG_INF)
        if hh == 0:
            kaug = jnp.concatenate([kt[0:hd], ind, zeros56], axis=0)
            mpad = jnp.concatenate([jnp.zeros((hd, s), F32), mask_t, zeros56], axis=0)
        else:
            kaug = jnp.concatenate([ind, zeros56, kt[hd:2 * hd]], axis=0)
            mpad = jnp.concatenate([mask_t, zeros56, jnp.zeros((hd, s), F32)], axis=0)
        kaug_scr[hh] = kaug.astype(BF16)
        mpad_scr[hh] = mpad.astype(BF16)

    lane = lax.broadcasted_iota(I32, (blk, LANES), 1)

    def attend(j, hh, par):
        rows = slice(j * blk, (j + 1) * blk)
        mf = lax.dot_general(eye, mpad_scr[hh, :, rows], _NT, preferred_element_type=F32)
        qa = qm_scr[hh, rows, :] + mf.astype(BF16)
        macc = None
        for n in range(j + 1):
            sc = jnp.dot(qa, kaug_scr[hh, :, n * blk:(n + 1) * blk],
                         preferred_element_type=F32)
            if n == j:
                sc = sc + bias_scr[hh, 0]
            elif n == j - 1:
                sc = sc + bias_scr[hh, 1]
            lg_scr[par, n] = sc
            mx = jnp.maximum(sc[:, 0:LANES], sc[:, LANES:2 * LANES])
            macc = mx if macc is None else jnp.maximum(macc, mx)
        mrow = jnp.max(macc, axis=1, keepdims=True)
        acc = None
        for n in range(j + 1):
            pn = jnp.exp2(lg_scr[par, n] - mrow).astype(BF16)
            pv = lax.dot_general(pn, vb_scr[hh, :, n * blk:(n + 1) * blk], _NT,
                                 preferred_element_type=F32)
            acc = pv if acc is None else acc + pv
        return acc

    def finish(j, accs):
        rows = slice(j * blk, (j + 1) * blk)
        first = lane < hd
        num = jnp.where(first, accs[0], accs[1])
        den = pltpu.roll(jnp.where(first, accs[1], accs[0]), hd, 1)
        o_ref[rows, :] = (num / den * _silu(ga_ref[rows, :])).astype(o_ref.dtype)

    sub8 = lax.broadcasted_iota(I32, (SUBLANES, LANES), 0)
    tiles = hd // SUBLANES

    def score_group(sg, slot):
        def trip(t, carry):
            res = [jnp.zeros((N_HEADS, LANES), F32) for _ in range(SCORE_PAGES)]
            for h in range(N_HEADS):
                qh = [qb_scr[(h * tiles + i) * SUBLANES:(h * tiles + i + 1) * SUBLANES, :]
                      for i in range(tiles)]
                for pp in range(SCORE_PAGES):
                    page = kbuf.at[slot, t * SCORE_PAGES + pp]
                    acc = None
                    for i in range(tiles):
                        r0 = (h * tiles + i) * SUBLANES
                        term = page[r0:r0 + SUBLANES, :] * qh[i]
                        acc = term if acc is None else acc + term
                    for sh in (4, 2, 1):
                        acc = acc + pltpu.roll(acc, sh, 0)
                    res[pp] = jnp.where(sub8 == h, acc, res[pp])
            for bb in range(SCORE_PAGES // 2):
                n = sg * (pg // 2) + t * (SCORE_PAGES // 2) + bb
                s_ref[n, :, 0:LANES] = res[2 * bb]
                s_ref[n, :, LANES:2 * LANES] = res[2 * bb + 1]
                bsum = jnp.sum(res[2 * bb] + res[2 * bb + 1], axis=1, keepdims=True)
                g_scr[n] = jnp.broadcast_to(bsum, (N_HEADS, LANES))
            return carry

        lax.fori_loop(0, pg // SCORE_PAGES, trip, 0)

    items = [(j, hh) for j in reversed(range(nb)) for hh in range(2)]
    work = sum(j + 1 for j, _ in items)
    segments = [[] for _ in range(gps)]
    done = 0
    for item in items:
        segments[min(gps - 1, done * gps // work)].append(item)
        done += item[0] + 1

    accs = {}
    count = 0
    for sg in range(gps):
        g = seq * gps + sg
        wait_group(g)
        score_group(sg, lax.rem(g, K_SLOTS))

        @pl.when(g + K_SLOTS < total)
        def _():
            start_group(g + K_SLOTS)

        for j, hh in segments[sg]:
            accs.setdefault(j, []).append(attend(j, hh, count % 2))
            count += 1
            if len(accs[j]) == 2:
                finish(j, accs.pop(j))

    gate = g_scr[...] * (1.0 / MOBA_BLOCK)
    nidx = lax.broadcasted_iota(I32, gate.shape, 0)
    for r in range(MOBA_TOPK):
        mx = jnp.max(gate, axis=0)
        am = jnp.min(jnp.where(gate == mx[None], nidx, gate.shape[0]), axis=0)
        idx_ref[r] = am
        val_ref[r] = mx
        gate = jnp.where(nidx == am[None], -jnp.inf, gate)


def _attn_and_score(q, kt, vt, ga, rel_bias, pt_flat, q_dec, kcache_t, n_seq, n_pages):
    b, s, w = q.shape
    hd = w // N_HEADS
    blk = MOBA_BLOCK
    nb = s // blk
    npair = w // LANES
    nblk = n_pages // 2
    assert nb <= SUBLANES and LANES == 2 * hd
    assert n_seq == npair * b and n_pages % PAGES_PER_GROUP == 0
    dq = jnp.arange(blk, dtype=I32)
    d_own = dq[:, None] - dq[None, :]
    bkt = jnp.stack([_t5_bucket(d_own), _t5_bucket(d_own + blk)])
    per_seq = lambda p, i, pt: (p * b + i, 0, 0, 0)
    grid_spec = pltpu.PrefetchScalarGridSpec(
        num_scalar_prefetch=1,
        grid=(npair, b),
        in_specs=[pl.BlockSpec(memory_space=pltpu.SMEM),
                  pl.BlockSpec((None, s, LANES), lambda p, i, pt: (i, 0, p)),
                  pl.BlockSpec((None, LANES, s), lambda p, i, pt: (i, p, 0)),
                  pl.BlockSpec((None, LANES, s), lambda p, i, pt: (i, p, 0)),
                  pl.BlockSpec((None, s, LANES), lambda p, i, pt: (i, 0, p)),
                  pl.BlockSpec((2, blk, blk), lambda p, i, pt: (0, 0, 0)),
                  pl.BlockSpec((w, LANES), lambda p, i, pt: (0, 0)),
                  pl.BlockSpec(memory_space=pl.ANY)],
        out_specs=[pl.BlockSpec((None, s, LANES), lambda p, i, pt: (i, 0, p)),
                   pl.BlockSpec((None, nblk, N_HEADS, MOBA_BLOCK), per_seq),
                   pl.BlockSpec((None, MOBA_TOPK, N_HEADS, LANES), per_seq),
                   pl.BlockSpec((None, MOBA_TOPK, N_HEADS, LANES), per_seq)],
        scratch_shapes=[pltpu.VMEM((2, 2, blk, blk), F32),
                        pltpu.VMEM((2, LANES, s), BF16),
                        pltpu.VMEM((2, LANES, s), BF16),
                        pltpu.VMEM((2, LANES, s), BF16),
                        pltpu.VMEM((2, s, LANES), BF16),
                        pltpu.VMEM((2, nb, blk, blk), F32),
                        pltpu.VMEM((K_SLOTS, PAGES_PER_GROUP, w, LANES), F32),
                        pltpu.SemaphoreType.DMA((K_SLOTS,)),
                        pltpu.VMEM((w, LANES), F32),
                        pltpu.VMEM((nblk, N_HEADS, LANES), F32)])
    return pl.pallas_call(
        functools.partial(_attn_score_kernel, hd=hd, nb=nb, n_pages=n_pages, n_seq=n_seq),
        out_shape=[jax.ShapeDtypeStruct((b, s, w), BF16),
                   jax.ShapeDtypeStruct((n_seq, nblk, N_HEADS, MOBA_BLOCK), F32),
                   jax.ShapeDtypeStruct((n_seq, MOBA_TOPK, N_HEADS, LANES), I32),
                   jax.ShapeDtypeStruct((n_seq, MOBA_TOPK, N_HEADS, LANES), F32)],
        grid_spec=grid_spec,
        compiler_params=_params("arbitrary", "arbitrary"),
        name="attn_score",
    )(pt_flat, rel_bias, q, kt, vt, ga, bkt, q_dec, kcache_t)


def _outproj_kernel(x_ref, mod_ref, ml_ref, ma_ref, wo_ref, gf_ref, y_ref, *, d, w):
    acc = jnp.dot(ml_ref[...].astype(BF16), wo_ref[0:w, :], preferred_element_type=F32)
    acc = acc + jnp.dot(ma_ref[...].astype(BF16), wo_ref[w:2 * w, :], preferred_element_type=F32)
    out = x_ref[...] + mod_ref[:, 2 * d:3 * d] * acc
    ms = jnp.mean(out * out, axis=-1, keepdims=True)
    y_ref[...] = out * lax.rsqrt(ms + RMS_EPS) * gf_ref[...]


def _outproj_prompt(x, mod3, mix_l, mix_a, w_out, g_final):
    b, s, d = x.shape
    w = mix_l.shape[-1]
    tm = ROW_TILE
    return pl.pallas_call(
        functools.partial(_outproj_kernel, d=d, w=w),
        out_shape=jax.ShapeDtypeStruct((b, s, d), F32),
        grid=(b, s // tm),
        in_specs=[pl.BlockSpec((None, tm, d), lambda i, t: (i, t, 0)),
                  pl.BlockSpec((None, 1, 3 * d), lambda i, t: (i, 0, 0)),
                  pl.BlockSpec((None, tm, w), lambda i, t: (i, t, 0)),
                  pl.BlockSpec((None, tm, w), lambda i, t: (i, t, 0)),
                  pl.BlockSpec((2 * w, d), lambda i, t: (0, 0)),
                  pl.BlockSpec((1, d), lambda i, t: (0, 0))],
        out_specs=pl.BlockSpec((None, tm, d), lambda i, t: (i, t, 0)),
        compiler_params=_params("arbitrary", "arbitrary"),
        name="outproj_prompt",
    )(x, mod3, mix_l, mix_a, w_out, g_final)


def _sample_proj_kernel(x_ref, mod_ref, g_ref, win_ref, wqkt_ref, cbuf_ref, h0_ref, cw_ref, cb_ref,
                        wg_ref, bg_ref, ap_ref,
                        qt_ref, kt_ref, k_ref, v_ref, h_ref, cnew_ref, ml_ref, ga_ref, *, d, w):
    n = x_ref.shape[0]
    xm = _modulated_norm(x_ref[...], mod_ref[...], g_ref[...], d).astype(BF16)
    pr = jnp.dot(xm, win_ref[...], preferred_element_type=F32)
    u = pr[:, 0:w]
    k_ref[...] = pr[:, 3 * w:4 * w]
    v_ref[...] = pr[:, 4 * w:5 * w]
    ga_ref[...] = pr[:, 5 * w:6 * w]
    xpad = jnp.concatenate([xm, jnp.zeros((LANES - n, d), BF16)], axis=0)
    pt = lax.dot_general(wqkt_ref[...], xpad, (((1,), (1,)), ((), ())),
                         preferred_element_type=F32)
    qt_ref[...] = pt[0:w]
    kt_ref[...] = pt[w:2 * w]
    cw = cw_ref[...]
    u_conv = cb_ref[...] + (cbuf_ref[0] * cw[0:1] + cbuf_ref[1] * cw[1:2]
                            + cbuf_ref[2] * cw[2:3] + u * cw[3:4])
    a, bx = _lru_gates(u_conv, wg_ref, bg_ref, ap_ref, w)
    h = a * h0_ref[...] + bx
    h_ref[...] = h
    cnew_ref[0] = cbuf_ref[1]
    cnew_ref[1] = cbuf_ref[2]
    cnew_ref[2] = u
    ml_ref[...] = h * _silu(pr[:, w:2 * w])


def _sample_proj(x, mod, g_norm, w_in_bf, w_qkt, cbuf, h0, conv_w, conv_b, wg, bg, a_param):
    n, d = x.shape
    w = h0.shape[1]
    row = jax.ShapeDtypeStruct((n, w), F32)
    col = jax.ShapeDtypeStruct((w, LANES), F32)
    return pl.pallas_call(
        functools.partial(_sample_proj_kernel, d=d, w=w),
        out_shape=[col, col, row, row, row, jax.ShapeDtypeStruct((CONV_W - 1, n, w), F32), row, row],
        compiler_params=pltpu.CompilerParams(vmem_limit_bytes=VMEM_LIMIT_BYTES),
        name="sample_proj",
    )(x, mod, g_norm, w_in_bf, w_qkt, cbuf, h0, conv_w, conv_b, wg, bg, a_param)


def _decode_kernel(pt_ref, ix_ref, s_ref, idxv_ref, valv_ref, qt_ref, kt_ref, vnew_ref, rbt_ref,
                   bkt_ref, vc_ref, y_ref, vbuf, sem, *, n_pages, n_seq, hd):
    b = pl.program_id(0)
    nsel = MOBA_TOPK
    blk = MOBA_BLOCK
    nb = n_pages // 2

    def tile_copy(bb, r, h, pp, slot):
        page = pt_ref[bb * n_pages + 2 * ix_ref[(bb * nsel + r) * N_HEADS + h] + pp]
        return pltpu.make_async_copy(
            vc_ref.at[page, h], vbuf.at[slot, h, :, pl.ds((2 * r + pp) * LANES, LANES)], sem.at[slot])

    def for_tiles(bb, fn):
        slot = lax.rem(bb, 2)
        for r in range(nsel):
            for h in range(N_HEADS):
                for pp in range(2):
                    fn(tile_copy(bb, r, h, pp, slot))

    @pl.when(b == 0)
    def _():
        for_tiles(0, lambda cp: cp.start())

    @pl.when(b + 1 < n_seq)
    def _():
        for_tiles(b + 1, lambda cp: cp.start())

    sub = lax.broadcasted_iota(I32, (N_HEADS, blk), 0)
    rbt = rbt_ref[...]
    bias_last = jnp.zeros((N_HEADS, blk), F32)
    for k in range(REL_BUCKETS):
        bias_last = jnp.where(bkt_ref[...] == k, rbt[:, k:k + 1], bias_last)
    bias_far = rbt[:, REL_BUCKETS - 1:REL_BUCKETS]

    logits = []
    for r in range(nsel):
        lr = jnp.zeros((N_HEADS, blk), F32)
        for h in range(N_HEADS):
            n = ix_ref[(b * nsel + r) * N_HEADS + h]
            lr = jnp.where(sub == h, s_ref[n], lr)
        idv = idxv_ref[r][:, 0:1]
        bias = jnp.where(idv == nb - 1, bias_last, bias_far)
        valid = valv_ref[r][:, 0:1] > 0.5 * NEG_INF
        logits.append(jnp.where(valid, lr * (hd ** -0.5) + bias, NEG_INF))

    lane = lax.broadcasted_iota(I32, (N_HEADS, LANES), 1)
    prod = (qt_ref[...] * kt_ref[...]).reshape(N_HEADS, hd, LANES)
    own = jnp.sum(jnp.where(lane == b, jnp.sum(prod, axis=1), 0.0), axis=1, keepdims=True)
    l_own = own * (hd ** -0.5) + rbt[:, 0:1]

    m = l_own
    for lr in logits:
        m = jnp.maximum(m, jnp.max(lr, axis=1, keepdims=True))
    p_own = jnp.exp(l_own - m)
    den = p_own
    probs = []
    for lr in logits:
        pr = jnp.exp(lr - m)
        den = den + jnp.sum(pr, axis=1, keepdims=True)
        probs.append(pr)
    pcat = jnp.concatenate(probs, axis=1)

    for_tiles(b, lambda cp: cp.wait())
    slot = lax.rem(b, 2)
    rows = []
    for h in range(N_HEADS):
        vt = vbuf[slot, h].astype(BF16)
        rows.append(lax.dot_general(pcat[h:h + 1, :].astype(BF16), vt, (((1,), (1,)), ((), ())),
                                    preferred_element_type=F32))
    acc = jnp.concatenate(rows, axis=0)
    y_ref[...] = (acc + p_own * vnew_ref[...]) / den


def _decode(pt_flat, idx_flat, s_all, idxv, valv, qt, kt, v_new, rbt, bkt_last, vcache_t, n_seq, n_pages):
    w = qt.shape[0]
    hd = w // N_HEADS
    nb = n_pages // 2
    zero4 = lambda b, pt, ix: (b, 0, 0, 0)
    grid_spec = pltpu.PrefetchScalarGridSpec(
        num_scalar_prefetch=2,
        grid=(n_seq,),
        in_specs=[pl.BlockSpec((None, nb, N_HEADS, MOBA_BLOCK), zero4),
                  pl.BlockSpec((None, MOBA_TOPK, N_HEADS, LANES), zero4),
                  pl.BlockSpec((None, MOBA_TOPK, N_HEADS, LANES), zero4),
                  pl.BlockSpec((w, LANES), lambda b, pt, ix: (0, 0)),
                  pl.BlockSpec((w, LANES), lambda b, pt, ix: (0, 0)),
                  pl.BlockSpec((None, N_HEADS, hd), lambda b, pt, ix: (b, 0, 0)),
                  pl.BlockSpec((N_HEADS, REL_BUCKETS), lambda b, pt, ix: (0, 0)),
                  pl.BlockSpec((1, MOBA_BLOCK), lambda b, pt, ix: (0, 0)),
                  pl.BlockSpec(memory_space=pl.ANY)],
        out_specs=pl.BlockSpec((None, N_HEADS, hd), lambda b, pt, ix: (b, 0, 0)),
        scratch_shapes=[pltpu.VMEM((2, N_HEADS, hd, MOBA_TOPK * MOBA_BLOCK), F32),
                        pltpu.SemaphoreType.DMA((2,))])
    return pl.pallas_call(
        functools.partial(_decode_kernel, n_pages=n_pages, n_seq=n_seq, hd=hd),
        out_shape=jax.ShapeDtypeStruct((n_seq, N_HEADS, hd), F32),
        grid_spec=grid_spec,
        compiler_params=_params("arbitrary"),
        name="decode_attn",
    )(pt_flat, idx_flat, s_all, idxv, valv, qt, kt, v_new, rbt, bkt_last, vcache_t)


def _outproj_sample_kernel(x_ref, mod_ref, ml_ref, ya_ref, ga_ref, wo_ref, gf_ref, y_ref, *, d, w):
    ma = ya_ref[...] * _silu(ga_ref[...])
    acc = jnp.dot(ml_ref[...].astype(BF16), wo_ref[0:w, :], preferred_element_type=F32)
    acc = acc + jnp.dot(ma.astype(BF16), wo_ref[w:2 * w, :], preferred_element_type=F32)
    out = x_ref[...] + mod_ref[:, 2 * d:3 * d] * acc
    ms = jnp.mean(out * out, axis=-1, keepdims=True)
    y_ref[...] = out * lax.rsqrt(ms + RMS_EPS) * gf_ref[...]


def _outproj_sample(x, mod, mix_l, y_att, ga, w_out, g_final):
    n, d = x.shape
    w = mix_l.shape[1]
    return pl.pallas_call(
        functools.partial(_outproj_sample_kernel, d=d, w=w),
        out_shape=jax.ShapeDtypeStruct((n, d), F32),
        compiler_params=pltpu.CompilerParams(vmem_limit_bytes=VMEM_LIMIT_BYTES),
        name="outproj_sample",
    )(x, mod, mix_l, y_att, ga, w_out, g_final)


def _block_diag(wb):
    n, c, dd = wb.shape
    return jnp.einsum("ncd,nm->ncmd", wb, jnp.eye(n, dtype=wb.dtype)).reshape(n * c, n * dd)


def kernel(x_prompt, x_sample, cache_k, cache_v, state_lru_h, state_lru_conv, page_table, c_prompt, c_sample, w_ada, b_ada, g_norm, w_in, conv_w, conv_b, w_rgate, b_rgate, w_igate, b_igate, lru_a_param, rel_bias, w_out, g_final):
    bp, s, d = x_prompt.shape
    ns = x_sample.shape[0]
    w = state_lru_h.shape[1]
    hd = w // N_HEADS
    n_pages = page_table.shape[1]
    page = cache_k.shape[1]
    past = n_pages * page
    assert x_sample.shape[1] == 1 and 2 * page == MOBA_BLOCK and past % MOBA_BLOCK == 0
    assert s % MOBA_BLOCK == 0 and s % ROW_TILE == 0 and s % LRU_CHUNK == 0
    assert w % LANES == 0 and ns <= LANES
    assert MOBA_BLOCK + 1 >= REL_MAX_DIST

    w_in_bf = w_in.astype(BF16)
    w_row = jnp.concatenate([w_in_bf[:, 0:3 * w], w_in_bf[:, 5 * w:6 * w]], axis=1)
    w_kvt = w_in_bf[:, 3 * w:5 * w].T
    w_qkt = w_in_bf[:, 2 * w:4 * w].T
    w_out_bf = w_out.astype(BF16)
    wg = jnp.concatenate([_block_diag(w_rgate), _block_diag(w_igate)], axis=1).astype(BF16)
    bg = jnp.concatenate([b_rgate, b_igate]).reshape(1, 2 * w)
    g_norm2 = g_norm.reshape(1, d)
    g_final2 = g_final.reshape(1, d)
    conv_b2 = conv_b.reshape(1, w)
    a_param2 = lru_a_param.reshape(1, w)

    mod = _adaln(jnp.concatenate([c_prompt, c_sample], axis=0), w_ada, b_ada)
    mod_p = mod[0:bp].reshape(bp, 1, 3 * d)
    mod_s = mod[bp:bp + ns]

    xs = x_sample.reshape(ns, d)
    cbuf = state_lru_conv.transpose(1, 0, 2)
    qt, ktn, k_s, v_s, h_s, cnew, mix_ls, ga_s = _sample_proj(
        xs, mod_s, g_norm2, w_in_bf, w_qkt, cbuf, state_lru_h, conv_w, conv_b2, wg, bg, a_param2)
    pt_flat = page_table.reshape(-1)
    n_phys = cache_k.shape[0]
    kc_t = cache_k.transpose(0, 2, 3, 1).reshape(n_phys, w, page)
    vc_t = cache_v.transpose(0, 2, 3, 1)

    u, gl, q, ga, kt, vt = _inproj(x_prompt, mod_p, g_norm2, w_row, w_kvt)
    mix_l, h_last, tail = _lru_prompt(u, gl, conv_w, conv_b2, wg, bg, a_param2)
    mix_a, s_all, idxv, valv = _attn_and_score(q, kt, vt, ga, rel_bias, pt_flat, qt, kc_t, ns, n_pages)
    y_prompt = _outproj_prompt(x_prompt, mod_p, mix_l, mix_a, w_out_bf, g_final2)
    k_prompt = kt.reshape(bp, N_HEADS, hd, s).transpose(0, 3, 1, 2)
    v_prompt = vt.reshape(bp, N_HEADS, hd, s).transpose(0, 3, 1, 2)
    lru_h_prompt = h_last[:, 0, :]
    lru_conv_prompt = tail[:, SUBLANES - (CONV_W - 1):, :]

    idx_flat = idxv[:, :, :, 0].reshape(-1)
    nb = n_pages // 2
    pos = (nb - 1) * MOBA_BLOCK + jnp.arange(MOBA_BLOCK, dtype=I32)
    bkt_last = _t5_bucket(past - pos).reshape(1, MOBA_BLOCK)
    y_att = _decode(pt_flat, idx_flat, s_all, idxv, valv, qt, ktn, v_s.reshape(ns, N_HEADS, hd),
                    rel_bias.T, bkt_last, vc_t, ns, n_pages)
    y_s = _outproj_sample(xs, mod_s, mix_ls, y_att.reshape(ns, w), ga_s, w_out_bf, g_final2)

    return (y_prompt, y_s.reshape(ns, 1, d), k_prompt, v_prompt, lru_h_prompt, lru_conv_prompt,
            k_s.reshape(ns, 1, N_HEADS, hd), v_s.reshape(ns, 1, N_HEADS, hd), h_s,
            cnew.transpose(1, 0, 2))
```

```python
import functools
import math

import jax
import jax.numpy as jnp
from jax import lax
from jax.experimental import pallas as pl
from jax.experimental.pallas import tpu as pltpu

F32 = jnp.float32
BF16 = jnp.bfloat16
I32 = jnp.int32

N_HEADS = 8
LRU_BLOCKS = 8
CONV_W = 4
LRU_C = 8.0
MOBA_BLOCK = 256
MOBA_TOPK = 3
REL_BUCKETS = 32
REL_MAX_DIST = 128
RMS_EPS = 1e-6
NEG_INF = -1e30
HIGHEST = lax.Precision.HIGHEST
LOG2E = math.log2(math.e)

LANES = 128
SUBLANES = 8
VMEM_LIMIT_BYTES = 56 * 1024 * 1024

ROW_TILE = 512
OUT_TILE = 1024
LRU_CHUNK = 512
PAGES_PER_GROUP = 64
K_SLOTS = 2
SCORE_PAGES = 16
MASK_ROWS = 16
DEC_SEQS = 4


def _sigmoid(x):
    return 0.5 * jnp.tanh(0.5 * x) + 0.5


def _silu(x):
    return x * _sigmoid(x)


def _softplus(z):
    return jnp.maximum(z, 0.0) + jnp.log1p(jnp.exp(-jnp.abs(z)))


def _t5_bucket(dist):
    n = jnp.maximum(dist, 0)
    max_exact = REL_BUCKETS // 2
    nf = jnp.maximum(n, 1).astype(F32)
    pos = jnp.log(nf / max_exact) / math.log(REL_MAX_DIST / max_exact) * (REL_BUCKETS - max_exact)
    large = max_exact + jnp.where(pos >= 0, jnp.floor(pos), jnp.ceil(pos)).astype(I32)
    large = jnp.minimum(large, REL_BUCKETS - 1)
    return jnp.where(n < max_exact, n, large)


def _params(*sem):
    return pltpu.CompilerParams(dimension_semantics=sem, vmem_limit_bytes=VMEM_LIMIT_BYTES)


def _adaln_kernel(c_ref, w_ref, b_ref, o_ref):
    s = _silu(c_ref[...]).astype(BF16)
    o_ref[...] = jnp.dot(s, w_ref[...].astype(BF16), preferred_element_type=F32) + b_ref[...]


def _adaln(c, w_ada, b_ada):
    n, d = c.shape
    return pl.pallas_call(
        _adaln_kernel,
        out_shape=jax.ShapeDtypeStruct((n, 3 * d), F32),
        grid=(3,),
        in_specs=[pl.BlockSpec((n, d), lambda j: (0, 0)),
                  pl.BlockSpec((d, d), lambda j: (0, j)),
                  pl.BlockSpec((1, d), lambda j: (0, j))],
        out_specs=pl.BlockSpec((n, d), lambda j: (0, j)),
        compiler_params=_params("arbitrary"),
        name="adaln",
    )(c, w_ada, b_ada.reshape(1, 3 * d))


def _wprep_kernel(w_ref, wb_ref, wt_ref):
    x = w_ref[...]
    wb_ref[...] = x.astype(BF16)
    wt_ref[...] = x.T.astype(BF16)


def _wprep(w_in, cols):
    d, n = w_in.shape
    return pl.pallas_call(
        _wprep_kernel,
        out_shape=[jax.ShapeDtypeStruct((d, n), BF16), jax.ShapeDtypeStruct((n, d), BF16)],
        grid=(n // cols,),
        in_specs=[pl.BlockSpec((d, cols), lambda j: (0, j))],
        out_specs=[pl.BlockSpec((d, cols), lambda j: (0, j)), pl.BlockSpec((cols, d), lambda j: (j, 0))],
        compiler_params=_params("arbitrary"),
        name="wprep",
    )(w_in)


def _modulated_norm(x, mod, g, d):
    ms = jnp.mean(x * x, axis=-1, keepdims=True)
    xn = x * lax.rsqrt(ms + RMS_EPS) * g
    return xn * (1.0 + mod[:, d:2 * d]) + mod[:, 0:d]


def _inproj_kernel(x_ref, mod_ref, g_ref, win_ref, wkvt_ref,
                   u_ref, gl_ref, q_ref, ga_ref, kt_ref, vt_ref, *, d, w):
    xm = _modulated_norm(x_ref[...], mod_ref[...], g_ref[...], d).astype(BF16)
    pr = jnp.dot(xm, win_ref[:, 0:3 * w], preferred_element_type=F32)
    u_ref[...] = pr[:, 0:w]
    gl_ref[...] = pr[:, w:2 * w]
    q_ref[...] = pr[:, 2 * w:3 * w]
    ga_ref[...] = jnp.dot(xm, win_ref[:, 5 * w:6 * w], preferred_element_type=F32)
    pt = lax.dot_general(wkvt_ref[...], xm, (((1,), (1,)), ((), ())),
                         preferred_element_type=F32)
    kt_ref[...] = pt[0:w]
    vt_ref[...] = pt[w:2 * w]


def _inproj(x, mod3, g_norm, w_in_bf, w_kvt):
    b, s, d = x.shape
    w = w_in_bf.shape[1] // 6
    tm = ROW_TILE
    row = pl.BlockSpec((None, tm, w), lambda i, t: (i, t, 0))
    col = pl.BlockSpec((None, w, tm), lambda i, t: (i, 0, t))
    return pl.pallas_call(
        functools.partial(_inproj_kernel, d=d, w=w),
        out_shape=[jax.ShapeDtypeStruct((b, s, w), F32)] * 4
        + [jax.ShapeDtypeStruct((b, w, s), F32)] * 2,
        grid=(b, s // tm),
        in_specs=[pl.BlockSpec((None, tm, d), lambda i, t: (i, t, 0)),
                  pl.BlockSpec((None, 1, 3 * d), lambda i, t: (i, 0, 0)),
                  pl.BlockSpec((1, d), lambda i, t: (0, 0)),
                  pl.BlockSpec((d, 6 * w), lambda i, t: (0, 0)),
                  pl.BlockSpec((2 * w, d), lambda i, t: (0, 0))],
        out_specs=[row, row, row, row, col, col],
        compiler_params=_params("arbitrary", "arbitrary"),
        name="inproj",
    )(x, mod3, g_norm, w_in_bf, w_kvt)


def _lru_gates(u_conv, wg_ref, bg_ref, ap_ref, w):
    g2 = jnp.dot(u_conv.astype(BF16), wg_ref[...], preferred_element_type=F32) + bg_ref[...]
    r = _sigmoid(g2[:, 0:w])
    i = _sigmoid(g2[:, w:2 * w])
    log_a = (-LRU_C * r) * _softplus(-ap_ref[...])
    a = jnp.exp(log_a)
    bx = jnp.sqrt(-jnp.tanh(log_a) * (a * a + 1.0)) * (i * u_conv)
    return a, bx


def _lru_kernel(u_ref, gl_ref, cw_ref, cb_ref, wg_ref, bg_ref, ap_ref,
                mix_ref, hl_ref, tail_ref, a_scr, b_scr, hs_scr, h_scr, tail_scr, *, tc, w):
    c = pl.program_id(1)

    @pl.when(c == 0)
    def _():
        h_scr[...] = jnp.zeros_like(h_scr)
        tail_scr[...] = jnp.zeros_like(tail_scr)

    u = u_ref[...]
    ext = jnp.concatenate([tail_scr[...], u], axis=0)
    cw = cw_ref[...]
    u_conv = cb_ref[...] + (ext[5:5 + tc] * cw[0:1] + ext[6:6 + tc] * cw[1:2]
                            + ext[7:7 + tc] * cw[2:3] + u * cw[3:4])
    tail_scr[...] = u[tc - SUBLANES:tc]
    a, bx = _lru_gates(u_conv, wg_ref, bg_ref, ap_ref, w)
    a_scr[...] = a
    b_scr[...] = bx

    row = lax.broadcasted_iota(I32, (SUBLANES, w), 0)

    def tile(t, h):
        r0 = pl.multiple_of(t * SUBLANES, SUBLANES)
        at = a_scr[pl.ds(r0, SUBLANES), :]
        bt = b_scr[pl.ds(r0, SUBLANES), :]
        for dd in (1, 2, 4):
            keep = row >= dd
            bt = jnp.where(keep, at * pltpu.roll(bt, dd, 0) + bt, bt)
            at = jnp.where(keep, at * pltpu.roll(at, dd, 0), at)
        hs = at * h + bt
        hs_scr[pl.ds(r0, SUBLANES), :] = hs
        return jnp.broadcast_to(hs[SUBLANES - 1:SUBLANES, :], (SUBLANES, w))

    h = lax.fori_loop(0, tc // SUBLANES, tile, h_scr[...], unroll=4)
    h_scr[...] = h
    hl_ref[...] = h
    tail_ref[...] = u[tc - SUBLANES:tc]
    mix_ref[...] = (hs_scr[...] * _silu(gl_ref[...])).astype(mix_ref.dtype)


def _lru_prompt(u, gl, conv_w, conv_b, wg, bg, a_param):
    b, s, w = u.shape
    tc = LRU_CHUNK
    const = lambda shape: pl.BlockSpec(shape, lambda i, t: (0,) * len(shape))
    return pl.pallas_call(
        functools.partial(_lru_kernel, tc=tc, w=w),
        out_shape=[jax.ShapeDtypeStruct((b, s, w), BF16),
                   jax.ShapeDtypeStruct((b, SUBLANES, w), F32),
                   jax.ShapeDtypeStruct((b, SUBLANES, w), F32)],
        grid=(b, s // tc),
        in_specs=[pl.BlockSpec((None, tc, w), lambda i, t: (i, t, 0)),
                  pl.BlockSpec((None, tc, w), lambda i, t: (i, t, 0)),
                  const((CONV_W, w)), const((1, w)), const((w, 2 * w)), const((1, 2 * w)),
                  const((1, w))],
        out_specs=[pl.BlockSpec((None, tc, w), lambda i, t: (i, t, 0)),
                   pl.BlockSpec((None, SUBLANES, w), lambda i, t: (i, 0, 0)),
                   pl.BlockSpec((None, SUBLANES, w), lambda i, t: (i, 0, 0))],
        scratch_shapes=[pltpu.VMEM((tc, w), F32), pltpu.VMEM((tc, w), F32),
                        pltpu.VMEM((tc, w), F32), pltpu.VMEM((SUBLANES, w), F32),
                        pltpu.VMEM((SUBLANES, w), F32)],
        compiler_params=_params("arbitrary", "arbitrary"),
        name="lru_prompt",
    )(u, gl, conv_w, conv_b, wg, bg, a_param)


_NT = (((1,), (1,)), ((), ()))


def _attn_score_kernel(pt_ref, rb_ref, q_ref, kt_ref, vt_ref, ga_ref, bkt_ref, qt_ref, kc_ref,
                       o_ref, s_ref, idx_ref, val_ref,
                       bias_scr, kaug_scr, vb_scr, mpad_scr, qm_scr, lg_scr, kbuf, sem, qb_scr, g_scr,
                       *, hd, nb, n_pages, n_seq):
    p = pl.program_id(0)
    b = pl.program_id(1)
    blk = MOBA_BLOCK
    s = nb * blk
    pg = PAGES_PER_GROUP
    gps = n_pages // pg
    total = n_seq * gps
    seq = p * pl.num_programs(1) + b

    def page_copy(g, jj, slot):
        return pltpu.make_async_copy(kc_ref.at[pt_ref[g * pg + jj]], kbuf.at[slot, jj], sem.at[slot])

    def start_group(g):
        slot = lax.rem(g, K_SLOTS)
        for jj in range(pg):
            page_copy(g, jj, slot).start()

    def wait_group(g):
        slot = lax.rem(g, K_SLOTS)
        for jj in range(pg):
            page_copy(g, jj, slot).wait()

    @pl.when(seq == 0)
    def _():
        for g in range(K_SLOTS):
            start_group(g)

    lane_t = lax.broadcasted_iota(I32, qt_ref.shape, 1)
    qcol = jnp.sum(jnp.where(lane_t == seq, qt_ref[...], 0.0), axis=1, keepdims=True)
    qb_scr[...] = jnp.broadcast_to(qcol, qb_scr.shape)

    @pl.when(b == 0)
    def _():
        qi = lax.broadcasted_iota(I32, (blk, blk), 0)
        ki = lax.broadcasted_iota(I32, (blk, blk), 1)
        for hh in range(2):
            h = 2 * p + hh
            far = rb_ref[REL_BUCKETS - 1, h]
            for which in range(2):
                bk = bkt_ref[which]
                tab = jnp.zeros((blk, blk), F32)
                for k in range(REL_BUCKETS):
                    tab = jnp.where(bk == k, (rb_ref[k, h] - far) * LOG2E, tab)
                if which == 0:
                    tab = jnp.where(qi >= ki, tab, NEG_INF)
                bias_scr[hh, which] = tab
        i16 = lax.broadcasted_iota(I32, (MASK_ROWS, s), 0)
        ind16 = jnp.where(i16 == lax.broadcasted_iota(I32, (MASK_ROWS, s), 1) // blk, 1.0, 0.0).astype(BF16)
        for hh in range(2):
            o0 = hd * (1 - hh)
            kaug_scr[hh, o0:o0 + MASK_ROWS, :] = ind16
            kaug_scr[hh, o0 + MASK_ROWS:o0 + hd, :] = jnp.zeros((hd - MASK_ROWS, s), BF16)
            vb_scr[hh, o0:o0 + hd, :] = jnp.ones((hd, s), BF16)

    n_i = lax.broadcasted_iota(I32, (SUBLANES, s), 0)
    j_i = lax.broadcasted_iota(I32, (SUBLANES, s), 1) // blk
    ind = jnp.where(n_i == j_i, 1.0, 0.0).astype(BF16)
    ktb = kt_ref[...].astype(BF16)
    km8 = lax.dot_general(ind, ktb, _NT, preferred_element_type=F32) * (1.0 / blk)
    vtb = vt_ref[...].astype(BF16)
    for hh in range(2):
        own = slice(hd * hh, hd * (hh + 1))
        kaug_scr[hh, own, :] = ktb[own]
        vb_scr[hh, own, :] = vtb[own]
    qsb = (q_ref[...] * (hd ** -0.5 * LOG2E)).astype(BF16)
    lane_q = lax.broadcasted_iota(I32, (1, LANES), 1)

    for hh in range(2):
        head_lanes = jnp.where((lane_q >= hd * hh) & (lane_q < hd * (hh + 1)), 1.0, 0.0).astype(BF16)
        qm = qsb * head_lanes
        qm_scr[hh] = qm
        c0 = min(MOBA_TOPK + 1, nb) * blk
        for q0, q1, ranked in ((0, c0, False), (c0, s, True)):
            if q1 == q0:
                continue
            gt = lax.dot_general(km8.astype(BF16), qm[q0:q1], _NT, preferred_element_type=F32)
            n_p = lax.broadcasted_iota(I32, gt.shape, 0)
            j_p = (lax.broadcasted_iota(I32, gt.shape, 1) + q0) // blk
            past = n_p < j_p
            g = jnp.where(past, gt, NEG_INF)
            keep = past & (g > 0.5 * NEG_INF * (hd ** -0.5 * LOG2E))
            if ranked:
                cnt = jnp.zeros(gt.shape, I32)
                for m in range(nb - 1):
                    gm = g[m:m + 1, :]
                    cnt = cnt + jnp.where((gm > g) | ((gm == g) & (n_p > m)), 1, 0)
                keep = keep & (cnt < MOBA_TOPK)
            mpad_scr[hh, :, q0:q1] = jnp.where(keep | (n_p >= j_p), 0.0, NEG_INF)

    lane = lax.broadcasted_iota(I32, (blk, LANES), 1)

    def attend(j, hh, par):
        rows = slice(j * blk, (j + 1) * blk)
        o0 = hd * (1 - hh)
        pieces = [mpad_scr[hh, :, rows], jnp.zeros((LANES - o0 - SUBLANES, blk), F32)]
        if o0:
            pieces.insert(0, jnp.zeros((o0, blk), F32))
        slab = jnp.concatenate(pieces, axis=0)
        qa = qm_scr[hh, rows, :] + slab.T.astype(BF16)
        macc = None
        for n in range(j + 1):
            sc = jnp.dot(qa, kaug_scr[hh, :, n * blk:(n + 1) * blk],
                         preferred_element_type=F32)
            if n == j:
                sc = sc + bias_scr[hh, 0]
            elif n == j - 1:
                sc = sc + bias_scr[hh, 1]
            lg_scr[par, n] = sc
            mx = jnp.maximum(sc[:, 0:LANES], sc[:, LANES:2 * LANES])
            macc = mx if macc is None else jnp.maximum(macc, mx)
        mrow = jnp.max(macc, axis=1, keepdims=True)
        acc = None
        for n in range(j + 1):
            pn = jnp.exp2(lg_scr[par, n] - mrow).astype(BF16)
            pv = lax.dot_general(pn, vb_scr[hh, :, n * blk:(n + 1) * blk], _NT,
                                 preferred_element_type=F32)
            acc = pv if acc is None else acc + pv
        return acc

    def finish(j, accs):
        rows = slice(j * blk, (j + 1) * blk)
        first = lane < hd
        num = jnp.where(first, accs[0], accs[1])
        den = pltpu.roll(jnp.where(first, accs[1], accs[0]), hd, 1)
        o_ref[rows, :] = (num / den * _silu(ga_ref[rows, :])).astype(o_ref.dtype)

    sub8 = lax.broadcasted_iota(I32, (SUBLANES, LANES), 0)
    tiles = hd // SUBLANES

    def score_group(sg, slot):
        def trip(t, carry):
            res = [jnp.zeros((N_HEADS, LANES), F32) for _ in range(SCORE_PAGES)]
            for h in range(N_HEADS):
                qh = [qb_scr[(h * tiles + i) * SUBLANES:(h * tiles + i + 1) * SUBLANES, :]
                      for i in range(tiles)]
                for pp in range(SCORE_PAGES):
                    page = kbuf.at[slot, t * SCORE_PAGES + pp]
                    acc = None
                    for i in range(tiles):
                        r0 = (h * tiles + i) * SUBLANES
                        term = page[r0:r0 + SUBLANES, :] * qh[i]
                        acc = term if acc is None else acc + term
                    for sh in (4, 2, 1):
                        acc = acc + pltpu.roll(acc, sh, 0)
                    res[pp] = jnp.where(sub8 == h, acc, res[pp])
            for bb in range(SCORE_PAGES // 2):
                n = sg * (pg // 2) + t * (SCORE_PAGES // 2) + bb
                s_ref[n, :, 0:LANES] = res[2 * bb]
                s_ref[n, :, LANES:2 * LANES] = res[2 * bb + 1]
                bsum = jnp.sum(res[2 * bb] + res[2 * bb + 1], axis=1, keepdims=True)
                g_scr[n] = jnp.broadcast_to(bsum, (N_HEADS, LANES))
            return carry

        lax.fori_loop(0, pg // SCORE_PAGES, trip, 0)

    items = [(j, hh) for j in reversed(range(nb)) for hh in range(2)]
    work = sum(j + 1 for j, _ in items)
    segments = [[] for _ in range(gps)]
    done = 0
    for item in items:
        segments[min(gps - 1, done * gps // work)].append(item)
        done += item[0] + 1

    accs = {}
    count = 0
    for sg in range(gps):
        g = seq * gps + sg
        wait_group(g)
        score_group(sg, lax.rem(g, K_SLOTS))

        @pl.when(g + K_SLOTS < total)
        def _():
            start_group(g + K_SLOTS)

        for j, hh in segments[sg]:
            accs.setdefault(j, []).append(attend(j, hh, count % 2))
            count += 1
            if len(accs[j]) == 2:
                finish(j, accs.pop(j))

    gate = g_scr[...] * (1.0 / MOBA_BLOCK)
    nidx = lax.broadcasted_iota(I32, gate.shape, 0)
    for r in range(MOBA_TOPK):
        mx = jnp.max(gate, axis=0)
        am = jnp.min(jnp.where(gate == mx[None], nidx, gate.shape[0]), axis=0)
        idx_ref[r] = am
        val_ref[r] = mx
        gate = jnp.where(nidx == am[None], -jnp.inf, gate)


def _attn_and_score(q, kt, vt, ga, rel_bias, pt_flat, q_dec, kcache_t, n_seq, n_pages):
    b, s, w = q.shape
    hd = w // N_HEADS
    blk = MOBA_BLOCK
    nb = s // blk
    npair = w // LANES
    nblk = n_pages // 2
    assert nb <= SUBLANES and LANES == 2 * hd
    assert n_seq == npair * b and n_pages % PAGES_PER_GROUP == 0
    dq = jnp.arange(blk, dtype=I32)
    d_own = dq[:, None] - dq[None, :]
    bkt = jnp.stack([_t5_bucket(d_own), _t5_bucket(d_own + blk)])
    per_seq = lambda p, i, pt: (p * b + i, 0, 0, 0)
    grid_spec = pltpu.PrefetchScalarGridSpec(
        num_scalar_prefetch=1,
        grid=(npair, b),
        in_specs=[pl.BlockSpec(memory_space=pltpu.SMEM),
                  pl.BlockSpec((None, s, LANES), lambda p, i, pt: (i, 0, p)),
                  pl.BlockSpec((None, LANES, s), lambda p, i, pt: (i, p, 0)),
                  pl.BlockSpec((None, LANES, s), lambda p, i, pt: (i, p, 0)),
                  pl.BlockSpec((None, s, LANES), lambda p, i, pt: (i, 0, p)),
                  pl.BlockSpec((2, blk, blk), lambda p, i, pt: (0, 0, 0)),
                  pl.BlockSpec((w, LANES), lambda p, i, pt: (0, 0)),
                  pl.BlockSpec(memory_space=pl.ANY)],
        out_specs=[pl.BlockSpec((None, s, LANES), lambda p, i, pt: (i, 0, p)),
                   pl.BlockSpec((None, nblk, N_HEADS, MOBA_BLOCK), per_seq),
                   pl.BlockSpec((None, MOBA_TOPK, N_HEADS, LANES), per_seq),
                   pl.BlockSpec((None, MOBA_TOPK, N_HEADS, LANES), per_seq)],
        scratch_shapes=[pltpu.VMEM((2, 2, blk, blk), F32),
                        pltpu.VMEM((2, LANES, s), BF16),
                        pltpu.VMEM((2, LANES, s), BF16),
                        pltpu.VMEM((2, SUBLANES, s), F32),
                        pltpu.VMEM((2, s, LANES), BF16),
                        pltpu.VMEM((2, nb, blk, blk), F32),
                        pltpu.VMEM((K_SLOTS, PAGES_PER_GROUP, w, LANES), F32),
                        pltpu.SemaphoreType.DMA((K_SLOTS,)),
                        pltpu.VMEM((w, LANES), F32),
                        pltpu.VMEM((nblk, N_HEADS, LANES), F32)])
    return pl.pallas_call(
        functools.partial(_attn_score_kernel, hd=hd, nb=nb, n_pages=n_pages, n_seq=n_seq),
        out_shape=[jax.ShapeDtypeStruct((b, s, w), BF16),
                   jax.ShapeDtypeStruct((n_seq, nblk, N_HEADS, MOBA_BLOCK), F32),
                   jax.ShapeDtypeStruct((n_seq, MOBA_TOPK, N_HEADS, LANES), I32),
                   jax.ShapeDtypeStruct((n_seq, MOBA_TOPK, N_HEADS, LANES), F32)],
        grid_spec=grid_spec,
        compiler_params=_params("arbitrary", "arbitrary"),
        name="attn_score",
    )(pt_flat, rel_bias, q, kt, vt, ga, bkt, q_dec, kcache_t)


def _outproj_kernel(x_ref, mod_ref, ml_ref, ma_ref, wo_ref, gf_ref, y_ref, *, d, w):
    acc = jnp.dot(ml_ref[...].astype(BF16), wo_ref[0:w, :], preferred_element_type=F32)
    acc = acc + jnp.dot(ma_ref[...].astype(BF16), wo_ref[w:2 * w, :], preferred_element_type=F32)
    out = x_ref[...] + mod_ref[:, 2 * d:3 * d] * acc
    ms = jnp.mean(out * out, axis=-1, keepdims=True)
    y_ref[...] = out * lax.rsqrt(ms + RMS_EPS) * gf_ref[...]


def _outproj_prompt(x, mod3, mix_l, mix_a, w_out, g_final):
    b, s, d = x.shape
    w = mix_l.shape[-1]
    tm = OUT_TILE
    return pl.pallas_call(
        functools.partial(_outproj_kernel, d=d, w=w),
        out_shape=jax.ShapeDtypeStruct((b, s, d), F32),
        grid=(b, s // tm),
        in_specs=[pl.BlockSpec((None, tm, d), lambda i, t: (i, t, 0)),
                  pl.BlockSpec((None, 1, 3 * d), lambda i, t: (i, 0, 0)),
                  pl.BlockSpec((None, tm, w), lambda i, t: (i, t, 0)),
                  pl.BlockSpec((None, tm, w), lambda i, t: (i, t, 0)),
                  pl.BlockSpec((2 * w, d), lambda i, t: (0, 0)),
                  pl.BlockSpec((1, d), lambda i, t: (0, 0))],
        out_specs=pl.BlockSpec((None, tm, d), lambda i, t: (i, t, 0)),
        compiler_params=_params("arbitrary", "arbitrary"),
        name="outproj_prompt",
    )(x, mod3, mix_l, mix_a, w_out, g_final)


def _sample_proj_kernel(x_ref, mod_ref, g_ref, win_ref, wqkt_ref, cbuf_ref, h0_ref, cw_ref, cb_ref,
                        wg_ref, bg_ref, ap_ref,
                        qt_ref, kt_ref, k_ref, v_ref, h_ref, cnew_ref, ml_ref, ga_ref, *, d, w):
    n = x_ref.shape[0]
    xm = _modulated_norm(x_ref[...], mod_ref[...], g_ref[...], d).astype(BF16)
    pr = jnp.dot(xm, win_ref[...], preferred_element_type=F32)
    u = pr[:, 0:w]
    k_ref[...] = pr[:, 3 * w:4 * w]
    v_ref[...] = pr[:, 4 * w:5 * w]
    ga_ref[...] = pr[:, 5 * w:6 * w]
    xpad = jnp.concatenate([xm, jnp.zeros((LANES - n, d), BF16)], axis=0)
    pt = lax.dot_general(wqkt_ref[...], xpad, (((1,), (1,)), ((), ())),
                         preferred_element_type=F32)
    qt_ref[...] = pt[0:w]
    kt_ref[...] = pt[w:2 * w]
    cw = cw_ref[...]
    u_conv = cb_ref[...] + (cbuf_ref[0] * cw[0:1] + cbuf_ref[1] * cw[1:2]
                            + cbuf_ref[2] * cw[2:3] + u * cw[3:4])
    a, bx = _lru_gates(u_conv, wg_ref, bg_ref, ap_ref, w)
    h = a * h0_ref[...] + bx
    h_ref[...] = h
    cnew_ref[0] = cbuf_ref[1]
    cnew_ref[1] = cbuf_ref[2]
    cnew_ref[2] = u
    ml_ref[...] = h * _silu(pr[:, w:2 * w])


def _sample_proj(x, mod, g_norm, w_in_bf, w_qkt, cbuf, h0, conv_w, conv_b, wg, bg, a_param):
    n, d = x.shape
    w = h0.shape[1]
    row = jax.ShapeDtypeStruct((n, w), F32)
    col = jax.ShapeDtypeStruct((w, LANES), F32)
    return pl.pallas_call(
        functools.partial(_sample_proj_kernel, d=d, w=w),
        out_shape=[col, col, row, row, row, jax.ShapeDtypeStruct((CONV_W - 1, n, w), F32), row, row],
        compiler_params=pltpu.CompilerParams(vmem_limit_bytes=VMEM_LIMIT_BYTES),
        name="sample_proj",
    )(x, mod, g_norm, w_in_bf, w_qkt, cbuf, h0, conv_w, conv_b, wg, bg, a_param)


def _decode_kernel(pt_ref, ix_ref, s_ref, idxv_ref, valv_ref, qt_ref, kt_ref, vnew_ref, rbt_ref,
                   bkt_ref, vc_ref, y_ref, vbuf, sem, *, n_pages, n_seq, hd):
    b = pl.program_id(0)
    nsel = MOBA_TOPK
    blk = MOBA_BLOCK
    nb = n_pages // 2
    grp = DEC_SEQS
    n_steps = n_seq // grp

    def tile_copy(step, gi, r, h, pp, slot):
        sq = step * grp + gi
        page = pt_ref[sq * n_pages + 2 * ix_ref[(sq * nsel + r) * N_HEADS + h] + pp]
        return pltpu.make_async_copy(
            vc_ref.at[page, h], vbuf.at[slot, gi, h, :, pl.ds((2 * r + pp) * LANES, LANES)], sem.at[slot])

    def for_tiles(step, fn):
        slot = lax.rem(step, 2)
        for gi in range(grp):
            for r in range(nsel):
                for h in range(N_HEADS):
                    for pp in range(2):
                        fn(tile_copy(step, gi, r, h, pp, slot))

    @pl.when(b == 0)
    def _():
        for_tiles(0, lambda cp: cp.start())

    @pl.when(b + 1 < n_steps)
    def _():
        for_tiles(b + 1, lambda cp: cp.start())

    sub = lax.broadcasted_iota(I32, (N_HEADS, blk), 0)
    rbt = rbt_ref[...]
    bias_last = jnp.zeros((N_HEADS, blk), F32)
    for k in range(REL_BUCKETS):
        bias_last = jnp.where(bkt_ref[...] == k, rbt[:, k:k + 1], bias_last)
    bias_far = rbt[:, REL_BUCKETS - 1:REL_BUCKETS]
    lane = lax.broadcasted_iota(I32, (N_HEADS, LANES), 1)
    qk = jnp.sum((qt_ref[...] * kt_ref[...]).reshape(N_HEADS, hd, LANES), axis=1)

    softmax = []
    for gi in range(grp):
        sq = b * grp + gi
        logits = []
        for r in range(nsel):
            lr = jnp.zeros((N_HEADS, blk), F32)
            for h in range(N_HEADS):
                n = ix_ref[(sq * nsel + r) * N_HEADS + h]
                lr = jnp.where(sub == h, s_ref[gi, n], lr)
            idv = idxv_ref[gi, r][:, 0:1]
            bias = jnp.where(idv == nb - 1, bias_last, bias_far)
            valid = valv_ref[gi, r][:, 0:1] > 0.5 * NEG_INF
            logits.append(jnp.where(valid, lr * (hd ** -0.5) + bias, NEG_INF))

        own = jnp.sum(jnp.where(lane == sq, qk, 0.0), axis=1, keepdims=True)
        l_own = own * (hd ** -0.5) + rbt[:, 0:1]
        m = l_own
        for lr in logits:
            m = jnp.maximum(m, jnp.max(lr, axis=1, keepdims=True))
        p_own = jnp.exp(l_own - m)
        den = p_own
        probs = []
        for lr in logits:
            pr = jnp.exp(lr - m)
            den = den + jnp.sum(pr, axis=1, keepdims=True)
            probs.append(pr)
        softmax.append((jnp.concatenate(probs, axis=1), p_own, den))

    for_tiles(b, lambda cp: cp.wait())
    slot = lax.rem(b, 2)
    for gi in range(grp):
        pcat, p_own, den = softmax[gi]
        rows = []
        for h in range(N_HEADS):
            vt = vbuf[slot, gi, h].astype(BF16)
            rows.append(lax.dot_general(pcat[h:h + 1, :].astype(BF16), vt, (((1,), (1,)), ((), ())),
                                        preferred_element_type=F32))
        acc = jnp.concatenate(rows, axis=0)
        y_ref[gi] = (acc + p_own * vnew_ref[gi]) / den


def _decode(pt_flat, idx_flat, s_all, idxv, valv, qt, kt, v_new, rbt, bkt_last, vcache_t, n_seq, n_pages):
    w = qt.shape[0]
    hd = w // N_HEADS
    nb = n_pages // 2
    grp = DEC_SEQS
    assert n_seq % grp == 0
    zero4 = lambda b, pt, ix: (b, 0, 0, 0)
    grid_spec = pltpu.PrefetchScalarGridSpec(
        num_scalar_prefetch=2,
        grid=(n_seq // grp,),
        in_specs=[pl.BlockSpec((grp, nb, N_HEADS, MOBA_BLOCK), zero4),
                  pl.BlockSpec((grp, MOBA_TOPK, N_HEADS, LANES), zero4),
                  pl.BlockSpec((grp, MOBA_TOPK, N_HEADS, LANES), zero4),
                  pl.BlockSpec((w, LANES), lambda b, pt, ix: (0, 0)),
                  pl.BlockSpec((w, LANES), lambda b, pt, ix: (0, 0)),
                  pl.BlockSpec((grp, N_HEADS, hd), lambda b, pt, ix: (b, 0, 0)),
                  pl.BlockSpec((N_HEADS, REL_BUCKETS), lambda b, pt, ix: (0, 0)),
                  pl.BlockSpec((1, MOBA_BLOCK), lambda b, pt, ix: (0, 0)),
                  pl.BlockSpec(memory_space=pl.ANY)],
        out_specs=pl.BlockSpec((grp, N_HEADS, hd), lambda b, pt, ix: (b, 0, 0)),
        scratch_shapes=[pltpu.VMEM((2, grp, N_HEADS, hd, MOBA_TOPK * MOBA_BLOCK), F32),
                        pltpu.SemaphoreType.DMA((2,))])
    return pl.pallas_call(
        functools.partial(_decode_kernel, n_pages=n_pages, n_seq=n_seq, hd=hd),
        out_shape=jax.ShapeDtypeStruct((n_seq, N_HEADS, hd), F32),
        grid_spec=grid_spec,
        compiler_params=_params("arbitrary"),
        name="decode_attn",
    )(pt_flat, idx_flat, s_all, idxv, valv, qt, kt, v_new, rbt, bkt_last, vcache_t)


def _outproj_sample_kernel(x_ref, mod_ref, ml_ref, ya_ref, ga_ref, wo_ref, gf_ref, y_ref, *, d, w):
    ma = ya_ref[...] * _silu(ga_ref[...])
    acc = jnp.dot(ml_ref[...].astype(BF16), wo_ref[0:w, :], preferred_element_type=F32)
    acc = acc + jnp.dot(ma.astype(BF16), wo_ref[w:2 * w, :], preferred_element_type=F32)
    out = x_ref[...] + mod_ref[:, 2 * d:3 * d] * acc
    ms = jnp.mean(out * out, axis=-1, keepdims=True)
    y_ref[...] = out * lax.rsqrt(ms + RMS_EPS) * gf_ref[...]


def _outproj_sample(x, mod, mix_l, y_att, ga, w_out, g_final):
    n, d = x.shape
    w = mix_l.shape[1]
    return pl.pallas_call(
        functools.partial(_outproj_sample_kernel, d=d, w=w),
        out_shape=jax.ShapeDtypeStruct((n, d), F32),
        compiler_params=pltpu.CompilerParams(vmem_limit_bytes=VMEM_LIMIT_BYTES),
        name="outproj_sample",
    )(x, mod, mix_l, y_att, ga, w_out, g_final)


def _block_diag(wb):
    n, c, dd = wb.shape
    return jnp.einsum("ncd,nm->ncmd", wb, jnp.eye(n, dtype=wb.dtype)).reshape(n * c, n * dd)


def kernel(x_prompt, x_sample, cache_k, cache_v, state_lru_h, state_lru_conv, page_table, c_prompt, c_sample, w_ada, b_ada, g_norm, w_in, conv_w, conv_b, w_rgate, b_rgate, w_igate, b_igate, lru_a_param, rel_bias, w_out, g_final):
    bp, s, d = x_prompt.shape
    ns = x_sample.shape[0]
    w = state_lru_h.shape[1]
    hd = w // N_HEADS
    n_pages = page_table.shape[1]
    page = cache_k.shape[1]
    past = n_pages * page
    assert x_sample.shape[1] == 1 and 2 * page == MOBA_BLOCK and past % MOBA_BLOCK == 0
    assert s % MOBA_BLOCK == 0 and s % ROW_TILE == 0 and s % OUT_TILE == 0 and s % LRU_CHUNK == 0
    assert w % LANES == 0 and ns <= LANES
    assert MOBA_BLOCK + 1 >= REL_MAX_DIST

    w_in_bf, w_in_t = _wprep(w_in, w)
    w_kvt = w_in_t[3 * w:5 * w]
    w_qkt = w_in_t[2 * w:4 * w]
    w_out_bf = w_out.astype(BF16)
    wg = jnp.concatenate([_block_diag(w_rgate), _block_diag(w_igate)], axis=1).astype(BF16)
    bg = jnp.concatenate([b_rgate, b_igate]).reshape(1, 2 * w)
    g_norm2 = g_norm.reshape(1, d)
    g_final2 = g_final.reshape(1, d)
    conv_b2 = conv_b.reshape(1, w)
    a_param2 = lru_a_param.reshape(1, w)

    mod = _adaln(jnp.concatenate([c_prompt, c_sample], axis=0), w_ada, b_ada)
    mod_p = mod[0:bp].reshape(bp, 1, 3 * d)
    mod_s = mod[bp:bp + ns]

    xs = x_sample.reshape(ns, d)
    cbuf = state_lru_conv.transpose(1, 0, 2)
    qt, ktn, k_s, v_s, h_s, cnew, mix_ls, ga_s = _sample_proj(
        xs, mod_s, g_norm2, w_in_bf, w_qkt, cbuf, state_lru_h, conv_w, conv_b2, wg, bg, a_param2)
    pt_flat = page_table.reshape(-1)
    n_phys = cache_k.shape[0]
    kc_t = cache_k.transpose(0, 2, 3, 1).reshape(n_phys, w, page)
    vc_t = cache_v.transpose(0, 2, 3, 1)

    u, gl, q, ga, kt, vt = _inproj(x_prompt, mod_p, g_norm2, w_in_bf, w_kvt)
    mix_l, h_last, tail = _lru_prompt(u, gl, conv_w, conv_b2, wg, bg, a_param2)
    mix_a, s_all, idxv, valv = _attn_and_score(q, kt, vt, ga, rel_bias, pt_flat, qt, kc_t, ns, n_pages)
    y_prompt = _outproj_prompt(x_prompt, mod_p, mix_l, mix_a, w_out_bf, g_final2)
    k_prompt = kt.reshape(bp, N_HEADS, hd, s).transpose(0, 3, 1, 2)
    v_prompt = vt.reshape(bp, N_HEADS, hd, s).transpose(0, 3, 1, 2)
    lru_h_prompt = h_last[:, 0, :]
    lru_conv_prompt = tail[:, SUBLANES - (CONV_W - 1):, :]

    idx_flat = idxv[:, :, :, 0].reshape(-1)
    nb = n_pages // 2
    pos = (nb - 1) * MOBA_BLOCK + jnp.arange(MOBA_BLOCK, dtype=I32)
    bkt_last = _t5_bucket(past - pos).reshape(1, MOBA_BLOCK)
    y_att = _decode(pt_flat, idx_flat, s_all, idxv, valv, qt, ktn, v_s.reshape(ns, N_HEADS, hd),
                    rel_bias.T, bkt_last, vc_t, ns, n_pages)
    y_s = _outproj_sample(xs, mod_s, mix_ls, y_att.reshape(ns, w), ga_s, w_out_bf, g_final2)

    return (y_prompt, y_s.reshape(ns, 1, d), k_prompt, v_prompt, lru_h_prompt, lru_conv_prompt,
            k_s.reshape(ns, 1, N_HEADS, hd), v_s.reshape(ns, 1, N_HEADS, hd), h_s,
            cnew.transpose(1, 0, 2))
```

```python
import functools
import math

import jax
import jax.numpy as jnp
from jax import lax
from jax.experimental import pallas as pl
from jax.experimental.pallas import tpu as pltpu

F32 = jnp.float32
BF16 = jnp.bfloat16
I32 = jnp.int32

N_HEADS = 8
LRU_BLOCKS = 8
CONV_W = 4
LRU_C = 8.0
MOBA_BLOCK = 256
MOBA_TOPK = 3
REL_BUCKETS = 32
REL_MAX_DIST = 128
RMS_EPS = 1e-6
NEG_INF = -1e30
HIGHEST = lax.Precision.HIGHEST
LOG2E = math.log2(math.e)

LANES = 128
SUBLANES = 8
VMEM_LIMIT_BYTES = 56 * 1024 * 1024

ROW_TILE = 1024
OUT_TILE = 1024
LRU_CHUNK = 512
PAGES_PER_GROUP = 64
K_SLOTS = 2
SCORE_PAGES = 16
MASK_ROWS = 16
DEC_SEQS = 4


def _sigmoid(x):
    return 0.5 * jnp.tanh(0.5 * x) + 0.5


def _silu(x):
    return x * _sigmoid(x)


def _softplus(z):
    return jnp.maximum(z, 0.0) + jnp.log1p(jnp.exp(-jnp.abs(z)))


def _t5_bucket(dist):
    n = jnp.maximum(dist, 0)
    max_exact = REL_BUCKETS // 2
    nf = jnp.maximum(n, 1).astype(F32)
    pos = jnp.log(nf / max_exact) / math.log(REL_MAX_DIST / max_exact) * (REL_BUCKETS - max_exact)
    large = max_exact + jnp.where(pos >= 0, jnp.floor(pos), jnp.ceil(pos)).astype(I32)
    large = jnp.minimum(large, REL_BUCKETS - 1)
    return jnp.where(n < max_exact, n, large)


def _params(*sem):
    return pltpu.CompilerParams(dimension_semantics=sem, vmem_limit_bytes=VMEM_LIMIT_BYTES)


def _adaln_kernel(c_ref, w_ref, b_ref, o_ref):
    s = _silu(c_ref[...]).astype(BF16)
    o_ref[...] = jnp.dot(s, w_ref[...].astype(BF16), preferred_element_type=F32) + b_ref[...]


def _adaln(c, w_ada, b_ada):
    n, d = c.shape
    return pl.pallas_call(
        _adaln_kernel,
        out_shape=jax.ShapeDtypeStruct((n, 3 * d), F32),
        grid=(3,),
        in_specs=[pl.BlockSpec((n, d), lambda j: (0, 0)),
                  pl.BlockSpec((d, d), lambda j: (0, j)),
                  pl.BlockSpec((1, d), lambda j: (0, j))],
        out_specs=pl.BlockSpec((n, d), lambda j: (0, j)),
        compiler_params=_params("arbitrary"),
        name="adaln",
    )(c, w_ada, b_ada.reshape(1, 3 * d))


def _wprep_kernel(w_ref, wb_ref, wt_ref):
    x = w_ref[...]
    wb_ref[...] = x.astype(BF16)
    wt_ref[...] = x.T.astype(BF16)


def _wprep(w_in, cols):
    d, n = w_in.shape
    return pl.pallas_call(
        _wprep_kernel,
        out_shape=[jax.ShapeDtypeStruct((d, n), BF16), jax.ShapeDtypeStruct((n, d), BF16)],
        grid=(n // cols,),
        in_specs=[pl.BlockSpec((d, cols), lambda j: (0, j))],
        out_specs=[pl.BlockSpec((d, cols), lambda j: (0, j)), pl.BlockSpec((cols, d), lambda j: (j, 0))],
        compiler_params=_params("arbitrary"),
        name="wprep",
    )(w_in)


def _modulated_norm(x, mod, g, d):
    ms = jnp.mean(x * x, axis=-1, keepdims=True)
    xn = x * lax.rsqrt(ms + RMS_EPS) * g
    return xn * (1.0 + mod[:, d:2 * d]) + mod[:, 0:d]


def _inproj_kernel(x_ref, mod_ref, g_ref, win_ref, wkvt_ref,
                   u_ref, gl_ref, q_ref, ga_ref, kt_ref, vt_ref, *, d, w):
    xm = _modulated_norm(x_ref[...], mod_ref[...], g_ref[...], d).astype(BF16)
    pr = jnp.dot(xm, win_ref[:, 0:3 * w], preferred_element_type=F32)
    u_ref[...] = pr[:, 0:w]
    gl_ref[...] = pr[:, w:2 * w]
    q_ref[...] = pr[:, 2 * w:3 * w]
    ga_ref[...] = jnp.dot(xm, win_ref[:, 5 * w:6 * w], preferred_element_type=F32)
    pt = lax.dot_general(wkvt_ref[...], xm, (((1,), (1,)), ((), ())),
                         preferred_element_type=F32)
    kt_ref[...] = pt[0:w]
    vt_ref[...] = pt[w:2 * w]


def _inproj(x, mod3, g_norm, w_in_bf, w_kvt):
    b, s, d = x.shape
    w = w_in_bf.shape[1] // 6
    tm = ROW_TILE
    row = pl.BlockSpec((None, tm, w), lambda i, t: (i, t, 0))
    col = pl.BlockSpec((None, w, tm), lambda i, t: (i, 0, t))
    return pl.pallas_call(
        functools.partial(_inproj_kernel, d=d, w=w),
        out_shape=[jax.ShapeDtypeStruct((b, s, w), F32)] * 4
        + [jax.ShapeDtypeStruct((b, w, s), F32)] * 2,
        grid=(b, s // tm),
        in_specs=[pl.BlockSpec((None, tm, d), lambda i, t: (i, t, 0)),
                  pl.BlockSpec((None, 1, 3 * d), lambda i, t: (i, 0, 0)),
                  pl.BlockSpec((1, d), lambda i, t: (0, 0)),
                  pl.BlockSpec((d, 6 * w), lambda i, t: (0, 0)),
                  pl.BlockSpec((2 * w, d), lambda i, t: (0, 0))],
        out_specs=[row, row, row, row, col, col],
        compiler_params=_params("arbitrary", "arbitrary"),
        name="inproj",
    )(x, mod3, g_norm, w_in_bf, w_kvt)


def _lru_gates(u_conv, wg_ref, bg_ref, ap_ref, w):
    g2 = jnp.dot(u_conv.astype(BF16), wg_ref[...], preferred_element_type=F32) + bg_ref[...]
    r = _sigmoid(g2[:, 0:w])
    i = _sigmoid(g2[:, w:2 * w])
    log_a = (-LRU_C * r) * _softplus(-ap_ref[...])
    a = jnp.exp(log_a)
    bx = jnp.sqrt(-jnp.tanh(log_a) * (a * a + 1.0)) * (i * u_conv)
    return a, bx


def _lru_kernel(u_ref, gl_ref, cw_ref, cb_ref, wg_ref, bg_ref, ap_ref,
                mix_ref, hl_ref, tail_ref, a_scr, b_scr, hs_scr, h_scr, tail_scr, *, tc, w):
    c = pl.program_id(1)

    @pl.when(c == 0)
    def _():
        h_scr[...] = jnp.zeros_like(h_scr)
        tail_scr[...] = jnp.zeros_like(tail_scr)

    u = u_ref[...]
    ext = jnp.concatenate([tail_scr[...], u], axis=0)
    cw = cw_ref[...]
    u_conv = cb_ref[...] + (ext[5:5 + tc] * cw[0:1] + ext[6:6 + tc] * cw[1:2]
                            + ext[7:7 + tc] * cw[2:3] + u * cw[3:4])
    tail_scr[...] = u[tc - SUBLANES:tc]
    a, bx = _lru_gates(u_conv, wg_ref, bg_ref, ap_ref, w)
    a_scr[...] = a
    b_scr[...] = bx

    row = lax.broadcasted_iota(I32, (SUBLANES, w), 0)

    def tile(t, h):
        r0 = pl.multiple_of(t * SUBLANES, SUBLANES)
        at = a_scr[pl.ds(r0, SUBLANES), :]
        bt = b_scr[pl.ds(r0, SUBLANES), :]
        for dd in (1, 2, 4):
            keep = row >= dd
            bt = jnp.where(keep, at * pltpu.roll(bt, dd, 0) + bt, bt)
            at = jnp.where(keep, at * pltpu.roll(at, dd, 0), at)
        hs = at * h + bt
        hs_scr[pl.ds(r0, SUBLANES), :] = hs
        return jnp.broadcast_to(hs[SUBLANES - 1:SUBLANES, :], (SUBLANES, w))

    h = lax.fori_loop(0, tc // SUBLANES, tile, h_scr[...], unroll=4)
    h_scr[...] = h
    hl_ref[...] = h
    tail_ref[...] = u[tc - SUBLANES:tc]
    mix_ref[...] = (hs_scr[...] * _silu(gl_ref[...])).astype(mix_ref.dtype)


def _lru_prompt(u, gl, conv_w, conv_b, wg, bg, a_param):
    b, s, w = u.shape
    tc = LRU_CHUNK
    const = lambda shape: pl.BlockSpec(shape, lambda i, t: (0,) * len(shape))
    return pl.pallas_call(
        functools.partial(_lru_kernel, tc=tc, w=w),
        out_shape=[jax.ShapeDtypeStruct((b, s, w), BF16),
                   jax.ShapeDtypeStruct((b, SUBLANES, w), F32),
                   jax.ShapeDtypeStruct((b, SUBLANES, w), F32)],
        grid=(b, s // tc),
        in_specs=[pl.BlockSpec((None, tc, w), lambda i, t: (i, t, 0)),
                  pl.BlockSpec((None, tc, w), lambda i, t: (i, t, 0)),
                  const((CONV_W, w)), const((1, w)), const((w, 2 * w)), const((1, 2 * w)),
                  const((1, w))],
        out_specs=[pl.BlockSpec((None, tc, w), lambda i, t: (i, t, 0)),
                   pl.BlockSpec((None, SUBLANES, w), lambda i, t: (i, 0, 0)),
                   pl.BlockSpec((None, SUBLANES, w), lambda i, t: (i, 0, 0))],
        scratch_shapes=[pltpu.VMEM((tc, w), F32), pltpu.VMEM((tc, w), F32),
                        pltpu.VMEM((tc, w), F32), pltpu.VMEM((SUBLANES, w), F32),
                        pltpu.VMEM((SUBLANES, w), F32)],
        compiler_params=_params("arbitrary", "arbitrary"),
        name="lru_prompt",
    )(u, gl, conv_w, conv_b, wg, bg, a_param)


_NT = (((1,), (1,)), ((), ()))


def _attn_score_kernel(pt_ref, rb_ref, q_ref, kt_ref, vt_ref, ga_ref, bkt_ref, qt_ref, kc_ref,
                       o_ref, s_ref, idx_ref, val_ref,
                       bias_scr, kaug_scr, vb_scr, mpad_scr, qm_scr, lg_scr, kbuf, sem, qb_scr, g_scr,
                       *, hd, nb, n_pages, n_seq):
    p = pl.program_id(0)
    b = pl.program_id(1)
    blk = MOBA_BLOCK
    s = nb * blk
    pg = PAGES_PER_GROUP
    gps = n_pages // pg
    total = n_seq * gps
    seq = p * pl.num_programs(1) + b

    def page_copy(g, jj, slot):
        return pltpu.make_async_copy(kc_ref.at[pt_ref[g * pg + jj]], kbuf.at[slot, jj], sem.at[slot])

    def start_group(g):
        slot = lax.rem(g, K_SLOTS)
        for jj in range(pg):
            page_copy(g, jj, slot).start()

    def wait_group(g):
        slot = lax.rem(g, K_SLOTS)
        for jj in range(pg):
            page_copy(g, jj, slot).wait()

    @pl.when(seq == 0)
    def _():
        for g in range(K_SLOTS):
            start_group(g)

    lane_t = lax.broadcasted_iota(I32, qt_ref.shape, 1)
    qcol = jnp.sum(jnp.where(lane_t == seq, qt_ref[...], 0.0), axis=1, keepdims=True)
    qb_scr[...] = jnp.broadcast_to(qcol, qb_scr.shape)

    @pl.when(b == 0)
    def _():
        qi = lax.broadcasted_iota(I32, (blk, blk), 0)
        ki = lax.broadcasted_iota(I32, (blk, blk), 1)
        for hh in range(2):
            h = 2 * p + hh
            far = rb_ref[REL_BUCKETS - 1, h]
            for which in range(2):
                bk = bkt_ref[which]
                tab = jnp.zeros((blk, blk), F32)
                for k in range(REL_BUCKETS):
                    tab = jnp.where(bk == k, (rb_ref[k, h] - far) * LOG2E, tab)
                if which == 0:
                    tab = jnp.where(qi >= ki, tab, NEG_INF)
                bias_scr[hh, which] = tab
        i16 = lax.broadcasted_iota(I32, (MASK_ROWS, s), 0)
        ind16 = jnp.where(i16 == lax.broadcasted_iota(I32, (MASK_ROWS, s), 1) // blk, 1.0, 0.0).astype(BF16)
        for hh in range(2):
            o0 = hd * (1 - hh)
            kaug_scr[hh, o0:o0 + MASK_ROWS, :] = ind16
            kaug_scr[hh, o0 + MASK_ROWS:o0 + hd, :] = jnp.zeros((hd - MASK_ROWS, s), BF16)
            vb_scr[hh, o0:o0 + hd, :] = jnp.ones((hd, s), BF16)

    def prepare():
        n_i = lax.broadcasted_iota(I32, (SUBLANES, s), 0)
        j_i = lax.broadcasted_iota(I32, (SUBLANES, s), 1) // blk
        ind = jnp.where(n_i == j_i, 1.0, 0.0).astype(BF16)
        ktb = kt_ref[...].astype(BF16)
        km8 = lax.dot_general(ind, ktb, _NT, preferred_element_type=F32) * (1.0 / blk)
        vtb = vt_ref[...].astype(BF16)
        for hh in range(2):
            own = slice(hd * hh, hd * (hh + 1))
            kaug_scr[hh, own, :] = ktb[own]
            vb_scr[hh, own, :] = vtb[own]
        qsb = (q_ref[...] * (hd ** -0.5 * LOG2E)).astype(BF16)
        lane_q = lax.broadcasted_iota(I32, (1, LANES), 1)

        for hh in range(2):
            head_lanes = jnp.where((lane_q >= hd * hh) & (lane_q < hd * (hh + 1)), 1.0, 0.0).astype(BF16)
            qm = qsb * head_lanes
            qm_scr[hh] = qm
            c0 = min(MOBA_TOPK + 1, nb) * blk
            for q0, q1, ranked in ((0, c0, False), (c0, s, True)):
                if q1 == q0:
                    continue
                gt = lax.dot_general(km8.astype(BF16), qm[q0:q1], _NT, preferred_element_type=F32)
                n_p = lax.broadcasted_iota(I32, gt.shape, 0)
                j_p = (lax.broadcasted_iota(I32, gt.shape, 1) + q0) // blk
                past = n_p < j_p
                g = jnp.where(past, gt, NEG_INF)
                keep = past & (g > 0.5 * NEG_INF * (hd ** -0.5 * LOG2E))
                if ranked:
                    cnt = jnp.zeros(gt.shape, I32)
                    for m in range(nb - 1):
                        gm = g[m:m + 1, :]
                        cnt = cnt + jnp.where((gm > g) | ((gm == g) & (n_p > m)), 1, 0)
                    keep = keep & (cnt < MOBA_TOPK)
                mpad_scr[hh, :, q0:q1] = jnp.where(keep | (n_p >= j_p), 0.0, NEG_INF)

    lane = lax.broadcasted_iota(I32, (blk, LANES), 1)

    def attend(j, hh, par):
        rows = slice(j * blk, (j + 1) * blk)
        qa = qm_scr[hh, rows, :]
        if j > 0:
            o0 = hd * (1 - hh)
            pieces = [mpad_scr[hh, :, rows], jnp.zeros((LANES - o0 - SUBLANES, blk), F32)]
            if o0:
                pieces.insert(0, jnp.zeros((o0, blk), F32))
            slab = jnp.concatenate(pieces, axis=0)
            qa = qa + slab.T.astype(BF16)
        macc = None
        for n in range(j + 1):
            sc = jnp.dot(qa, kaug_scr[hh, :, n * blk:(n + 1) * blk],
                         preferred_element_type=F32)
            if n == j:
                sc = sc + bias_scr[hh, 0]
            elif n == j - 1:
                sc = sc + bias_scr[hh, 1]
            lg_scr[par, n] = sc
            mx = jnp.maximum(sc[:, 0:LANES], sc[:, LANES:2 * LANES])
            macc = mx if macc is None else jnp.maximum(macc, mx)
        mrow = jnp.max(macc, axis=1, keepdims=True)
        acc = None
        for n in range(j + 1):
            pn = jnp.exp2(lg_scr[par, n] - mrow).astype(BF16)
            pv = lax.dot_general(pn, vb_scr[hh, :, n * blk:(n + 1) * blk], _NT,
                                 preferred_element_type=F32)
            acc = pv if acc is None else acc + pv
        return acc

    def finish(j, accs):
        rows = slice(j * blk, (j + 1) * blk)
        first = lane < hd
        num = jnp.where(first, accs[0], accs[1])
        den = pltpu.roll(jnp.where(first, accs[1], accs[0]), hd, 1)
        o_ref[rows, :] = (num / den * _silu(ga_ref[rows, :])).astype(o_ref.dtype)

    sub8 = lax.broadcasted_iota(I32, (SUBLANES, LANES), 0)
    tiles = hd // SUBLANES

    def score_group(sg, slot):
        def trip(t, carry):
            res = [jnp.zeros((N_HEADS, LANES), F32) for _ in range(SCORE_PAGES)]
            for h in range(N_HEADS):
                qh = [qb_scr[(h * tiles + i) * SUBLANES:(h * tiles + i + 1) * SUBLANES, :]
                      for i in range(tiles)]
                for pp in range(SCORE_PAGES):
                    page = kbuf.at[slot, t * SCORE_PAGES + pp]
                    acc = None
                    for i in range(tiles):
                        r0 = (h * tiles + i) * SUBLANES
                        term = page[r0:r0 + SUBLANES, :] * qh[i]
                        acc = term if acc is None else acc + term
                    for sh in (4, 2, 1):
                        acc = acc + pltpu.roll(acc, sh, 0)
                    res[pp] = jnp.where(sub8 == h, acc, res[pp])
            for bb in range(SCORE_PAGES // 2):
                n = sg * (pg // 2) + t * (SCORE_PAGES // 2) + bb
                s_ref[n, :, 0:LANES] = res[2 * bb]
                s_ref[n, :, LANES:2 * LANES] = res[2 * bb + 1]
                bsum = jnp.sum(res[2 * bb] + res[2 * bb + 1], axis=1, keepdims=True)
                g_scr[n] = jnp.broadcast_to(bsum, (N_HEADS, LANES))
            return carry

        lax.fori_loop(0, pg // SCORE_PAGES, trip, 0)

    items = [(0, hh) for hh in range(2)] + [(j, hh) for j in reversed(range(1, nb)) for hh in range(2)]
    work = sum(j + 1 for j, _ in items)
    segments = [[] for _ in range(gps)]
    done = 0
    for item in items:
        segments[min(gps - 1, done * gps // work)].append(item)
        done += item[0] + 1

    accs = {}
    count = 0
    for sg in range(gps):
        g = seq * gps + sg
        wait_group(g)
        score_group(sg, lax.rem(g, K_SLOTS))

        start_group(lax.rem(g + K_SLOTS, total))

        if sg == 0:
            prepare()
        for j, hh in segments[sg]:
            accs.setdefault(j, []).append(attend(j, hh, count % 2))
            count += 1
            if len(accs[j]) == 2:
                finish(j, accs.pop(j))

    @pl.when(seq == n_seq - 1)
    def _():
        for g in range(K_SLOTS):
            wait_group(g)

    gate = g_scr[...] * (1.0 / MOBA_BLOCK)
    nidx = lax.broadcasted_iota(I32, gate.shape, 0)
    for r in range(MOBA_TOPK):
        mx = jnp.max(gate, axis=0)
        am = jnp.min(jnp.where(gate == mx[None], nidx, gate.shape[0]), axis=0)
        idx_ref[r] = am
        val_ref[r] = mx
        gate = jnp.where(nidx == am[None], -jnp.inf, gate)


def _attn_and_score(q, kt, vt, ga, rel_bias, pt_flat, q_dec, kcache_t, n_seq, n_pages):
    b, s, w = q.shape
    hd = w // N_HEADS
    blk = MOBA_BLOCK
    nb = s // blk
    npair = w // LANES
    nblk = n_pages // 2
    assert nb <= SUBLANES and LANES == 2 * hd
    assert n_seq == npair * b and n_pages % PAGES_PER_GROUP == 0
    assert (n_seq * (n_pages // PAGES_PER_GROUP)) % K_SLOTS == 0
    dq = jnp.arange(blk, dtype=I32)
    d_own = dq[:, None] - dq[None, :]
    bkt = jnp.stack([_t5_bucket(d_own), _t5_bucket(d_own + blk)])
    per_seq = lambda p, i, pt: (p * b + i, 0, 0, 0)
    grid_spec = pltpu.PrefetchScalarGridSpec(
        num_scalar_prefetch=1,
        grid=(npair, b),
        in_specs=[pl.BlockSpec(memory_space=pltpu.SMEM),
                  pl.BlockSpec((None, s, LANES), lambda p, i, pt: (i, 0, p)),
                  pl.BlockSpec((None, LANES, s), lambda p, i, pt: (i, p, 0)),
                  pl.BlockSpec((None, LANES, s), lambda p, i, pt: (i, p, 0)),
                  pl.BlockSpec((None, s, LANES), lambda p, i, pt: (i, 0, p)),
                  pl.BlockSpec((2, blk, blk), lambda p, i, pt: (0, 0, 0)),
                  pl.BlockSpec((w, LANES), lambda p, i, pt: (0, 0)),
                  pl.BlockSpec(memory_space=pl.ANY)],
        out_specs=[pl.BlockSpec((None, s, LANES), lambda p, i, pt: (i, 0, p)),
                   pl.BlockSpec((None, nblk, N_HEADS, MOBA_BLOCK), per_seq),
                   pl.BlockSpec((None, MOBA_TOPK, N_HEADS, LANES), per_seq),
                   pl.BlockSpec((None, MOBA_TOPK, N_HEADS, LANES), per_seq)],
        scratch_shapes=[pltpu.VMEM((2, 2, blk, blk), F32),
                        pltpu.VMEM((2, LANES, s), BF16),
                        pltpu.VMEM((2, LANES, s), BF16),
                        pltpu.VMEM((2, SUBLANES, s), F32),
                        pltpu.VMEM((2, s, LANES), BF16),
                        pltpu.VMEM((2, nb, blk, blk), F32),
                        pltpu.VMEM((K_SLOTS, PAGES_PER_GROUP, w, LANES), F32),
                        pltpu.SemaphoreType.DMA((K_SLOTS,)),
                        pltpu.VMEM((w, LANES), F32),
                        pltpu.VMEM((nblk, N_HEADS, LANES), F32)])
    return pl.pallas_call(
        functools.partial(_attn_score_kernel, hd=hd, nb=nb, n_pages=n_pages, n_seq=n_seq),
        out_shape=[jax.ShapeDtypeStruct((b, s, w), BF16),
                   jax.ShapeDtypeStruct((n_seq, nblk, N_HEADS, MOBA_BLOCK), F32),
                   jax.ShapeDtypeStruct((n_seq, MOBA_TOPK, N_HEADS, LANES), I32),
                   jax.ShapeDtypeStruct((n_seq, MOBA_TOPK, N_HEADS, LANES), F32)],
        grid_spec=grid_spec,
        compiler_params=_params("arbitrary", "arbitrary"),
        name="attn_score",
    )(pt_flat, rel_bias, q, kt, vt, ga, bkt, q_dec, kcache_t)


def _outproj_kernel(x_ref, mod_ref, ml_ref, ma_ref, wo_ref, gf_ref, y_ref, *, d, w):
    acc = jnp.dot(ml_ref[...].astype(BF16), wo_ref[0:w, :], preferred_element_type=F32)
    acc = acc + jnp.dot(ma_ref[...].astype(BF16), wo_ref[w:2 * w, :], preferred_element_type=F32)
    out = x_ref[...] + mod_ref[:, 2 * d:3 * d] * acc
    ms = jnp.mean(out * out, axis=-1, keepdims=True)
    y_ref[...] = out * lax.rsqrt(ms + RMS_EPS) * gf_ref[...]


def _outproj_prompt(x, mod3, mix_l, mix_a, w_out, g_final):
    b, s, d = x.shape
    w = mix_l.shape[-1]
    tm = OUT_TILE
    return pl.pallas_call(
        functools.partial(_outproj_kernel, d=d, w=w),
        out_shape=jax.ShapeDtypeStruct((b, s, d), F32),
        grid=(b, s // tm),
        in_specs=[pl.BlockSpec((None, tm, d), lambda i, t: (i, t, 0)),
                  pl.BlockSpec((None, 1, 3 * d), lambda i, t: (i, 0, 0)),
                  pl.BlockSpec((None, tm, w), lambda i, t: (i, t, 0)),
                  pl.BlockSpec((None, tm, w), lambda i, t: (i, t, 0)),
                  pl.BlockSpec((2 * w, d), lambda i, t: (0, 0)),
                  pl.BlockSpec((1, d), lambda i, t: (0, 0))],
        out_specs=pl.BlockSpec((None, tm, d), lambda i, t: (i, t, 0)),
        compiler_params=_params("arbitrary", "arbitrary"),
        name="outproj_prompt",
    )(x, mod3, mix_l, mix_a, w_out, g_final)


def _sample_proj_kernel(x_ref, mod_ref, g_ref, win_ref, wqkt_ref, cbuf_ref, h0_ref, cw_ref, cb_ref,
                        wg_ref, bg_ref, ap_ref,
                        qt_ref, kt_ref, k_ref, v_ref, h_ref, cnew_ref, ml_ref, ga_ref, *, d, w):
    n = x_ref.shape[0]
    xm = _modulated_norm(x_ref[...], mod_ref[...], g_ref[...], d).astype(BF16)
    pr = jnp.dot(xm, win_ref[...], preferred_element_type=F32)
    u = pr[:, 0:w]
    k_ref[...] = pr[:, 3 * w:4 * w]
    v_ref[...] = pr[:, 4 * w:5 * w]
    ga_ref[...] = pr[:, 5 * w:6 * w]
    xpad = jnp.concatenate([xm, jnp.zeros((LANES - n, d), BF16)], axis=0)
    pt = lax.dot_general(wqkt_ref[...], xpad, (((1,), (1,)), ((), ())),
                         preferred_element_type=F32)
    qt_ref[...] = pt[0:w]
    kt_ref[...] = pt[w:2 * w]
    cw = cw_ref[...]
    u_conv = cb_ref[...] + (cbuf_ref[0] * cw[0:1] + cbuf_ref[1] * cw[1:2]
                            + cbuf_ref[2] * cw[2:3] + u * cw[3:4])
    a, bx = _lru_gates(u_conv, wg_ref, bg_ref, ap_ref, w)
    h = a * h0_ref[...] + bx
    h_ref[...] = h
    cnew_ref[0] = cbuf_ref[1]
    cnew_ref[1] = cbuf_ref[2]
    cnew_ref[2] = u
    ml_ref[...] = h * _silu(pr[:, w:2 * w])


def _sample_proj(x, mod, g_norm, w_in_bf, w_qkt, cbuf, h0, conv_w, conv_b, wg, bg, a_param):
    n, d = x.shape
    w = h0.shape[1]
    row = jax.ShapeDtypeStruct((n, w), F32)
    col = jax.ShapeDtypeStruct((w, LANES), F32)
    return pl.pallas_call(
        functools.partial(_sample_proj_kernel, d=d, w=w),
        out_shape=[col, col, row, row, row, jax.ShapeDtypeStruct((CONV_W - 1, n, w), F32), row, row],
        compiler_params=pltpu.CompilerParams(vmem_limit_bytes=VMEM_LIMIT_BYTES),
        name="sample_proj",
    )(x, mod, g_norm, w_in_bf, w_qkt, cbuf, h0, conv_w, conv_b, wg, bg, a_param)


def _decode_kernel(pt_ref, ix_ref, s_ref, idxv_ref, valv_ref, qt_ref, kt_ref, vnew_ref, rbt_ref,
                   bkt_ref, vc_ref, y_ref, vbuf, sem, *, n_pages, n_seq, hd):
    b = pl.program_id(0)
    nsel = MOBA_TOPK
    blk = MOBA_BLOCK
    nb = n_pages // 2
    grp = DEC_SEQS
    n_steps = n_seq // grp

    def tile_copy(step, gi, r, h, pp, slot):
        sq = step * grp + gi
        page = pt_ref[sq * n_pages + 2 * ix_ref[(sq * nsel + r) * N_HEADS + h] + pp]
        return pltpu.make_async_copy(
            vc_ref.at[page, h], vbuf.at[slot, gi, h, :, pl.ds((2 * r + pp) * LANES, LANES)], sem.at[slot])

    def for_tiles(step, fn):
        slot = lax.rem(step, 2)
        for gi in range(grp):
            for r in range(nsel):
                for h in range(N_HEADS):
                    for pp in range(2):
                        fn(tile_copy(step, gi, r, h, pp, slot))

    @pl.when(b == 0)
    def _():
        for_tiles(0, lambda cp: cp.start())

    for_tiles(lax.rem(b + 1, n_steps), lambda cp: cp.start())

    sub = lax.broadcasted_iota(I32, (N_HEADS, blk), 0)
    rbt = rbt_ref[...]
    bias_last = jnp.zeros((N_HEADS, blk), F32)
    for k in range(REL_BUCKETS):
        bias_last = jnp.where(bkt_ref[...] == k, rbt[:, k:k + 1], bias_last)
    bias_far = rbt[:, REL_BUCKETS - 1:REL_BUCKETS]
    lane = lax.broadcasted_iota(I32, (N_HEADS, LANES), 1)
    qk = jnp.sum((qt_ref[...] * kt_ref[...]).reshape(N_HEADS, hd, LANES), axis=1)

    softmax = []
    for gi in range(grp):
        sq = b * grp + gi
        logits = []
        for r in range(nsel):
            lr = jnp.zeros((N_HEADS, blk), F32)
            for h in range(N_HEADS):
                n = ix_ref[(sq * nsel + r) * N_HEADS + h]
                lr = jnp.where(sub == h, s_ref[gi, n], lr)
            idv = idxv_ref[gi, r][:, 0:1]
            bias = jnp.where(idv == nb - 1, bias_last, bias_far)
            valid = valv_ref[gi, r][:, 0:1] > 0.5 * NEG_INF
            logits.append(jnp.where(valid, lr * (hd ** -0.5) + bias, NEG_INF))

        own = jnp.sum(jnp.where(lane == sq, qk, 0.0), axis=1, keepdims=True)
        l_own = own * (hd ** -0.5) + rbt[:, 0:1]
        m = l_own
        for lr in logits:
            m = jnp.maximum(m, jnp.max(lr, axis=1, keepdims=True))
        p_own = jnp.exp(l_own - m)
        den = p_own
        probs = []
        for lr in logits:
            pr = jnp.exp(lr - m)
            den = den + jnp.sum(pr, axis=1, keepdims=True)
            probs.append(pr)
        softmax.append((jnp.concatenate(probs, axis=1), p_own, den))

    for_tiles(b, lambda cp: cp.wait())
    slot = lax.rem(b, 2)
    for gi in range(grp):
        pcat, p_own, den = softmax[gi]
        rows = []
        for h in range(N_HEADS):
            vt = vbuf[slot, gi, h].astype(BF16)
            rows.append(lax.dot_general(pcat[h:h + 1, :].astype(BF16), vt, (((1,), (1,)), ((), ())),
                                        preferred_element_type=F32))
        acc = jnp.concatenate(rows, axis=0)
        y_ref[gi] = (acc + p_own * vnew_ref[gi]) / den

    @pl.when(b == n_steps - 1)
    def _():
        for_tiles(0, lambda cp: cp.wait())


def _decode(pt_flat, idx_flat, s_all, idxv, valv, qt, kt, v_new, rbt, bkt_last, vcache_t, n_seq, n_pages):
    w = qt.shape[0]
    hd = w // N_HEADS
    nb = n_pages // 2
    grp = DEC_SEQS
    assert n_seq % (2 * grp) == 0
    zero4 = lambda b, pt, ix: (b, 0, 0, 0)
    grid_spec = pltpu.PrefetchScalarGridSpec(
        num_scalar_prefetch=2,
        grid=(n_seq // grp,),
        in_specs=[pl.BlockSpec((grp, nb, N_HEADS, MOBA_BLOCK), zero4),
                  pl.BlockSpec((grp, MOBA_TOPK, N_HEADS, LANES), zero4),
                  pl.BlockSpec((grp, MOBA_TOPK, N_HEADS, LANES), zero4),
                  pl.BlockSpec((w, LANES), lambda b, pt, ix: (0, 0)),
                  pl.BlockSpec((w, LANES), lambda b, pt, ix: (0, 0)),
                  pl.BlockSpec((grp, N_HEADS, hd), lambda b, pt, ix: (b, 0, 0)),
                  pl.BlockSpec((N_HEADS, REL_BUCKETS), lambda b, pt, ix: (0, 0)),
                  pl.BlockSpec((1, MOBA_BLOCK), lambda b, pt, ix: (0, 0)),
                  pl.BlockSpec(memory_space=pl.ANY)],
        out_specs=pl.BlockSpec((grp, N_HEADS, hd), lambda b, pt, ix: (b, 0, 0)),
        scratch_shapes=[pltpu.VMEM((2, grp, N_HEADS, hd, MOBA_TOPK * MOBA_BLOCK), F32),
                        pltpu.SemaphoreType.DMA((2,))])
    return pl.pallas_call(
        functools.partial(_decode_kernel, n_pages=n_pages, n_seq=n_seq, hd=hd),
        out_shape=jax.ShapeDtypeStruct((n_seq, N_HEADS, hd), F32),
        grid_spec=grid_spec,
        compiler_params=_params("arbitrary"),
        name="decode_attn",
    )(pt_flat, idx_flat, s_all, idxv, valv, qt, kt, v_new, rbt, bkt_last, vcache_t)


def _outproj_sample_kernel(x_ref, mod_ref, ml_ref, ya_ref, ga_ref, wo_ref, gf_ref, y_ref, *, d, w):
    ma = ya_ref[...] * _silu(ga_ref[...])
    acc = jnp.dot(ml_ref[...].astype(BF16), wo_ref[0:w, :], preferred_element_type=F32)
    acc = acc + jnp.dot(ma.astype(BF16), wo_ref[w:2 * w, :], preferred_element_type=F32)
    out = x_ref[...] + mod_ref[:, 2 * d:3 * d] * acc
    ms = jnp.mean(out * out, axis=-1, keepdims=True)
    y_ref[...] = out * lax.rsqrt(ms + RMS_EPS) * gf_ref[...]


def _outproj_sample(x, mod, mix_l, y_att, ga, w_out, g_final):
    n, d = x.shape
    w = mix_l.shape[1]
    return pl.pallas_call(
        functools.partial(_outproj_sample_kernel, d=d, w=w),
        out_shape=jax.ShapeDtypeStruct((n, d), F32),
        compiler_params=pltpu.CompilerParams(vmem_limit_bytes=VMEM_LIMIT_BYTES),
        name="outproj_sample",
    )(x, mod, mix_l, y_att, ga, w_out, g_final)


def _block_diag(wb):
    n, c, dd = wb.shape
    return jnp.einsum("ncd,nm->ncmd", wb, jnp.eye(n, dtype=wb.dtype)).reshape(n * c, n * dd)


def kernel(x_prompt, x_sample, cache_k, cache_v, state_lru_h, state_lru_conv, page_table, c_prompt, c_sample, w_ada, b_ada, g_norm, w_in, conv_w, conv_b, w_rgate, b_rgate, w_igate, b_igate, lru_a_param, rel_bias, w_out, g_final):
    bp, s, d = x_prompt.shape
    ns = x_sample.shape[0]
    w = state_lru_h.shape[1]
    hd = w // N_HEADS
    n_pages = page_table.shape[1]
    page = cache_k.shape[1]
    past = n_pages * page
    assert x_sample.shape[1] == 1 and 2 * page == MOBA_BLOCK and past % MOBA_BLOCK == 0
    assert s % MOBA_BLOCK == 0 and s % ROW_TILE == 0 and s % OUT_TILE == 0 and s % LRU_CHUNK == 0
    assert w % LANES == 0 and ns <= LANES
    assert MOBA_BLOCK + 1 >= REL_MAX_DIST

    w_in_bf, w_in_t = _wprep(w_in, w)
    w_kvt = w_in_t[3 * w:5 * w]
    w_qkt = w_in_t[2 * w:4 * w]
    w_out_bf = w_out.astype(BF16)
    wg = jnp.concatenate([_block_diag(w_rgate), _block_diag(w_igate)], axis=1).astype(BF16)
    bg = jnp.concatenate([b_rgate, b_igate]).reshape(1, 2 * w)
    g_norm2 = g_norm.reshape(1, d)
    g_final2 = g_final.reshape(1, d)
    conv_b2 = conv_b.reshape(1, w)
    a_param2 = lru_a_param.reshape(1, w)

    mod = _adaln(jnp.concatenate([c_prompt, c_sample], axis=0), w_ada, b_ada)
    mod_p = mod[0:bp].reshape(bp, 1, 3 * d)
    mod_s = mod[bp:bp + ns]

    xs = x_sample.reshape(ns, d)
    cbuf = state_lru_conv.transpose(1, 0, 2)
    qt, ktn, k_s, v_s, h_s, cnew, mix_ls, ga_s = _sample_proj(
        xs, mod_s, g_norm2, w_in_bf, w_qkt, cbuf, state_lru_h, conv_w, conv_b2, wg, bg, a_param2)
    pt_flat = page_table.reshape(-1)
    n_phys = cache_k.shape[0]
    kc_t = cache_k.transpose(0, 2, 3, 1).reshape(n_phys, w, page)
    vc_t = cache_v.transpose(0, 2, 3, 1)

    u, gl, q, ga, kt, vt = _inproj(x_prompt, mod_p, g_norm2, w_in_bf, w_kvt)
    mix_l, h_last, tail = _lru_prompt(u, gl, conv_w, conv_b2, wg, bg, a_param2)
    mix_a, s_all, idxv, valv = _attn_and_score(q, kt, vt, ga, rel_bias, pt_flat, qt, kc_t, ns, n_pages)
    y_prompt = _outproj_prompt(x_prompt, mod_p, mix_l, mix_a, w_out_bf, g_final2)
    k_prompt = kt.reshape(bp, N_HEADS, hd, s).transpose(0, 3, 1, 2)
    v_prompt = vt.reshape(bp, N_HEADS, hd, s).transpose(0, 3, 1, 2)
    lru_h_prompt = h_last[:, 0, :]
    lru_conv_prompt = tail[:, SUBLANES - (CONV_W - 1):, :]

    idx_flat = idxv[:, :, :, 0].reshape(-1)
    nb = n_pages // 2
    pos = (nb - 1) * MOBA_BLOCK + jnp.arange(MOBA_BLOCK, dtype=I32)
    bkt_last = _t5_bucket(past - pos).reshape(1, MOBA_BLOCK)
    y_att = _decode(pt_flat, idx_flat, s_all, idxv, valv, qt, ktn, v_s.reshape(ns, N_HEADS, hd),
                    rel_bias.T, bkt_last, vc_t, ns, n_pages)
    y_s = _outproj_sample(xs, mod_s, mix_ls, y_att.reshape(ns, w), ga_s, w_out_bf, g_final2)

    return (y_prompt, y_s.reshape(ns, 1, d), k_prompt, v_prompt, lru_h_prompt, lru_conv_prompt,
            k_s.reshape(ns, 1, N_HEADS, hd), v_s.reshape(ns, 1, N_HEADS, hd), h_s,
            cnew.transpose(1, 0, 2))
```

```python
import functools
import math

import jax
import jax.numpy as jnp
from jax import lax
from jax.experimental import pallas as pl
from jax.experimental.pallas import tpu as pltpu

F32 = jnp.float32
BF16 = jnp.bfloat16
I32 = jnp.int32

N_HEADS = 8
LRU_BLOCKS = 8
CONV_W = 4
LRU_C = 8.0
MOBA_BLOCK = 256
MOBA_TOPK = 3
REL_BUCKETS = 32
REL_MAX_DIST = 128
RMS_EPS = 1e-6
NEG_INF = -1e30
HIGHEST = lax.Precision.HIGHEST
LOG2E = math.log2(math.e)

LANES = 128
SUBLANES = 8
VMEM_LIMIT_BYTES = 56 * 1024 * 1024

ROW_TILE = 1024
OUT_TILE = 1024
LRU_CHUNK = 512
PAGES_PER_GROUP = 64
K_SLOTS = 2
SCORE_PAGES = 16
MASK_ROWS = 16
DEC_SEQS = 4


def _sigmoid(x):
    return 0.5 * jnp.tanh(0.5 * x) + 0.5


def _silu(x):
    return x * _sigmoid(x)


def _softplus(z):
    return jnp.maximum(z, 0.0) + jnp.log1p(jnp.exp(-jnp.abs(z)))


def _t5_bucket(dist):
    n = jnp.maximum(dist, 0)
    max_exact = REL_BUCKETS // 2
    nf = jnp.maximum(n, 1).astype(F32)
    pos = jnp.log(nf / max_exact) / math.log(REL_MAX_DIST / max_exact) * (REL_BUCKETS - max_exact)
    large = max_exact + jnp.where(pos >= 0, jnp.floor(pos), jnp.ceil(pos)).astype(I32)
    large = jnp.minimum(large, REL_BUCKETS - 1)
    return jnp.where(n < max_exact, n, large)


def _params(*sem):
    return pltpu.CompilerParams(dimension_semantics=sem, vmem_limit_bytes=VMEM_LIMIT_BYTES)


def _adaln_kernel(c_ref, w_ref, b_ref, o_ref):
    s = _silu(c_ref[...]).astype(BF16)
    o_ref[...] = jnp.dot(s, w_ref[...].astype(BF16), preferred_element_type=F32) + b_ref[...]


def _adaln(c, w_ada, b_ada):
    n, d = c.shape
    return pl.pallas_call(
        _adaln_kernel,
        out_shape=jax.ShapeDtypeStruct((n, 3 * d), F32),
        grid=(3,),
        in_specs=[pl.BlockSpec((n, d), lambda j: (0, 0)),
                  pl.BlockSpec((d, d), lambda j: (0, j)),
                  pl.BlockSpec((1, d), lambda j: (0, j))],
        out_specs=pl.BlockSpec((n, d), lambda j: (0, j)),
        compiler_params=_params("arbitrary"),
        name="adaln",
    )(c, w_ada, b_ada.reshape(1, 3 * d))


def _wprep_kernel(w_ref, wb_ref, wt_ref):
    x = w_ref[...]
    wb_ref[...] = x.astype(BF16)
    wt_ref[...] = x.T.astype(BF16)


def _wprep(w_in, cols):
    d, n = w_in.shape
    return pl.pallas_call(
        _wprep_kernel,
        out_shape=[jax.ShapeDtypeStruct((d, n), BF16), jax.ShapeDtypeStruct((n, d), BF16)],
        grid=(n // cols,),
        in_specs=[pl.BlockSpec((d, cols), lambda j: (0, j))],
        out_specs=[pl.BlockSpec((d, cols), lambda j: (0, j)), pl.BlockSpec((cols, d), lambda j: (j, 0))],
        compiler_params=_params("arbitrary"),
        name="wprep",
    )(w_in)


def _modulated_norm(x, mod, g, d):
    ms = jnp.mean(x * x, axis=-1, keepdims=True)
    xn = x * lax.rsqrt(ms + RMS_EPS) * g
    return xn * (1.0 + mod[:, d:2 * d]) + mod[:, 0:d]


def _inproj_kernel(x_ref, mod_ref, g_ref, win_ref, wkt_ref, wvt_ref,
                   u_ref, gl_ref, q_ref, ga_ref, kt_ref, vt_ref, *, d, w):
    xm = _modulated_norm(x_ref[...], mod_ref[...], g_ref[...], d).astype(BF16)
    pr = jnp.dot(xm, win_ref[:, 0:3 * w], preferred_element_type=F32)
    u_ref[...] = pr[:, 0:w]
    gl_ref[...] = pr[:, w:2 * w]
    q_ref[...] = pr[:, 2 * w:3 * w]
    ga_ref[...] = jnp.dot(xm, win_ref[:, 5 * w:6 * w], preferred_element_type=F32)
    nt = (((1,), (1,)), ((), ()))
    kt_ref[...] = lax.dot_general(wkt_ref[...], xm, nt, preferred_element_type=F32)
    vt_ref[...] = lax.dot_general(wvt_ref[...], xm, nt, preferred_element_type=F32)


def _inproj(x, mod3, g_norm, w_in_bf, w_in_t):
    b, s, d = x.shape
    w = w_in_bf.shape[1] // 6
    tm = ROW_TILE
    row = pl.BlockSpec((None, tm, w), lambda i, t: (i, t, 0))
    col = pl.BlockSpec((None, w, tm), lambda i, t: (i, 0, t))
    return pl.pallas_call(
        functools.partial(_inproj_kernel, d=d, w=w),
        out_shape=[jax.ShapeDtypeStruct((b, s, w), F32)] * 4
        + [jax.ShapeDtypeStruct((b, w, s), F32)] * 2,
        grid=(b, s // tm),
        in_specs=[pl.BlockSpec((None, tm, d), lambda i, t: (i, t, 0)),
                  pl.BlockSpec((None, 1, 3 * d), lambda i, t: (i, 0, 0)),
                  pl.BlockSpec((1, d), lambda i, t: (0, 0)),
                  pl.BlockSpec((d, 6 * w), lambda i, t: (0, 0)),
                  pl.BlockSpec((w, d), lambda i, t: (3, 0)),
                  pl.BlockSpec((w, d), lambda i, t: (4, 0))],
        out_specs=[row, row, row, row, col, col],
        compiler_params=_params("arbitrary", "arbitrary"),
        name="inproj",
    )(x, mod3, g_norm, w_in_bf, w_in_t, w_in_t)


def _lru_gates(u_conv, wg_ref, bg_ref, ap_ref, w):
    g2 = jnp.dot(u_conv.astype(BF16), wg_ref[...], preferred_element_type=F32) + bg_ref[...]
    r = _sigmoid(g2[:, 0:w])
    i = _sigmoid(g2[:, w:2 * w])
    log_a = (-LRU_C * r) * _softplus(-ap_ref[...])
    a = jnp.exp(log_a)
    z = -jnp.tanh(log_a) * (a * a + 1.0)
    root = jnp.where(z > 0.0, z * lax.rsqrt(z), 0.0)
    bx = root * (i * u_conv)
    return a, bx


def _lru_kernel(u_ref, gl_ref, cw_ref, cb_ref, wg_ref, bg_ref, ap_ref,
                mix_ref, hl_ref, tail_ref, a_scr, b_scr, hs_scr, h_scr, tail_scr, *, tc, w):
    c = pl.program_id(1)

    @pl.when(c == 0)
    def _():
        h_scr[...] = jnp.zeros_like(h_scr)
        tail_scr[...] = jnp.zeros_like(tail_scr)

    u = u_ref[...]
    cw = cw_ref[...]
    tail = tail_scr[...]
    u1, u2, u3 = (tail[SUBLANES - k:SUBLANES - k + 1] for k in (1, 2, 3))

    def shift1(z, first):
        return jnp.concatenate([first, z[0:tc - 1]], axis=0)

    z = shift1(u * cw[0:1], u1 * cw[0:1])
    z = shift1(u * cw[1:2] + z, u1 * cw[1:2] + u2 * cw[0:1])
    z = shift1(u * cw[2:3] + z, u1 * cw[2:3] + u2 * cw[1:2] + u3 * cw[0:1])
    u_conv = cb_ref[...] + (u * cw[3:4] + z)
    tail_scr[...] = u[tc - SUBLANES:tc]
    a, bx = _lru_gates(u_conv, wg_ref, bg_ref, ap_ref, w)
    a_scr[...] = a
    b_scr[...] = bx

    row = lax.broadcasted_iota(I32, (SUBLANES, w), 0)

    def tile(t, h):
        r0 = pl.multiple_of(t * SUBLANES, SUBLANES)
        at = a_scr[pl.ds(r0, SUBLANES), :]
        bt = b_scr[pl.ds(r0, SUBLANES), :]
        for dd in (1, 2, 4):
            keep = row >= dd
            bt = jnp.where(keep, at * pltpu.roll(bt, dd, 0) + bt, bt)
            at = jnp.where(keep, at * pltpu.roll(at, dd, 0), at)
        hs = at * h + bt
        hs_scr[pl.ds(r0, SUBLANES), :] = hs
        return jnp.broadcast_to(hs[SUBLANES - 1:SUBLANES, :], (SUBLANES, w))

    h = lax.fori_loop(0, tc // SUBLANES, tile, h_scr[...], unroll=4)
    h_scr[...] = h
    hl_ref[...] = h
    tail_ref[...] = u[tc - SUBLANES:tc]
    mix_ref[...] = (hs_scr[...] * _silu(gl_ref[...])).astype(mix_ref.dtype)


def _lru_prompt(u, gl, conv_w, conv_b, wg, bg, a_param):
    b, s, w = u.shape
    tc = LRU_CHUNK
    const = lambda shape: pl.BlockSpec(shape, lambda i, t: (0,) * len(shape))
    return pl.pallas_call(
        functools.partial(_lru_kernel, tc=tc, w=w),
        out_shape=[jax.ShapeDtypeStruct((b, s, w), BF16),
                   jax.ShapeDtypeStruct((b, SUBLANES, w), F32),
                   jax.ShapeDtypeStruct((b, SUBLANES, w), F32)],
        grid=(b, s // tc),
        in_specs=[pl.BlockSpec((None, tc, w), lambda i, t: (i, t, 0)),
                  pl.BlockSpec((None, tc, w), lambda i, t: (i, t, 0)),
                  const((CONV_W, w)), const((1, w)), const((w, 2 * w)), const((1, 2 * w)),
                  const((1, w))],
        out_specs=[pl.BlockSpec((None, tc, w), lambda i, t: (i, t, 0)),
                   pl.BlockSpec((None, SUBLANES, w), lambda i, t: (i, 0, 0)),
                   pl.BlockSpec((None, SUBLANES, w), lambda i, t: (i, 0, 0))],
        scratch_shapes=[pltpu.VMEM((tc, w), F32), pltpu.VMEM((tc, w), F32),
                        pltpu.VMEM((tc, w), F32), pltpu.VMEM((SUBLANES, w), F32),
                        pltpu.VMEM((SUBLANES, w), F32)],
        compiler_params=_params("arbitrary", "arbitrary"),
        name="lru_prompt",
    )(u, gl, conv_w, conv_b, wg, bg, a_param)


_NT = (((1,), (1,)), ((), ()))


def _attn_score_kernel(pt_ref, rb_ref, q_ref, kt_ref, vt_ref, ga_ref, bkt_ref, qt_ref, kc_ref,
                       o_ref, s_ref, idx_ref, val_ref,
                       bias_scr, kaug_scr, vb_scr, mpad_scr, qm_scr, lg_scr, kbuf, sem, qb_scr, g_scr,
                       *, hd, nb, n_pages, n_seq):
    p = pl.program_id(0)
    b = pl.program_id(1)
    blk = MOBA_BLOCK
    s = nb * blk
    pg = PAGES_PER_GROUP
    gps = n_pages // pg
    total = n_seq * gps
    seq = p * pl.num_programs(1) + b

    def page_copy(g, jj, slot):
        return pltpu.make_async_copy(kc_ref.at[pt_ref[g * pg + jj]], kbuf.at[slot, jj], sem.at[slot])

    def start_group(g):
        slot = lax.rem(g, K_SLOTS)
        for jj in range(pg):
            page_copy(g, jj, slot).start()

    def wait_group(g):
        slot = lax.rem(g, K_SLOTS)
        for jj in range(pg):
            page_copy(g, jj, slot).wait()

    @pl.when(seq == 0)
    def _():
        for g in range(K_SLOTS):
            start_group(g)

    lane_t = lax.broadcasted_iota(I32, qt_ref.shape, 1)
    qcol = jnp.sum(jnp.where(lane_t == seq, qt_ref[...], 0.0), axis=1, keepdims=True)
    qb_scr[...] = jnp.broadcast_to(qcol, qb_scr.shape)

    @pl.when(b == 0)
    def _():
        qi = lax.broadcasted_iota(I32, (blk, blk), 0)
        ki = lax.broadcasted_iota(I32, (blk, blk), 1)
        for hh in range(2):
            h = 2 * p + hh
            far = rb_ref[REL_BUCKETS - 1, h]
            for which in range(2):
                bk = bkt_ref[which]
                tab = jnp.zeros((blk, blk), F32)
                for k in range(REL_BUCKETS):
                    tab = jnp.where(bk == k, (rb_ref[k, h] - far) * LOG2E, tab)
                if which == 0:
                    tab = jnp.where(qi >= ki, tab, NEG_INF)
                bias_scr[hh, which] = tab
        i16 = lax.broadcasted_iota(I32, (MASK_ROWS, s), 0)
        ind16 = jnp.where(i16 == lax.broadcasted_iota(I32, (MASK_ROWS, s), 1) // blk, 1.0, 0.0).astype(BF16)
        for hh in range(2):
            o0 = hd * (1 - hh)
            kaug_scr[hh, o0:o0 + MASK_ROWS, :] = ind16
            kaug_scr[hh, o0 + MASK_ROWS:o0 + hd, :] = jnp.zeros((hd - MASK_ROWS, s), BF16)
            vb_scr[hh, o0:o0 + hd, :] = jnp.ones((hd, s), BF16)

    def prepare():
        n_i = lax.broadcasted_iota(I32, (SUBLANES, s), 0)
        j_i = lax.broadcasted_iota(I32, (SUBLANES, s), 1) // blk
        ind = jnp.where(n_i == j_i, 1.0, 0.0).astype(BF16)
        ktb = kt_ref[...].astype(BF16)
        km8 = lax.dot_general(ind, ktb, _NT, preferred_element_type=F32) * (1.0 / blk)
        vtb = vt_ref[...].astype(BF16)
        for hh in range(2):
            own = slice(hd * hh, hd * (hh + 1))
            kaug_scr[hh, own, :] = ktb[own]
            vb_scr[hh, own, :] = vtb[own]
        qsb = (q_ref[...] * (hd ** -0.5 * LOG2E)).astype(BF16)
        lane_q = lax.broadcasted_iota(I32, (1, LANES), 1)

        for hh in range(2):
            head_lanes = jnp.where((lane_q >= hd * hh) & (lane_q < hd * (hh + 1)), 1.0, 0.0).astype(BF16)
            qm = qsb * head_lanes
            qm_scr[hh] = qm
            c0 = min(MOBA_TOPK + 1, nb) * blk
            for q0, q1, ranked in ((0, c0, False), (c0, s, True)):
                if q1 == q0:
                    continue
                gt = lax.dot_general(km8.astype(BF16), qm[q0:q1], _NT, preferred_element_type=F32)
                n_p = lax.broadcasted_iota(I32, gt.shape, 0)
                j_p = (lax.broadcasted_iota(I32, gt.shape, 1) + q0) // blk
                past = n_p < j_p
                g = jnp.where(past, gt, NEG_INF)
                keep = past & (g > 0.5 * NEG_INF * (hd ** -0.5 * LOG2E))
                if ranked:
                    cnt = jnp.zeros(gt.shape, I32)
                    for m in range(nb - 1):
                        gm = g[m:m + 1, :]
                        cnt = cnt + jnp.where((gm > g) | ((gm == g) & (n_p > m)), 1, 0)
                    keep = keep & (cnt < MOBA_TOPK)
                mpad_scr[hh, :, q0:q1] = jnp.where(keep | (n_p >= j_p), 0.0, NEG_INF)

    lane = lax.broadcasted_iota(I32, (blk, LANES), 1)

    def attend(j, hh, par):
        rows = slice(j * blk, (j + 1) * blk)
        qa = qm_scr[hh, rows, :]
        if j > 0:
            o0 = hd * (1 - hh)
            pieces = [mpad_scr[hh, :, rows], jnp.zeros((LANES - o0 - SUBLANES, blk), F32)]
            if o0:
                pieces.insert(0, jnp.zeros((o0, blk), F32))
            slab = jnp.concatenate(pieces, axis=0)
            qa = qa + slab.T.astype(BF16)
        macc = None
        for n in range(j + 1):
            sc = jnp.dot(qa, kaug_scr[hh, :, n * blk:(n + 1) * blk],
                         preferred_element_type=F32)
            if n == j:
                sc = sc + bias_scr[hh, 0]
            elif n == j - 1:
                sc = sc + bias_scr[hh, 1]
            lg_scr[par, n] = sc
            mx = jnp.maximum(sc[:, 0:LANES], sc[:, LANES:2 * LANES])
            macc = mx if macc is None else jnp.maximum(macc, mx)
        mrow = jnp.max(macc, axis=1, keepdims=True)
        acc = None
        for n in range(j + 1):
            pn = jnp.exp2(lg_scr[par, n] - mrow).astype(BF16)
            pv = lax.dot_general(pn, vb_scr[hh, :, n * blk:(n + 1) * blk], _NT,
                                 preferred_element_type=F32)
            acc = pv if acc is None else acc + pv
        return acc

    def finish(j, accs):
        rows = slice(j * blk, (j + 1) * blk)
        first = lane < hd
        num = jnp.where(first, accs[0], accs[1])
        den = pltpu.roll(jnp.where(first, accs[1], accs[0]), hd, 1)
        o_ref[rows, :] = (num / den * _silu(ga_ref[rows, :])).astype(o_ref.dtype)

    sub8 = lax.broadcasted_iota(I32, (SUBLANES, LANES), 0)
    tiles = hd // SUBLANES

    def score_group(sg, slot):
        def trip(t, carry):
            res = [jnp.zeros((N_HEADS, LANES), F32) for _ in range(SCORE_PAGES)]
            for h in range(N_HEADS):
                qh = [qb_scr[(h * tiles + i) * SUBLANES:(h * tiles + i + 1) * SUBLANES, :]
                      for i in range(tiles)]
                for pp in range(SCORE_PAGES):
                    page = kbuf.at[slot, t * SCORE_PAGES + pp]
                    acc = None
                    for i in range(tiles):
                        r0 = (h * tiles + i) * SUBLANES
                        term = page[r0:r0 + SUBLANES, :] * qh[i]
                        acc = term if acc is None else acc + term
                    for sh in (4, 2, 1):
                        acc = acc + pltpu.roll(acc, sh, 0)
                    res[pp] = jnp.where(sub8 == h, acc, res[pp])
            for bb in range(SCORE_PAGES // 2):
                n = sg * (pg // 2) + t * (SCORE_PAGES // 2) + bb
                s_ref[n, :, 0:LANES] = res[2 * bb]
                s_ref[n, :, LANES:2 * LANES] = res[2 * bb + 1]
                bsum = jnp.sum(res[2 * bb] + res[2 * bb + 1], axis=1, keepdims=True)
                g_scr[n] = jnp.broadcast_to(bsum, (N_HEADS, LANES))
            return carry

        lax.fori_loop(0, pg // SCORE_PAGES, trip, 0)

    items = [(0, hh) for hh in range(2)] + [(j, hh) for j in reversed(range(1, nb)) for hh in range(2)]
    work = sum(j + 1 for j, _ in items)
    segments = [[] for _ in range(gps)]
    done = 0
    for item in items:
        segments[min(gps - 1, done * gps // work)].append(item)
        done += item[0] + 1

    accs = {}
    count = 0
    for sg in range(gps):
        g = seq * gps + sg
        wait_group(g)
        score_group(sg, lax.rem(g, K_SLOTS))

        start_group(lax.rem(g + K_SLOTS, total))

        if sg == 0:
            prepare()
        for j, hh in segments[sg]:
            accs.setdefault(j, []).append(attend(j, hh, count % 2))
            count += 1
            if len(accs[j]) == 2:
                finish(j, accs.pop(j))

    @pl.when(seq == n_seq - 1)
    def _():
        for g in range(K_SLOTS):
            wait_group(g)

    gate = g_scr[...] * (1.0 / MOBA_BLOCK)
    nidx = lax.broadcasted_iota(I32, gate.shape, 0)
    for r in range(MOBA_TOPK):
        mx = jnp.max(gate, axis=0)
        am = jnp.min(jnp.where(gate == mx[None], nidx, gate.shape[0]), axis=0)
        idx_ref[r] = am
        val_ref[r] = mx
        gate = jnp.where(nidx == am[None], -jnp.inf, gate)


def _attn_and_score(q, kt, vt, ga, rel_bias, pt_flat, q_dec, kcache_t, n_seq, n_pages):
    b, s, w = q.shape
    hd = w // N_HEADS
    blk = MOBA_BLOCK
    nb = s // blk
    npair = w // LANES
    nblk = n_pages // 2
    assert nb <= SUBLANES and LANES == 2 * hd
    assert n_seq == npair * b and n_pages % PAGES_PER_GROUP == 0
    assert (n_seq * (n_pages // PAGES_PER_GROUP)) % K_SLOTS == 0
    dq = jnp.arange(blk, dtype=I32)
    d_own = dq[:, None] - dq[None, :]
    bkt = jnp.stack([_t5_bucket(d_own), _t5_bucket(d_own + blk)])
    per_seq = lambda p, i, pt: (p * b + i, 0, 0, 0)
    grid_spec = pltpu.PrefetchScalarGridSpec(
        num_scalar_prefetch=1,
        grid=(npair, b),
        in_specs=[pl.BlockSpec(memory_space=pltpu.SMEM),
                  pl.BlockSpec((None, s, LANES), lambda p, i, pt: (i, 0, p)),
                  pl.BlockSpec((None, LANES, s), lambda p, i, pt: (i, p, 0)),
                  pl.BlockSpec((None, LANES, s), lambda p, i, pt: (i, p, 0)),
                  pl.BlockSpec((None, s, LANES), lambda p, i, pt: (i, 0, p)),
                  pl.BlockSpec((2, blk, blk), lambda p, i, pt: (0, 0, 0)),
                  pl.BlockSpec((w, LANES), lambda p, i, pt: (0, 0)),
                  pl.BlockSpec(memory_space=pl.ANY)],
        out_specs=[pl.BlockSpec((None, s, LANES), lambda p, i, pt: (i, 0, p)),
                   pl.BlockSpec((None, nblk, N_HEADS, MOBA_BLOCK), per_seq),
                   pl.BlockSpec((None, MOBA_TOPK, N_HEADS, LANES), per_seq),
                   pl.BlockSpec((None, MOBA_TOPK, N_HEADS, LANES), per_seq)],
        scratch_shapes=[pltpu.VMEM((2, 2, blk, blk), F32),
                        pltpu.VMEM((2, LANES, s), BF16),
                        pltpu.VMEM((2, LANES, s), BF16),
                        pltpu.VMEM((2, SUBLANES, s), F32),
                        pltpu.VMEM((2, s, LANES), BF16),
                        pltpu.VMEM((2, nb, blk, blk), F32),
                        pltpu.VMEM((K_SLOTS, PAGES_PER_GROUP, w, LANES), F32),
                        pltpu.SemaphoreType.DMA((K_SLOTS,)),
                        pltpu.VMEM((w, LANES), F32),
                        pltpu.VMEM((nblk, N_HEADS, LANES), F32)])
    return pl.pallas_call(
        functools.partial(_attn_score_kernel, hd=hd, nb=nb, n_pages=n_pages, n_seq=n_seq),
        out_shape=[jax.ShapeDtypeStruct((b, s, w), BF16),
                   jax.ShapeDtypeStruct((n_seq, nblk, N_HEADS, MOBA_BLOCK), F32),
                   jax.ShapeDtypeStruct((n_seq, MOBA_TOPK, N_HEADS, LANES), I32),
                   jax.ShapeDtypeStruct((n_seq, MOBA_TOPK, N_HEADS, LANES), F32)],
        grid_spec=grid_spec,
        compiler_params=_params("arbitrary", "arbitrary"),
        name="attn_score",
    )(pt_flat, rel_bias, q, kt, vt, ga, bkt, q_dec, kcache_t)


def _outproj_kernel(x_ref, mod_ref, ml_ref, ma_ref, wo_ref, gf_ref, y_ref, *, d, w):
    acc = jnp.dot(ml_ref[...].astype(BF16), wo_ref[0:w, :], preferred_element_type=F32)
    acc = acc + jnp.dot(ma_ref[...].astype(BF16), wo_ref[w:2 * w, :], preferred_element_type=F32)
    out = x_ref[...] + mod_ref[:, 2 * d:3 * d] * acc
    ms = jnp.mean(out * out, axis=-1, keepdims=True)
    y_ref[...] = out * lax.rsqrt(ms + RMS_EPS) * gf_ref[...]


def _outproj_prompt(x, mod3, mix_l, mix_a, w_out, g_final):
    b, s, d = x.shape
    w = mix_l.shape[-1]
    tm = OUT_TILE
    return pl.pallas_call(
        functools.partial(_outproj_kernel, d=d, w=w),
        out_shape=jax.ShapeDtypeStruct((b, s, d), F32),
        grid=(b, s // tm),
        in_specs=[pl.BlockSpec((None, tm, d), lambda i, t: (i, t, 0)),
                  pl.BlockSpec((None, 1, 3 * d), lambda i, t: (i, 0, 0)),
                  pl.BlockSpec((None, tm, w), lambda i, t: (i, t, 0)),
                  pl.BlockSpec((None, tm, w), lambda i, t: (i, t, 0)),
                  pl.BlockSpec((2 * w, d), lambda i, t: (0, 0)),
                  pl.BlockSpec((1, d), lambda i, t: (0, 0))],
        out_specs=pl.BlockSpec((None, tm, d), lambda i, t: (i, t, 0)),
        compiler_params=_params("arbitrary", "arbitrary"),
        name="outproj_prompt",
    )(x, mod3, mix_l, mix_a, w_out, g_final)


def _sample_proj_kernel(x_ref, mod_ref, g_ref, win_ref, wqkt_ref, cbuf_ref, h0_ref, cw_ref, cb_ref,
                        wg_ref, bg_ref, ap_ref,
                        qt_ref, kt_ref, k_ref, v_ref, h_ref, cnew_ref, ml_ref, ga_ref, *, d, w):
    n = x_ref.shape[0]
    xm = _modulated_norm(x_ref[...], mod_ref[...], g_ref[...], d).astype(BF16)
    pr = jnp.dot(xm, win_ref[...], preferred_element_type=F32)
    u = pr[:, 0:w]
    k_ref[...] = pr[:, 3 * w:4 * w]
    v_ref[...] = pr[:, 4 * w:5 * w]
    ga_ref[...] = pr[:, 5 * w:6 * w]
    xpad = jnp.concatenate([xm, jnp.zeros((LANES - n, d), BF16)], axis=0)
    pt = lax.dot_general(wqkt_ref[2 * w:4 * w, :], xpad, (((1,), (1,)), ((), ())),
                         preferred_element_type=F32)
    qt_ref[...] = pt[0:w]
    kt_ref[...] = pt[w:2 * w]
    cw = cw_ref[...]
    u_conv = cb_ref[...] + (cbuf_ref[0] * cw[0:1] + cbuf_ref[1] * cw[1:2]
                            + cbuf_ref[2] * cw[2:3] + u * cw[3:4])
    a, bx = _lru_gates(u_conv, wg_ref, bg_ref, ap_ref, w)
    h = a * h0_ref[...] + bx
    h_ref[...] = h
    cnew_ref[0] = cbuf_ref[1]
    cnew_ref[1] = cbuf_ref[2]
    cnew_ref[2] = u
    ml_ref[...] = h * _silu(pr[:, w:2 * w])


def _sample_proj(x, mod, g_norm, w_in_bf, w_in_t, cbuf, h0, conv_w, conv_b, wg, bg, a_param):
    n, d = x.shape
    w = h0.shape[1]
    row = jax.ShapeDtypeStruct((n, w), F32)
    col = jax.ShapeDtypeStruct((w, LANES), F32)
    return pl.pallas_call(
        functools.partial(_sample_proj_kernel, d=d, w=w),
        out_shape=[col, col, row, row, row, jax.ShapeDtypeStruct((CONV_W - 1, n, w), F32), row, row],
        compiler_params=pltpu.CompilerParams(vmem_limit_bytes=VMEM_LIMIT_BYTES),
        name="sample_proj",
    )(x, mod, g_norm, w_in_bf, w_in_t, cbuf, h0, conv_w, conv_b, wg, bg, a_param)


def _decode_kernel(pt_ref, ix_ref, s_ref, idxv_ref, valv_ref, qt_ref, kt_ref, vnew_ref, rbt_ref,
                   bkt_ref, vc_ref, y_ref, vbuf, sem, *, n_pages, n_seq, hd):
    b = pl.program_id(0)
    nsel = MOBA_TOPK
    blk = MOBA_BLOCK
    nb = n_pages // 2
    grp = DEC_SEQS
    n_steps = n_seq // grp

    def tile_copy(step, gi, r, h, pp, slot):
        sq = step * grp + gi
        page = pt_ref[sq * n_pages + 2 * ix_ref[(sq * nsel + r) * N_HEADS + h] + pp]
        return pltpu.make_async_copy(
            vc_ref.at[page, h], vbuf.at[slot, gi, h, :, pl.ds((2 * r + pp) * LANES, LANES)], sem.at[slot])

    def for_tiles(step, fn):
        slot = lax.rem(step, 2)
        for gi in range(grp):
            for r in range(nsel):
                for h in range(N_HEADS):
                    for pp in range(2):
                        fn(tile_copy(step, gi, r, h, pp, slot))

    @pl.when(b == 0)
    def _():
        for_tiles(0, lambda cp: cp.start())

    for_tiles(lax.rem(b + 1, n_steps), lambda cp: cp.start())

    sub = lax.broadcasted_iota(I32, (N_HEADS, blk), 0)
    rbt = rbt_ref[...]
    bias_last = jnp.zeros((N_HEADS, blk), F32)
    for k in range(REL_BUCKETS):
        bias_last = jnp.where(bkt_ref[...] == k, rbt[:, k:k + 1], bias_last)
    bias_far = rbt[:, REL_BUCKETS - 1:REL_BUCKETS]
    lane = lax.broadcasted_iota(I32, (N_HEADS, LANES), 1)
    qk = jnp.sum((qt_ref[...] * kt_ref[...]).reshape(N_HEADS, hd, LANES), axis=1)

    softmax = []
    for gi in range(grp):
        sq = b * grp + gi
        logits = []
        for r in range(nsel):
            lr = jnp.zeros((N_HEADS, blk), F32)
            for h in range(N_HEADS):
                n = ix_ref[(sq * nsel + r) * N_HEADS + h]
                lr = jnp.where(sub == h, s_ref[gi, n], lr)
            idv = idxv_ref[gi, r][:, 0:1]
            bias = jnp.where(idv == nb - 1, bias_last, bias_far)
            valid = valv_ref[gi, r][:, 0:1] > 0.5 * NEG_INF
            logits.append(jnp.where(valid, lr * (hd ** -0.5) + bias, NEG_INF))

        own = jnp.sum(jnp.where(lane == sq, qk, 0.0), axis=1, keepdims=True)
        l_own = own * (hd ** -0.5) + rbt[:, 0:1]
        m = l_own
        for lr in logits:
            m = jnp.maximum(m, jnp.max(lr, axis=1, keepdims=True))
        p_own = jnp.exp(l_own - m)
        den = p_own
        probs = []
        for lr in logits:
            pr = jnp.exp(lr - m)
            den = den + jnp.sum(pr, axis=1, keepdims=True)
            probs.append(pr)
        softmax.append((jnp.concatenate(probs, axis=1), p_own, den))

    for_tiles(b, lambda cp: cp.wait())
    slot = lax.rem(b, 2)
    for gi in range(grp):
        pcat, p_own, den = softmax[gi]
        rows = []
        for h in range(N_HEADS):
            vt = vbuf[slot, gi, h].astype(BF16)
            rows.append(lax.dot_general(pcat[h:h + 1, :].astype(BF16), vt, (((1,), (1,)), ((), ())),
                                        preferred_element_type=F32))
        acc = jnp.concatenate(rows, axis=0)
        y_ref[gi] = (acc + p_own * vnew_ref[gi]) / den

    @pl.when(b == n_steps - 1)
    def _():
        for_tiles(0, lambda cp: cp.wait())


def _decode(pt_flat, idx_flat, s_all, idxv, valv, qt, kt, v_new, rbt, bkt_last, vcache_t, n_seq, n_pages):
    w = qt.shape[0]
    hd = w // N_HEADS
    nb = n_pages // 2
    grp = DEC_SEQS
    assert n_seq % (2 * grp) == 0
    zero4 = lambda b, pt, ix: (b, 0, 0, 0)
    grid_spec = pltpu.PrefetchScalarGridSpec(
        num_scalar_prefetch=2,
        grid=(n_seq // grp,),
        in_specs=[pl.BlockSpec((grp, nb, N_HEADS, MOBA_BLOCK), zero4),
                  pl.BlockSpec((grp, MOBA_TOPK, N_HEADS, LANES), zero4),
                  pl.BlockSpec((grp, MOBA_TOPK, N_HEADS, LANES), zero4),
                  pl.BlockSpec((w, LANES), lambda b, pt, ix: (0, 0)),
                  pl.BlockSpec((w, LANES), lambda b, pt, ix: (0, 0)),
                  pl.BlockSpec((grp, N_HEADS, hd), lambda b, pt, ix: (b, 0, 0)),
                  pl.BlockSpec((N_HEADS, REL_BUCKETS), lambda b, pt, ix: (0, 0)),
                  pl.BlockSpec((1, MOBA_BLOCK), lambda b, pt, ix: (0, 0)),
                  pl.BlockSpec(memory_space=pl.ANY)],
        out_specs=pl.BlockSpec((grp, N_HEADS, hd), lambda b, pt, ix: (b, 0, 0)),
        scratch_shapes=[pltpu.VMEM((2, grp, N_HEADS, hd, MOBA_TOPK * MOBA_BLOCK), F32),
                        pltpu.SemaphoreType.DMA((2,))])
    return pl.pallas_call(
        functools.partial(_decode_kernel, n_pages=n_pages, n_seq=n_seq, hd=hd),
        out_shape=jax.ShapeDtypeStruct((n_seq, N_HEADS, hd), F32),
        grid_spec=grid_spec,
        compiler_params=_params("arbitrary"),
        name="decode_attn",
    )(pt_flat, idx_flat, s_all, idxv, valv, qt, kt, v_new, rbt, bkt_last, vcache_t)


def _outproj_sample_kernel(x_ref, mod_ref, ml_ref, ya_ref, ga_ref, wo_ref, gf_ref, y_ref, *, d, w):
    ma = ya_ref[...] * _silu(ga_ref[...])
    acc = jnp.dot(ml_ref[...].astype(BF16), wo_ref[0:w, :], preferred_element_type=F32)
    acc = acc + jnp.dot(ma.astype(BF16), wo_ref[w:2 * w, :], preferred_element_type=F32)
    out = x_ref[...] + mod_ref[:, 2 * d:3 * d] * acc
    ms = jnp.mean(out * out, axis=-1, keepdims=True)
    y_ref[...] = out * lax.rsqrt(ms + RMS_EPS) * gf_ref[...]


def _outproj_sample(x, mod, mix_l, y_att, ga, w_out, g_final):
    n, d = x.shape
    w = mix_l.shape[1]
    return pl.pallas_call(
        functools.partial(_outproj_sample_kernel, d=d, w=w),
        out_shape=jax.ShapeDtypeStruct((n, d), F32),
        compiler_params=pltpu.CompilerParams(vmem_limit_bytes=VMEM_LIMIT_BYTES),
        name="outproj_sample",
    )(x, mod, mix_l, y_att, ga, w_out, g_final)


def _block_diag(wb):
    n, c, dd = wb.shape
    return jnp.einsum("ncd,nm->ncmd", wb, jnp.eye(n, dtype=wb.dtype)).reshape(n * c, n * dd)


def kernel(x_prompt, x_sample, cache_k, cache_v, state_lru_h, state_lru_conv, page_table, c_prompt, c_sample, w_ada, b_ada, g_norm, w_in, conv_w, conv_b, w_rgate, b_rgate, w_igate, b_igate, lru_a_param, rel_bias, w_out, g_final):
    bp, s, d = x_prompt.shape
    ns = x_sample.shape[0]
    w = state_lru_h.shape[1]
    hd = w // N_HEADS
    n_pages = page_table.shape[1]
    page = cache_k.shape[1]
    past = n_pages * page
    assert x_sample.shape[1] == 1 and 2 * page == MOBA_BLOCK and past % MOBA_BLOCK == 0
    assert s % MOBA_BLOCK == 0 and s % ROW_TILE == 0 and s % OUT_TILE == 0 and s % LRU_CHUNK == 0
    assert w % LANES == 0 and ns <= LANES
    assert MOBA_BLOCK + 1 >= REL_MAX_DIST

    w_in_bf, w_in_t = _wprep(w_in, w)
    w_out_bf = w_out.astype(BF16)
    wg = jnp.concatenate([_block_diag(w_rgate), _block_diag(w_igate)], axis=1).astype(BF16)
    bg = jnp.concatenate([b_rgate, b_igate]).reshape(1, 2 * w)
    g_norm2 = g_norm.reshape(1, d)
    g_final2 = g_final.reshape(1, d)
    conv_b2 = conv_b.reshape(1, w)
    a_param2 = lru_a_param.reshape(1, w)

    mod = _adaln(jnp.concatenate([c_prompt, c_sample], axis=0), w_ada, b_ada)
    mod_p = mod[0:bp].reshape(bp, 1, 3 * d)
    mod_s = mod[bp:bp + ns]

    xs = x_sample.reshape(ns, d)
    cbuf = state_lru_conv.transpose(1, 0, 2)
    qt, ktn, k_s, v_s, h_s, cnew, mix_ls, ga_s = _sample_proj(
        xs, mod_s, g_norm2, w_in_bf, w_in_t, cbuf, state_lru_h, conv_w, conv_b2, wg, bg, a_param2)
    pt_flat = page_table.reshape(-1)
    n_phys = cache_k.shape[0]
    kc_t = cache_k.transpose(0, 2, 3, 1).reshape(n_phys, w, page)
    vc_t = cache_v.transpose(0, 2, 3, 1)

    u, gl, q, ga, kt, vt = _inproj(x_prompt, mod_p, g_norm2, w_in_bf, w_in_t)
    mix_l, h_last, tail = _lru_prompt(u, gl, conv_w, conv_b2, wg, bg, a_param2)
    mix_a, s_all, idxv, valv = _attn_and_score(q, kt, vt, ga, rel_bias, pt_flat, qt, kc_t, ns, n_pages)
    y_prompt = _outproj_prompt(x_prompt, mod_p, mix_l, mix_a, w_out_bf, g_final2)
    k_prompt = kt.reshape(bp, N_HEADS, hd, s).transpose(0, 3, 1, 2)
    v_prompt = vt.reshape(bp, N_HEADS, hd, s).transpose(0, 3, 1, 2)
    lru_h_prompt = h_last[:, 0, :]
    lru_conv_prompt = tail[:, SUBLANES - (CONV_W - 1):, :]

    idx_flat = idxv[:, :, :, 0].reshape(-1)
    nb = n_pages // 2
    pos = (nb - 1) * MOBA_BLOCK + jnp.arange(MOBA_BLOCK, dtype=I32)
    bkt_last = _t5_bucket(past - pos).reshape(1, MOBA_BLOCK)
    y_att = _decode(pt_flat, idx_flat, s_all, idxv, valv, qt, ktn, v_s.reshape(ns, N_HEADS, hd),
                    rel_bias.T, bkt_last, vc_t, ns, n_pages)
    y_s = _outproj_sample(xs, mod_s, mix_ls, y_att.reshape(ns, w), ga_s, w_out_bf, g_final2)

    return (y_prompt, y_s.reshape(ns, 1, d), k_prompt, v_prompt, lru_h_prompt, lru_conv_prompt,
            k_s.reshape(ns, 1, N_HEADS, hd), v_s.reshape(ns, 1, N_HEADS, hd), h_s,
            cnew.transpose(1, 0, 2))
```

```python
import functools
import math

import jax
import jax.numpy as jnp
from jax import lax
from jax.experimental import pallas as pl
from jax.experimental.pallas import tpu as pltpu

F32 = jnp.float32
BF16 = jnp.bfloat16
I32 = jnp.int32

N_HEADS = 8
LRU_BLOCKS = 8
CONV_W = 4
LRU_C = 8.0
MOBA_BLOCK = 256
MOBA_TOPK = 3
REL_BUCKETS = 32
REL_MAX_DIST = 128
RMS_EPS = 1e-6
NEG_INF = -1e30
HIGHEST = lax.Precision.HIGHEST
LOG2E = math.log2(math.e)

LANES = 128
SUBLANES = 8
VMEM_LIMIT_BYTES = 56 * 1024 * 1024

ROW_TILE = 1024
OUT_TILE = 1024
LRU_CHUNK = 512
PAGES_PER_GROUP = 64
K_SLOTS = 2
SCORE_PAGES = 16
SHARE_PAGES = 4
MASK_ROWS = 16
DEC_SEQS = 4


def _sigmoid(x):
    return 0.5 * jnp.tanh(0.5 * x) + 0.5


def _silu(x):
    return x * _sigmoid(x)


def _softplus(z):
    return jnp.maximum(z, 0.0) + jnp.log1p(jnp.exp(-jnp.abs(z)))


def _t5_bucket(dist):
    n = jnp.maximum(dist, 0)
    max_exact = REL_BUCKETS // 2
    nf = jnp.maximum(n, 1).astype(F32)
    pos = jnp.log(nf / max_exact) / math.log(REL_MAX_DIST / max_exact) * (REL_BUCKETS - max_exact)
    large = max_exact + jnp.where(pos >= 0, jnp.floor(pos), jnp.ceil(pos)).astype(I32)
    large = jnp.minimum(large, REL_BUCKETS - 1)
    return jnp.where(n < max_exact, n, large)


def _params(*sem):
    return pltpu.CompilerParams(dimension_semantics=sem, vmem_limit_bytes=VMEM_LIMIT_BYTES)


def _adaln_kernel(c_ref, w_ref, b_ref, o_ref):
    s = _silu(c_ref[...]).astype(BF16)
    o_ref[...] = jnp.dot(s, w_ref[...].astype(BF16), preferred_element_type=F32) + b_ref[...]


def _adaln(c, w_ada, b_ada):
    n, d = c.shape
    return pl.pallas_call(
        _adaln_kernel,
        out_shape=jax.ShapeDtypeStruct((n, 3 * d), F32),
        grid=(3,),
        in_specs=[pl.BlockSpec((n, d), lambda j: (0, 0)),
                  pl.BlockSpec((d, d), lambda j: (0, j)),
                  pl.BlockSpec((1, d), lambda j: (0, j))],
        out_specs=pl.BlockSpec((n, d), lambda j: (0, j)),
        compiler_params=_params("arbitrary"),
        name="adaln",
    )(c, w_ada, b_ada.reshape(1, 3 * d))


def _wprep_kernel(w_ref, wb_ref, wt_ref):
    x = w_ref[...]
    wb_ref[...] = x.astype(BF16)
    wt_ref[...] = x.T.astype(BF16)


def _wprep(w_in, cols):
    d, n = w_in.shape
    return pl.pallas_call(
        _wprep_kernel,
        out_shape=[jax.ShapeDtypeStruct((d, n), BF16), jax.ShapeDtypeStruct((n, d), BF16)],
        grid=(n // cols,),
        in_specs=[pl.BlockSpec((d, cols), lambda j: (0, j))],
        out_specs=[pl.BlockSpec((d, cols), lambda j: (0, j)), pl.BlockSpec((cols, d), lambda j: (j, 0))],
        compiler_params=_params("arbitrary"),
        name="wprep",
    )(w_in)


def _modulated_norm(x, mod, g, d):
    ms = jnp.mean(x * x, axis=-1, keepdims=True)
    xn = x * lax.rsqrt(ms + RMS_EPS) * g
    return xn * (1.0 + mod[:, d:2 * d]) + mod[:, 0:d]


def _inproj_kernel(x_ref, mod_ref, g_ref, win_ref, wkt_ref, wvt_ref,
                   u_ref, gl_ref, q_ref, ga_ref, kt_ref, vt_ref, *, d, w):
    xm = _modulated_norm(x_ref[...], mod_ref[...], g_ref[...], d).astype(BF16)
    pr = jnp.dot(xm, win_ref[:, 0:3 * w], preferred_element_type=F32)
    u_ref[...] = pr[:, 0:w]
    gl_ref[...] = pr[:, w:2 * w]
    q_ref[...] = (pr[:, 2 * w:3 * w] * ((w // N_HEADS) ** -0.5 * LOG2E)).astype(BF16)
    ga_ref[...] = jnp.dot(xm, win_ref[:, 5 * w:6 * w], preferred_element_type=F32)
    nt = (((1,), (1,)), ((), ()))
    kt_ref[...] = lax.dot_general(wkt_ref[...], xm, nt, preferred_element_type=F32)
    vt_ref[...] = lax.dot_general(wvt_ref[...], xm, nt, preferred_element_type=F32)


def _inproj(x, mod3, g_norm, w_in_bf, w_in_t):
    b, s, d = x.shape
    w = w_in_bf.shape[1] // 6
    tm = ROW_TILE
    row = pl.BlockSpec((None, tm, w), lambda i, t: (i, t, 0))
    col = pl.BlockSpec((None, w, tm), lambda i, t: (i, 0, t))
    return pl.pallas_call(
        functools.partial(_inproj_kernel, d=d, w=w),
        out_shape=[jax.ShapeDtypeStruct((b, s, w), F32)] * 2 + [jax.ShapeDtypeStruct((b, s, w), BF16)]
        + [jax.ShapeDtypeStruct((b, s, w), F32)] + [jax.ShapeDtypeStruct((b, w, s), F32)] * 2,
        grid=(b, s // tm),
        in_specs=[pl.BlockSpec((None, tm, d), lambda i, t: (i, t, 0)),
                  pl.BlockSpec((None, 1, 3 * d), lambda i, t: (i, 0, 0)),
                  pl.BlockSpec((1, d), lambda i, t: (0, 0)),
                  pl.BlockSpec((d, 6 * w), lambda i, t: (0, 0)),
                  pl.BlockSpec((w, d), lambda i, t: (3, 0)),
                  pl.BlockSpec((w, d), lambda i, t: (4, 0))],
        out_specs=[row, row, row, row, col, col],
        compiler_params=_params("arbitrary", "arbitrary"),
        name="inproj",
    )(x, mod3, g_norm, w_in_bf, w_in_t, w_in_t)


def _lru_gates(u_conv, wg_ref, bg_ref, ap_ref, w):
    g2 = jnp.dot(u_conv.astype(BF16), wg_ref[...], preferred_element_type=F32) + bg_ref[...]
    r = _sigmoid(g2[:, 0:w])
    i = _sigmoid(g2[:, w:2 * w])
    log_a = (-LRU_C * r) * _softplus(-ap_ref[...])
    a = jnp.exp(log_a)
    z = -jnp.tanh(log_a) * (a * a + 1.0)
    root = jnp.where(z > 0.0, z * lax.rsqrt(z), 0.0)
    bx = root * (i * u_conv)
    return a, bx


def _lru_kernel(u_ref, gl_ref, cw_ref, cb_ref, wg_ref, bg_ref, ap_ref,
                mix_ref, hl_ref, tail_ref, a_scr, b_scr, hs_scr, h_scr, tail_scr, *, tc, w):
    c = pl.program_id(1)

    @pl.when(c == 0)
    def _():
        h_scr[...] = jnp.zeros_like(h_scr)
        tail_scr[...] = jnp.zeros_like(tail_scr)

    u = u_ref[...]
    cw = cw_ref[...]
    tail = tail_scr[...]
    u1, u2, u3 = (tail[SUBLANES - k:SUBLANES - k + 1] for k in (1, 2, 3))

    def shift1(z, first):
        return jnp.concatenate([first, z[0:tc - 1]], axis=0)

    z = shift1(u * cw[0:1], u1 * cw[0:1])
    z = shift1(u * cw[1:2] + z, u1 * cw[1:2] + u2 * cw[0:1])
    z = shift1(u * cw[2:3] + z, u1 * cw[2:3] + u2 * cw[1:2] + u3 * cw[0:1])
    u_conv = cb_ref[...] + (u * cw[3:4] + z)
    tail_scr[...] = u[tc - SUBLANES:tc]
    a, bx = _lru_gates(u_conv, wg_ref, bg_ref, ap_ref, w)
    a_scr[...] = a
    b_scr[...] = bx

    row = lax.broadcasted_iota(I32, (SUBLANES, w), 0)

    def tile(t, h):
        r0 = pl.multiple_of(t * SUBLANES, SUBLANES)
        at = a_scr[pl.ds(r0, SUBLANES), :]
        bt = b_scr[pl.ds(r0, SUBLANES), :]
        for dd in (1, 2, 4):
            keep = row >= dd
            bt = jnp.where(keep, at * pltpu.roll(bt, dd, 0) + bt, bt)
            at = jnp.where(keep, at * pltpu.roll(at, dd, 0), at)
        hs = at * h + bt
        hs_scr[pl.ds(r0, SUBLANES), :] = hs
        return jnp.broadcast_to(hs[SUBLANES - 1:SUBLANES, :], (SUBLANES, w))

    h = lax.fori_loop(0, tc // SUBLANES, tile, h_scr[...], unroll=4)
    h_scr[...] = h
    hl_ref[...] = h
    tail_ref[...] = u[tc - SUBLANES:tc]
    mix_ref[...] = (hs_scr[...] * _silu(gl_ref[...])).astype(mix_ref.dtype)


def _lru_prompt(u, gl, conv_w, conv_b, wg, bg, a_param):
    b, s, w = u.shape
    tc = LRU_CHUNK
    const = lambda shape: pl.BlockSpec(shape, lambda i, t: (0,) * len(shape))
    return pl.pallas_call(
        functools.partial(_lru_kernel, tc=tc, w=w),
        out_shape=[jax.ShapeDtypeStruct((b, s, w), BF16),
                   jax.ShapeDtypeStruct((b, SUBLANES, w), F32),
                   jax.ShapeDtypeStruct((b, SUBLANES, w), F32)],
        grid=(b, s // tc),
        in_specs=[pl.BlockSpec((None, tc, w), lambda i, t: (i, t, 0)),
                  pl.BlockSpec((None, tc, w), lambda i, t: (i, t, 0)),
                  const((CONV_W, w)), const((1, w)), const((w, 2 * w)), const((1, 2 * w)),
                  const((1, w))],
        out_specs=[pl.BlockSpec((None, tc, w), lambda i, t: (i, t, 0)),
                   pl.BlockSpec((None, SUBLANES, w), lambda i, t: (i, 0, 0)),
                   pl.BlockSpec((None, SUBLANES, w), lambda i, t: (i, 0, 0))],
        scratch_shapes=[pltpu.VMEM((tc, w), F32), pltpu.VMEM((tc, w), F32),
                        pltpu.VMEM((tc, w), F32), pltpu.VMEM((SUBLANES, w), F32),
                        pltpu.VMEM((SUBLANES, w), F32)],
        compiler_params=_params("arbitrary", "arbitrary"),
        name="lru_prompt",
    )(u, gl, conv_w, conv_b, wg, bg, a_param)


_NT = (((1,), (1,)), ((), ()))


def _attn_score_kernel(pt_ref, rb_ref, q_ref, kt_ref, vt_ref, ga_ref, bkt_ref, qt_ref, kc_ref,
                       o_ref, s_ref, idx_ref, val_ref,
                       bias_scr, kaug_scr, vb_scr, mpad_scr, qm_scr, lg_scr, kbuf, sem, qb_scr, g_scr,
                       *, hd, nb, n_pages, n_seq):
    p = pl.program_id(0)
    b = pl.program_id(1)
    blk = MOBA_BLOCK
    s = nb * blk
    pg = PAGES_PER_GROUP
    gps = n_pages // pg
    total = n_seq * gps
    seq = p * pl.num_programs(1) + b

    def page_copy(g, jj, slot):
        return pltpu.make_async_copy(kc_ref.at[pt_ref[g * pg + jj]], kbuf.at[slot, jj], sem.at[slot])

    def start_group(g):
        slot = lax.rem(g, K_SLOTS)
        for jj in range(pg):
            page_copy(g, jj, slot).start()

    def wait_group(g):
        slot = lax.rem(g, K_SLOTS)
        for jj in range(pg):
            page_copy(g, jj, slot).wait()

    @pl.when(seq == 0)
    def _():
        for g in range(K_SLOTS):
            start_group(g)

    lane_t = lax.broadcasted_iota(I32, qt_ref.shape, 1)
    qcol = jnp.sum(jnp.where(lane_t == seq, qt_ref[...], 0.0), axis=1, keepdims=True)
    qb_scr[...] = jnp.broadcast_to(qcol, qb_scr.shape)

    @pl.when(b == 0)
    def _():
        qi = lax.broadcasted_iota(I32, (blk, blk), 0)
        ki = lax.broadcasted_iota(I32, (blk, blk), 1)
        for hh in range(2):
            h = 2 * p + hh
            far = rb_ref[REL_BUCKETS - 1, h]
            for which in range(2):
                bk = bkt_ref[which]
                tab = jnp.zeros((blk, blk), F32)
                for k in range(REL_BUCKETS):
                    tab = jnp.where(bk == k, (rb_ref[k, h] - far) * LOG2E, tab)
                if which == 0:
                    tab = jnp.where(qi >= ki, tab, NEG_INF)
                bias_scr[hh, which] = tab
        i16 = lax.broadcasted_iota(I32, (MASK_ROWS, s), 0)
        ind16 = jnp.where(i16 == lax.broadcasted_iota(I32, (MASK_ROWS, s), 1) // blk, 1.0, 0.0).astype(BF16)
        for hh in range(2):
            o0 = hd * (1 - hh)
            kaug_scr[hh, o0:o0 + MASK_ROWS, :] = ind16
            kaug_scr[hh, o0 + MASK_ROWS:o0 + hd, :] = jnp.zeros((hd - MASK_ROWS, s), BF16)
            vb_scr[hh, o0:o0 + hd, :] = jnp.ones((hd, s), BF16)

    def prepare():
        n_i = lax.broadcasted_iota(I32, (SUBLANES, s), 0)
        j_i = lax.broadcasted_iota(I32, (SUBLANES, s), 1) // blk
        ind = jnp.where(n_i == j_i, 1.0, 0.0).astype(BF16)
        ktb = kt_ref[...].astype(BF16)
        km8 = lax.dot_general(ind, ktb, _NT, preferred_element_type=F32) * (1.0 / blk)
        vtb = vt_ref[...].astype(BF16)
        for hh in range(2):
            own = slice(hd * hh, hd * (hh + 1))
            kaug_scr[hh, own, :] = ktb[own]
            vb_scr[hh, own, :] = vtb[own]
        qsb = q_ref[...]
        lane_q = lax.broadcasted_iota(I32, (1, LANES), 1)

        for hh in range(2):
            head_lanes = jnp.where((lane_q >= hd * hh) & (lane_q < hd * (hh + 1)), 1.0, 0.0).astype(BF16)
            qm = qsb * head_lanes
            qm_scr[hh] = qm
            c0 = min(MOBA_TOPK + 1, nb) * blk
            for q0, q1, ranked in ((0, c0, False), (c0, s, True)):
                if q1 == q0:
                    continue
                gt = lax.dot_general(km8.astype(BF16), qm[q0:q1], _NT, preferred_element_type=F32)
                n_p = lax.broadcasted_iota(I32, gt.shape, 0)
                j_p = (lax.broadcasted_iota(I32, gt.shape, 1) + q0) // blk
                past = n_p < j_p
                g = jnp.where(past, gt, NEG_INF)
                keep = past & (g > 0.5 * NEG_INF * (hd ** -0.5 * LOG2E))
                if ranked:
                    cnt = jnp.zeros(gt.shape, I32)
                    for m in range(nb - 1):
                        gm = g[m:m + 1, :]
                        cnt = cnt + jnp.where((gm > g) | ((gm == g) & (n_p > m)), 1, 0)
                    keep = keep & (cnt < MOBA_TOPK)
                mpad_scr[hh, :, q0:q1] = jnp.where(keep | (n_p >= j_p), 0.0, NEG_INF)

    lane = lax.broadcasted_iota(I32, (blk, LANES), 1)

    def attend(j, hh, par):
        rows = slice(j * blk, (j + 1) * blk)
        qa = qm_scr[hh, rows, :]
        if j > 0:
            o0 = hd * (1 - hh)
            pieces = [mpad_scr[hh, :, rows], jnp.zeros((LANES - o0 - SUBLANES, blk), F32)]
            if o0:
                pieces.insert(0, jnp.zeros((o0, blk), F32))
            slab = jnp.concatenate(pieces, axis=0)
            qa = qa + slab.T.astype(BF16)
        macc = None
        for n in range(j + 1):
            sc = jnp.dot(qa, kaug_scr[hh, :, n * blk:(n + 1) * blk],
                         preferred_element_type=F32)
            if n == j:
                sc = sc + bias_scr[hh, 0]
            elif n == j - 1:
                sc = sc + bias_scr[hh, 1]
            lg_scr[par, n] = sc
            mx = jnp.maximum(sc[:, 0:LANES], sc[:, LANES:2 * LANES])
            macc = mx if macc is None else jnp.maximum(macc, mx)
        mrow = jnp.max(macc, axis=1, keepdims=True)
        acc = None
        for n in range(j + 1):
            pn = jnp.exp2(lg_scr[par, n] - mrow).astype(BF16)
            pv = lax.dot_general(pn, vb_scr[hh, :, n * blk:(n + 1) * blk], _NT,
                                 preferred_element_type=F32)
            acc = pv if acc is None else acc + pv
        return acc

    def finish(j, accs):
        rows = slice(j * blk, (j + 1) * blk)
        first = lane < hd
        num = jnp.where(first, accs[0], accs[1])
        den = pltpu.roll(jnp.where(first, accs[1], accs[0]), hd, 1)
        o_ref[rows, :] = (num / den * _silu(ga_ref[rows, :])).astype(o_ref.dtype)

    sub8 = lax.broadcasted_iota(I32, (SUBLANES, LANES), 0)
    tiles = hd // SUBLANES

    def head_sums(parts):
        assert len(parts) == SUBLANES
        for k in (4, 2, 1):
            low = (sub8 % (2 * k)) < k
            nxt = []
            for a in range(k):
                first, second = parts[a], parts[a + k]
                lo = jnp.where(low, first, second)
                hi = jnp.where(low, second, first)
                if k == SUBLANES // 2:
                    moved = pltpu.roll(hi, k, 0)
                else:
                    moved = jnp.where(low, pltpu.roll(hi, SUBLANES - k, 0), pltpu.roll(hi, k, 0))
                nxt.append(lo + moved)
            parts = nxt
        return parts[0]

    def score_group(sg, slot):
        def trip(t, carry):
            for qq in range(SCORE_PAGES // SHARE_PAGES):
                p0 = t * SCORE_PAGES + qq * SHARE_PAGES
                pages = [kbuf.at[slot, p0 + pp] for pp in range(SHARE_PAGES)]
                part = [[] for _ in range(SHARE_PAGES)]
                for h in range(N_HEADS):
                    qh = [qb_scr[(h * tiles + i) * SUBLANES:(h * tiles + i + 1) * SUBLANES, :]
                          for i in range(tiles)]
                    for pp in range(SHARE_PAGES):
                        acc = None
                        for i in range(tiles):
                            r0 = (h * tiles + i) * SUBLANES
                            term = pages[pp][r0:r0 + SUBLANES, :] * qh[i]
                            acc = term if acc is None else acc + term
                        part[pp].append(acc)
                logits = [head_sums(part[pp]) for pp in range(SHARE_PAGES)]
                for bb in range(SHARE_PAGES // 2):
                    n = sg * (pg // 2) + p0 // 2 + bb
                    s_ref[n, :, 0:LANES] = logits[2 * bb]
                    s_ref[n, :, LANES:2 * LANES] = logits[2 * bb + 1]
                    bsum = jnp.sum(logits[2 * bb] + logits[2 * bb + 1], axis=1, keepdims=True)
                    g_scr[n] = jnp.broadcast_to(bsum, (N_HEADS, LANES))
            return carry

        lax.fori_loop(0, pg // SCORE_PAGES, trip, 0)

    items = [(0, hh) for hh in range(2)] + [(j, hh) for j in reversed(range(1, nb)) for hh in range(2)]
    work = sum(j + 1 for j, _ in items)
    segments = [[] for _ in range(gps)]
    done = 0
    for item in items:
        segments[min(gps - 1, done * gps // work)].append(item)
        done += item[0] + 1

    accs = {}
    count = 0
    for sg in range(gps):
        g = seq * gps + sg
        wait_group(g)
        score_group(sg, lax.rem(g, K_SLOTS))

        start_group(lax.rem(g + K_SLOTS, total))

        if sg == 0:
            prepare()
        for j, hh in segments[sg]:
            accs.setdefault(j, []).append(attend(j, hh, count % 2))
            count += 1
            if len(accs[j]) == 2:
                finish(j, accs.pop(j))

    @pl.when(seq == n_seq - 1)
    def _():
        for g in range(K_SLOTS):
            wait_group(g)

    gate = g_scr[...] * (1.0 / MOBA_BLOCK)
    nidx = lax.broadcasted_iota(I32, gate.shape, 0)
    for r in range(MOBA_TOPK):
        mx = jnp.max(gate, axis=0)
        am = jnp.min(jnp.where(gate == mx[None], nidx, gate.shape[0]), axis=0)
        idx_ref[r] = am
        val_ref[r] = mx
        gate = jnp.where(nidx == am[None], -jnp.inf, gate)


def _attn_and_score(q, kt, vt, ga, rel_bias, pt_flat, q_dec, kcache_t, n_seq, n_pages):
    b, s, w = q.shape
    hd = w // N_HEADS
    blk = MOBA_BLOCK
    nb = s // blk
    npair = w // LANES
    nblk = n_pages // 2
    assert nb <= SUBLANES and LANES == 2 * hd
    assert n_seq == npair * b and n_pages % PAGES_PER_GROUP == 0
    assert (n_seq * (n_pages // PAGES_PER_GROUP)) % K_SLOTS == 0
    dq = jnp.arange(blk, dtype=I32)
    d_own = dq[:, None] - dq[None, :]
    bkt = jnp.stack([_t5_bucket(d_own), _t5_bucket(d_own + blk)])
    per_seq = lambda p, i, pt: (p * b + i, 0, 0, 0)
    grid_spec = pltpu.PrefetchScalarGridSpec(
        num_scalar_prefetch=1,
        grid=(npair, b),
        in_specs=[pl.BlockSpec(memory_space=pltpu.SMEM),
                  pl.BlockSpec((None, s, LANES), lambda p, i, pt: (i, 0, p)),
                  pl.BlockSpec((None, LANES, s), lambda p, i, pt: (i, p, 0)),
                  pl.BlockSpec((None, LANES, s), lambda p, i, pt: (i, p, 0)),
                  pl.BlockSpec((None, s, LANES), lambda p, i, pt: (i, 0, p)),
                  pl.BlockSpec((2, blk, blk), lambda p, i, pt: (0, 0, 0)),
                  pl.BlockSpec((w, LANES), lambda p, i, pt: (0, 0)),
                  pl.BlockSpec(memory_space=pl.ANY)],
        out_specs=[pl.BlockSpec((None, s, LANES), lambda p, i, pt: (i, 0, p)),
                   pl.BlockSpec((None, nblk, N_HEADS, MOBA_BLOCK), per_seq),
                   pl.BlockSpec((None, MOBA_TOPK, N_HEADS, LANES), per_seq),
                   pl.BlockSpec((None, MOBA_TOPK, N_HEADS, LANES), per_seq)],
        scratch_shapes=[pltpu.VMEM((2, 2, blk, blk), F32),
                        pltpu.VMEM((2, LANES, s), BF16),
                        pltpu.VMEM((2, LANES, s), BF16),
                        pltpu.VMEM((2, SUBLANES, s), F32),
                        pltpu.VMEM((2, s, LANES), BF16),
                        pltpu.VMEM((2, nb, blk, blk), F32),
                        pltpu.VMEM((K_SLOTS, PAGES_PER_GROUP, w, LANES), F32),
                        pltpu.SemaphoreType.DMA((K_SLOTS,)),
                        pltpu.VMEM((w, LANES), F32),
                        pltpu.VMEM((nblk, N_HEADS, LANES), F32)])
    return pl.pallas_call(
        functools.partial(_attn_score_kernel, hd=hd, nb=nb, n_pages=n_pages, n_seq=n_seq),
        out_shape=[jax.ShapeDtypeStruct((b, s, w), BF16),
                   jax.ShapeDtypeStruct((n_seq, nblk, N_HEADS, MOBA_BLOCK), F32),
                   jax.ShapeDtypeStruct((n_seq, MOBA_TOPK, N_HEADS, LANES), I32),
                   jax.ShapeDtypeStruct((n_seq, MOBA_TOPK, N_HEADS, LANES), F32)],
        grid_spec=grid_spec,
        compiler_params=_params("arbitrary", "arbitrary"),
        name="attn_score",
    )(pt_flat, rel_bias, q, kt, vt, ga, bkt, q_dec, kcache_t)


def _outproj_kernel(x_ref, mod_ref, ml_ref, ma_ref, wo_ref, gf_ref, y_ref, *, d, w):
    acc = jnp.dot(ml_ref[...].astype(BF16), wo_ref[0:w, :], preferred_element_type=F32)
    acc = acc + jnp.dot(ma_ref[...].astype(BF16), wo_ref[w:2 * w, :], preferred_element_type=F32)
    out = x_ref[...] + mod_ref[:, 2 * d:3 * d] * acc
    ms = jnp.mean(out * out, axis=-1, keepdims=True)
    y_ref[...] = out * lax.rsqrt(ms + RMS_EPS) * gf_ref[...]


def _outproj_prompt(x, mod3, mix_l, mix_a, w_out, g_final):
    b, s, d = x.shape
    w = mix_l.shape[-1]
    tm = OUT_TILE
    return pl.pallas_call(
        functools.partial(_outproj_kernel, d=d, w=w),
        out_shape=jax.ShapeDtypeStruct((b, s, d), F32),
        grid=(b, s // tm),
        in_specs=[pl.BlockSpec((None, tm, d), lambda i, t: (i, t, 0)),
                  pl.BlockSpec((None, 1, 3 * d), lambda i, t: (i, 0, 0)),
                  pl.BlockSpec((None, tm, w), lambda i, t: (i, t, 0)),
                  pl.BlockSpec((None, tm, w), lambda i, t: (i, t, 0)),
                  pl.BlockSpec((2 * w, d), lambda i, t: (0, 0)),
                  pl.BlockSpec((1, d), lambda i, t: (0, 0))],
        out_specs=pl.BlockSpec((None, tm, d), lambda i, t: (i, t, 0)),
        compiler_params=_params("arbitrary", "arbitrary"),
        name="outproj_prompt",
    )(x, mod3, mix_l, mix_a, w_out, g_final)


def _sample_proj_kernel(x_ref, mod_ref, g_ref, win_ref, wqkt_ref, cbuf_ref, h0_ref, cw_ref, cb_ref,
                        wg_ref, bg_ref, ap_ref,
                        qt_ref, kt_ref, k_ref, v_ref, h_ref, cnew_ref, ml_ref, ga_ref, *, d, w):
    n = x_ref.shape[0]
    xm = _modulated_norm(x_ref[...], mod_ref[...], g_ref[...], d).astype(BF16)
    pr = jnp.dot(xm, win_ref[...], preferred_element_type=F32)
    u = pr[:, 0:w]
    k_ref[...] = pr[:, 3 * w:4 * w]
    v_ref[...] = pr[:, 4 * w:5 * w]
    ga_ref[...] = pr[:, 5 * w:6 * w]
    xpad = jnp.concatenate([xm, jnp.zeros((LANES - n, d), BF16)], axis=0)
    pt = lax.dot_general(wqkt_ref[2 * w:4 * w, :], xpad, (((1,), (1,)), ((), ())),
                         preferred_element_type=F32)
    qt_ref[...] = pt[0:w]
    kt_ref[...] = pt[w:2 * w]
    cw = cw_ref[...]
    u_conv = cb_ref[...] + (cbuf_ref[0] * cw[0:1] + cbuf_ref[1] * cw[1:2]
                            + cbuf_ref[2] * cw[2:3] + u * cw[3:4])
    a, bx = _lru_gates(u_conv, wg_ref, bg_ref, ap_ref, w)
    h = a * h0_ref[...] + bx
    h_ref[...] = h
    cnew_ref[0] = cbuf_ref[1]
    cnew_ref[1] = cbuf_ref[2]
    cnew_ref[2] = u
    ml_ref[...] = h * _silu(pr[:, w:2 * w])


def _sample_proj(x, mod, g_norm, w_in_bf, w_in_t, cbuf, h0, conv_w, conv_b, wg, bg, a_param):
    n, d = x.shape
    w = h0.shape[1]
    row = jax.ShapeDtypeStruct((n, w), F32)
    col = jax.ShapeDtypeStruct((w, LANES), F32)
    return pl.pallas_call(
        functools.partial(_sample_proj_kernel, d=d, w=w),
        out_shape=[col, col, row, row, row, jax.ShapeDtypeStruct((CONV_W - 1, n, w), F32), row, row],
        compiler_params=pltpu.CompilerParams(vmem_limit_bytes=VMEM_LIMIT_BYTES),
        name="sample_proj",
    )(x, mod, g_norm, w_in_bf, w_in_t, cbuf, h0, conv_w, conv_b, wg, bg, a_param)


def _decode_kernel(pt_ref, ix_ref, s_ref, idxv_ref, valv_ref, qt_ref, kt_ref, vnew_ref, rbt_ref,
                   bkt_ref, vc_ref, y_ref, vbuf, sem, *, n_pages, n_seq, hd):
    b = pl.program_id(0)
    nsel = MOBA_TOPK
    blk = MOBA_BLOCK
    nb = n_pages // 2
    grp = DEC_SEQS
    n_steps = n_seq // grp

    def tile_copy(step, gi, r, h, pp, slot):
        sq = step * grp + gi
        page = pt_ref[sq * n_pages + 2 * ix_ref[(sq * nsel + r) * N_HEADS + h] + pp]
        return pltpu.make_async_copy(
            vc_ref.at[page, h], vbuf.at[slot, gi, h, :, pl.ds((2 * r + pp) * LANES, LANES)], sem.at[slot])

    def for_tiles(step, fn):
        slot = lax.rem(step, 2)
        for gi in range(grp):
            for r in range(nsel):
                for h in range(N_HEADS):
                    for pp in range(2):
                        fn(tile_copy(step, gi, r, h, pp, slot))

    @pl.when(b == 0)
    def _():
        for_tiles(0, lambda cp: cp.start())

    for_tiles(lax.rem(b + 1, n_steps), lambda cp: cp.start())

    sub = lax.broadcasted_iota(I32, (N_HEADS, blk), 0)
    rbt = rbt_ref[...]
    bias_last = jnp.zeros((N_HEADS, blk), F32)
    for k in range(REL_BUCKETS):
        bias_last = jnp.where(bkt_ref[...] == k, rbt[:, k:k + 1], bias_last)
    bias_far = rbt[:, REL_BUCKETS - 1:REL_BUCKETS]
    lane = lax.broadcasted_iota(I32, (N_HEADS, LANES), 1)
    qk = jnp.sum((qt_ref[...] * kt_ref[...]).reshape(N_HEADS, hd, LANES), axis=1)

    softmax = []
    for gi in range(grp):
        sq = b * grp + gi
        logits = []
        for r in range(nsel):
            lr = jnp.zeros((N_HEADS, blk), F32)
            for h in range(N_HEADS):
                n = ix_ref[(sq * nsel + r) * N_HEADS + h]
                lr = jnp.where(sub == h, s_ref[gi, n], lr)
            idv = idxv_ref[gi, r][:, 0:1]
            bias = jnp.where(idv == nb - 1, bias_last, bias_far)
            valid = valv_ref[gi, r][:, 0:1] > 0.5 * NEG_INF
            logits.append(jnp.where(valid, lr * (hd ** -0.5) + bias, NEG_INF))

        own = jnp.sum(jnp.where(lane == sq, qk, 0.0), axis=1, keepdims=True)
        l_own = own * (hd ** -0.5) + rbt[:, 0:1]
        m = l_own
        for lr in logits:
            m = jnp.maximum(m, jnp.max(lr, axis=1, keepdims=True))
        p_own = jnp.exp(l_own - m)
        den = p_own
        probs = []
        for lr in logits:
            pr = jnp.exp(lr - m)
            den = den + jnp.sum(pr, axis=1, keepdims=True)
            probs.append(pr)
        softmax.append((jnp.concatenate(probs, axis=1), p_own, den))

    for_tiles(b, lambda cp: cp.wait())
    slot = lax.rem(b, 2)
    for gi in range(grp):
        pcat, p_own, den = softmax[gi]
        rows = []
        for h in range(N_HEADS):
            vt = vbuf[slot, gi, h].astype(BF16)
            rows.append(lax.dot_general(pcat[h:h + 1, :].astype(BF16), vt, (((1,), (1,)), ((), ())),
                                        preferred_element_type=F32))
        acc = jnp.concatenate(rows, axis=0)
        y_ref[gi] = (acc + p_own * vnew_ref[gi]) / den

    @pl.when(b == n_steps - 1)
    def _():
        for_tiles(0, lambda cp: cp.wait())


def _decode(pt_flat, idx_flat, s_all, idxv, valv, qt, kt, v_new, rbt, bkt_last, vcache_t, n_seq, n_pages):
    w = qt.shape[0]
    hd = w // N_HEADS
    nb = n_pages // 2
    grp = DEC_SEQS
    assert n_seq % (2 * grp) == 0
    zero4 = lambda b, pt, ix: (b, 0, 0, 0)
    grid_spec = pltpu.PrefetchScalarGridSpec(
        num_scalar_prefetch=2,
        grid=(n_seq // grp,),
        in_specs=[pl.BlockSpec((grp, nb, N_HEADS, MOBA_BLOCK), zero4),
                  pl.BlockSpec((grp, MOBA_TOPK, N_HEADS, LANES), zero4),
                  pl.BlockSpec((grp, MOBA_TOPK, N_HEADS, LANES), zero4),
                  pl.BlockSpec((w, LANES), lambda b, pt, ix: (0, 0)),
                  pl.BlockSpec((w, LANES), lambda b, pt, ix: (0, 0)),
                  pl.BlockSpec((grp, N_HEADS, hd), lambda b, pt, ix: (b, 0, 0)),
                  pl.BlockSpec((N_HEADS, REL_BUCKETS), lambda b, pt, ix: (0, 0)),
                  pl.BlockSpec((1, MOBA_BLOCK), lambda b, pt, ix: (0, 0)),
                  pl.BlockSpec(memory_space=pl.ANY)],
        out_specs=pl.BlockSpec((grp, N_HEADS, hd), lambda b, pt, ix: (b, 0, 0)),
        scratch_shapes=[pltpu.VMEM((2, grp, N_HEADS, hd, MOBA_TOPK * MOBA_BLOCK), F32),
                        pltpu.SemaphoreType.DMA((2,))])
    return pl.pallas_call(
        functools.partial(_decode_kernel, n_pages=n_pages, n_seq=n_seq, hd=hd),
        out_shape=jax.ShapeDtypeStruct((n_seq, N_HEADS, hd), F32),
        grid_spec=grid_spec,
        compiler_params=_params("arbitrary"),
        name="decode_attn",
    )(pt_flat, idx_flat, s_all, idxv, valv, qt, kt, v_new, rbt, bkt_last, vcache_t)


def _outproj_sample_kernel(x_ref, mod_ref, ml_ref, ya_ref, ga_ref, wo_ref, gf_ref, y_ref, *, d, w):
    ma = ya_ref[...] * _silu(ga_ref[...])
    acc = jnp.dot(ml_ref[...].astype(BF16), wo_ref[0:w, :], preferred_element_type=F32)
    acc = acc + jnp.dot(ma.astype(BF16), wo_ref[w:2 * w, :], preferred_element_type=F32)
    out = x_ref[...] + mod_ref[:, 2 * d:3 * d] * acc
    ms = jnp.mean(out * out, axis=-1, keepdims=True)
    y_ref[...] = out * lax.rsqrt(ms + RMS_EPS) * gf_ref[...]


def _outproj_sample(x, mod, mix_l, y_att, ga, w_out, g_final):
    n, d = x.shape
    w = mix_l.shape[1]
    return pl.pallas_call(
        functools.partial(_outproj_sample_kernel, d=d, w=w),
        out_shape=jax.ShapeDtypeStruct((n, d), F32),
        compiler_params=pltpu.CompilerParams(vmem_limit_bytes=VMEM_LIMIT_BYTES),
        name="outproj_sample",
    )(x, mod, mix_l, y_att, ga, w_out, g_final)


def _block_diag(wb):
    n, c, dd = wb.shape
    return jnp.einsum("ncd,nm->ncmd", wb, jnp.eye(n, dtype=wb.dtype)).reshape(n * c, n * dd)


def kernel(x_prompt, x_sample, cache_k, cache_v, state_lru_h, state_lru_conv, page_table, c_prompt, c_sample, w_ada, b_ada, g_norm, w_in, conv_w, conv_b, w_rgate, b_rgate, w_igate, b_igate, lru_a_param, rel_bias, w_out, g_final):
    bp, s, d = x_prompt.shape
    ns = x_sample.shape[0]
    w = state_lru_h.shape[1]
    hd = w // N_HEADS
    n_pages = page_table.shape[1]
    page = cache_k.shape[1]
    past = n_pages * page
    assert x_sample.shape[1] == 1 and 2 * page == MOBA_BLOCK and past % MOBA_BLOCK == 0
    assert s % MOBA_BLOCK == 0 and s % ROW_TILE == 0 and s % OUT_TILE == 0 and s % LRU_CHUNK == 0
    assert w % LANES == 0 and ns <= LANES
    assert MOBA_BLOCK + 1 >= REL_MAX_DIST

    w_in_bf, w_in_t = _wprep(w_in, w)
    w_out_bf = w_out.astype(BF16)
    wg = jnp.concatenate([_block_diag(w_rgate), _block_diag(w_igate)], axis=1).astype(BF16)
    bg = jnp.concatenate([b_rgate, b_igate]).reshape(1, 2 * w)
    g_norm2 = g_norm.reshape(1, d)
    g_final2 = g_final.reshape(1, d)
    conv_b2 = conv_b.reshape(1, w)
    a_param2 = lru_a_param.reshape(1, w)

    mod = _adaln(jnp.concatenate([c_prompt, c_sample], axis=0), w_ada, b_ada)
    mod_p = mod[0:bp].reshape(bp, 1, 3 * d)
    mod_s = mod[bp:bp + ns]

    xs = x_sample.reshape(ns, d)
    cbuf = state_lru_conv.transpose(1, 0, 2)
    qt, ktn, k_s, v_s, h_s, cnew, mix_ls, ga_s = _sample_proj(
        xs, mod_s, g_norm2, w_in_bf, w_in_t, cbuf, state_lru_h, conv_w, conv_b2, wg, bg, a_param2)
    pt_flat = page_table.reshape(-1)
    n_phys = cache_k.shape[0]
    kc_t = cache_k.transpose(0, 2, 3, 1).reshape(n_phys, w, page)
    vc_t = cache_v.transpose(0, 2, 3, 1)

    u, gl, q, ga, kt, vt = _inproj(x_prompt, mod_p, g_norm2, w_in_bf, w_in_t)
    mix_l, h_last, tail = _lru_prompt(u, gl, conv_w, conv_b2, wg, bg, a_param2)
    mix_a, s_all, idxv, valv = _attn_and_score(q, kt, vt, ga, rel_bias, pt_flat, qt, kc_t, ns, n_pages)
    y_prompt = _outproj_prompt(x_prompt, mod_p, mix_l, mix_a, w_out_bf, g_final2)
    k_prompt = kt.reshape(bp, N_HEADS, hd, s).transpose(0, 3, 1, 2)
    v_prompt = vt.reshape(bp, N_HEADS, hd, s).transpose(0, 3, 1, 2)
    lru_h_prompt = h_last[:, 0, :]
    lru_conv_prompt = tail[:, SUBLANES - (CONV_W - 1):, :]

    idx_flat = idxv[:, :, :, 0].reshape(-1)
    nb = n_pages // 2
    pos = (nb - 1) * MOBA_BLOCK + jnp.arange(MOBA_BLOCK, dtype=I32)
    bkt_last = _t5_bucket(past - pos).reshape(1, MOBA_BLOCK)
    y_att = _decode(pt_flat, idx_flat, s_all, idxv, valv, qt, ktn, v_s.reshape(ns, N_HEADS, hd),
                    rel_bias.T, bkt_last, vc_t, ns, n_pages)
    y_s = _outproj_sample(xs, mod_s, mix_ls, y_att.reshape(ns, w), ga_s, w_out_bf, g_final2)

    return (y_prompt, y_s.reshape(ns, 1, d), k_prompt, v_prompt, lru_h_prompt, lru_conv_prompt,
            k_s.reshape(ns, 1, N_HEADS, hd), v_s.reshape(ns, 1, N_HEADS, hd), h_s,
            cnew.transpose(1, 0, 2))
```

```python
import functools
import math

import jax
import jax.numpy as jnp
from jax import lax
from jax.experimental import pallas as pl
from jax.experimental.pallas import tpu as pltpu

F32 = jnp.float32
BF16 = jnp.bfloat16
I32 = jnp.int32

N_HEADS = 8
LRU_BLOCKS = 8
CONV_W = 4
LRU_C = 8.0
MOBA_BLOCK = 256
MOBA_TOPK = 3
REL_BUCKETS = 32
REL_MAX_DIST = 128
RMS_EPS = 1e-6
NEG_INF = -1e30
HIGHEST = lax.Precision.HIGHEST
LOG2E = math.log2(math.e)

LANES = 128
SUBLANES = 8
VMEM_LIMIT_BYTES = 56 * 1024 * 1024

ROW_TILE = 1024
OUT_TILE = 1024
LRU_CHUNK = 512
PAGES_PER_GROUP = 64
K_SLOTS = 2
SCORE_PAGES = 16
SHARE_PAGES = 4
MASK_ROWS = 16
DEC_SEQS = 8


def _sigmoid(x):
    return 0.5 * jnp.tanh(0.5 * x) + 0.5


def _silu(x):
    return x * _sigmoid(x)


def _softplus(z):
    return jnp.maximum(z, 0.0) + jnp.log1p(jnp.exp(-jnp.abs(z)))


def _t5_bucket(dist):
    n = jnp.maximum(dist, 0)
    max_exact = REL_BUCKETS // 2
    nf = jnp.maximum(n, 1).astype(F32)
    pos = jnp.log(nf / max_exact) / math.log(REL_MAX_DIST / max_exact) * (REL_BUCKETS - max_exact)
    large = max_exact + jnp.where(pos >= 0, jnp.floor(pos), jnp.ceil(pos)).astype(I32)
    large = jnp.minimum(large, REL_BUCKETS - 1)
    return jnp.where(n < max_exact, n, large)


def _params(*sem):
    return pltpu.CompilerParams(dimension_semantics=sem, vmem_limit_bytes=VMEM_LIMIT_BYTES)


def _adaln_kernel(c_ref, w_ref, b_ref, o_ref):
    s = _silu(c_ref[...]).astype(BF16)
    o_ref[...] = jnp.dot(s, w_ref[...].astype(BF16), preferred_element_type=F32) + b_ref[...]


def _adaln(c, w_ada, b_ada):
    n, d = c.shape
    return pl.pallas_call(
        _adaln_kernel,
        out_shape=jax.ShapeDtypeStruct((n, 3 * d), F32),
        grid=(3,),
        in_specs=[pl.BlockSpec((n, d), lambda j: (0, 0)),
                  pl.BlockSpec((d, d), lambda j: (0, j)),
                  pl.BlockSpec((1, d), lambda j: (0, j))],
        out_specs=pl.BlockSpec((n, d), lambda j: (0, j)),
        compiler_params=_params("arbitrary"),
        name="adaln",
    )(c, w_ada, b_ada.reshape(1, 3 * d))


def _wprep_kernel(w_ref, wb_ref, wt_ref):
    x = w_ref[...]
    wb_ref[...] = x.astype(BF16)
    wt_ref[...] = x.T.astype(BF16)


def _wprep(w_in, cols):
    d, n = w_in.shape
    return pl.pallas_call(
        _wprep_kernel,
        out_shape=[jax.ShapeDtypeStruct((d, n), BF16), jax.ShapeDtypeStruct((n, d), BF16)],
        grid=(n // cols,),
        in_specs=[pl.BlockSpec((d, cols), lambda j: (0, j))],
        out_specs=[pl.BlockSpec((d, cols), lambda j: (0, j)), pl.BlockSpec((cols, d), lambda j: (j, 0))],
        compiler_params=_params("arbitrary"),
        name="wprep",
    )(w_in)


def _modulated_norm(x, mod, g, d):
    ms = jnp.mean(x * x, axis=-1, keepdims=True)
    xn = x * lax.rsqrt(ms + RMS_EPS) * g
    return xn * (1.0 + mod[:, d:2 * d]) + mod[:, 0:d]


def _inproj_kernel(x_ref, mod_ref, g_ref, win_ref, wkt_ref, wvt_ref,
                   u_ref, gl_ref, q_ref, ga_ref, kt_ref, vt_ref, ktb_ref, vtb_ref, *, d, w):
    xm = _modulated_norm(x_ref[...], mod_ref[...], g_ref[...], d).astype(BF16)
    pr = jnp.dot(xm, win_ref[:, 0:3 * w], preferred_element_type=F32)
    u_ref[...] = pr[:, 0:w]
    gl_ref[...] = pr[:, w:2 * w]
    q_ref[...] = (pr[:, 2 * w:3 * w] * ((w // N_HEADS) ** -0.5 * LOG2E)).astype(BF16)
    ga_ref[...] = jnp.dot(xm, win_ref[:, 5 * w:6 * w], preferred_element_type=F32)
    nt = (((1,), (1,)), ((), ()))
    kt = lax.dot_general(wkt_ref[...], xm, nt, preferred_element_type=F32)
    vt = lax.dot_general(wvt_ref[...], xm, nt, preferred_element_type=F32)
    kt_ref[...] = kt
    vt_ref[...] = vt
    ktb_ref[...] = kt.astype(BF16)
    vtb_ref[...] = vt.astype(BF16)


def _inproj(x, mod3, g_norm, w_in_bf, w_in_t):
    b, s, d = x.shape
    w = w_in_bf.shape[1] // 6
    tm = ROW_TILE
    row = pl.BlockSpec((None, tm, w), lambda i, t: (i, t, 0))
    col = pl.BlockSpec((None, w, tm), lambda i, t: (i, 0, t))
    return pl.pallas_call(
        functools.partial(_inproj_kernel, d=d, w=w),
        out_shape=[jax.ShapeDtypeStruct((b, s, w), F32)] * 2 + [jax.ShapeDtypeStruct((b, s, w), BF16)]
        + [jax.ShapeDtypeStruct((b, s, w), F32)] + [jax.ShapeDtypeStruct((b, w, s), F32)] * 2
        + [jax.ShapeDtypeStruct((b, w, s), BF16)] * 2,
        grid=(b, s // tm),
        in_specs=[pl.BlockSpec((None, tm, d), lambda i, t: (i, t, 0)),
                  pl.BlockSpec((None, 1, 3 * d), lambda i, t: (i, 0, 0)),
                  pl.BlockSpec((1, d), lambda i, t: (0, 0)),
                  pl.BlockSpec((d, 6 * w), lambda i, t: (0, 0)),
                  pl.BlockSpec((w, d), lambda i, t: (3, 0)),
                  pl.BlockSpec((w, d), lambda i, t: (4, 0))],
        out_specs=[row, row, row, row, col, col, col, col],
        compiler_params=_params("arbitrary", "arbitrary"),
        name="inproj",
    )(x, mod3, g_norm, w_in_bf, w_in_t, w_in_t)


def _lru_gates(u_conv, wg_ref, bg_ref, ap_ref, w):
    g2 = jnp.dot(u_conv.astype(BF16), wg_ref[...], preferred_element_type=F32) + bg_ref[...]
    r = _sigmoid(g2[:, 0:w])
    i = _sigmoid(g2[:, w:2 * w])
    log_a = (-LRU_C * r) * _softplus(-ap_ref[...])
    a = jnp.exp(log_a)
    z = -jnp.tanh(log_a) * (a * a + 1.0)
    root = jnp.where(z > 0.0, z * lax.rsqrt(z), 0.0)
    bx = root * (i * u_conv)
    return a, bx


def _lru_kernel(u_ref, gl_ref, cw_ref, cb_ref, wg_ref, bg_ref, ap_ref,
                mix_ref, hl_ref, tail_ref, a_scr, b_scr, hs_scr, h_scr, tail_scr, *, tc, w):
    c = pl.program_id(1)

    @pl.when(c == 0)
    def _():
        h_scr[...] = jnp.zeros_like(h_scr)
        tail_scr[...] = jnp.zeros_like(tail_scr)

    u = u_ref[...]
    cw = cw_ref[...]
    tail = tail_scr[...]
    u1, u2, u3 = (tail[SUBLANES - k:SUBLANES - k + 1] for k in (1, 2, 3))

    def shift1(z, first):
        return jnp.concatenate([first, z[0:tc - 1]], axis=0)

    z = shift1(u * cw[0:1], u1 * cw[0:1])
    z = shift1(u * cw[1:2] + z, u1 * cw[1:2] + u2 * cw[0:1])
    z = shift1(u * cw[2:3] + z, u1 * cw[2:3] + u2 * cw[1:2] + u3 * cw[0:1])
    u_conv = cb_ref[...] + (u * cw[3:4] + z)
    tail_scr[...] = u[tc - SUBLANES:tc]
    a, bx = _lru_gates(u_conv, wg_ref, bg_ref, ap_ref, w)
    a_scr[...] = a
    b_scr[...] = bx

    row = lax.broadcasted_iota(I32, (SUBLANES, w), 0)

    def tile(t, h):
        r0 = pl.multiple_of(t * SUBLANES, SUBLANES)
        at = a_scr[pl.ds(r0, SUBLANES), :]
        bt = b_scr[pl.ds(r0, SUBLANES), :]
        for dd in (1, 2, 4):
            keep = row >= dd
            bt = jnp.where(keep, at * pltpu.roll(bt, dd, 0) + bt, bt)
            at = jnp.where(keep, at * pltpu.roll(at, dd, 0), at)
        hs = at * h + bt
        hs_scr[pl.ds(r0, SUBLANES), :] = hs
        return jnp.broadcast_to(hs[SUBLANES - 1:SUBLANES, :], (SUBLANES, w))

    h = lax.fori_loop(0, tc // SUBLANES, tile, h_scr[...], unroll=4)
    h_scr[...] = h
    hl_ref[...] = h
    tail_ref[...] = u[tc - SUBLANES:tc]
    mix_ref[...] = (hs_scr[...] * _silu(gl_ref[...])).astype(mix_ref.dtype)


def _lru_prompt(u, gl, conv_w, conv_b, wg, bg, a_param):
    b, s, w = u.shape
    tc = LRU_CHUNK
    const = lambda shape: pl.BlockSpec(shape, lambda i, t: (0,) * len(shape))
    return pl.pallas_call(
        functools.partial(_lru_kernel, tc=tc, w=w),
        out_shape=[jax.ShapeDtypeStruct((b, s, w), BF16),
                   jax.ShapeDtypeStruct((b, SUBLANES, w), F32),
                   jax.ShapeDtypeStruct((b, SUBLANES, w), F32)],
        grid=(b, s // tc),
        in_specs=[pl.BlockSpec((None, tc, w), lambda i, t: (i, t, 0)),
                  pl.BlockSpec((None, tc, w), lambda i, t: (i, t, 0)),
                  const((CONV_W, w)), const((1, w)), const((w, 2 * w)), const((1, 2 * w)),
                  const((1, w))],
        out_specs=[pl.BlockSpec((None, tc, w), lambda i, t: (i, t, 0)),
                   pl.BlockSpec((None, SUBLANES, w), lambda i, t: (i, 0, 0)),
                   pl.BlockSpec((None, SUBLANES, w), lambda i, t: (i, 0, 0))],
        scratch_shapes=[pltpu.VMEM((tc, w), F32), pltpu.VMEM((tc, w), F32),
                        pltpu.VMEM((tc, w), F32), pltpu.VMEM((SUBLANES, w), F32),
                        pltpu.VMEM((SUBLANES, w), F32)],
        compiler_params=_params("arbitrary", "arbitrary"),
        name="lru_prompt",
    )(u, gl, conv_w, conv_b, wg, bg, a_param)


_NT = (((1,), (1,)), ((), ()))


def _attn_score_kernel(pt_ref, rb_ref, q_ref, kt_ref, vt_ref, ga_ref, bkt_ref, qt_ref, kc_ref,
                       o_ref, s_ref, idx_ref, val_ref,
                       bias_scr, kaug_scr, vb_scr, mpad_scr, qm_scr, lg_scr, kbuf, sem, qb_scr, g_scr,
                       *, hd, nb, n_pages, n_seq):
    p = pl.program_id(0)
    b = pl.program_id(1)
    blk = MOBA_BLOCK
    s = nb * blk
    pg = PAGES_PER_GROUP
    gps = n_pages // pg
    total = n_seq * gps
    seq = p * pl.num_programs(1) + b

    def page_copy(g, jj, slot):
        return pltpu.make_async_copy(kc_ref.at[pt_ref[g * pg + jj]], kbuf.at[slot, jj], sem.at[slot])

    def start_group(g):
        slot = lax.rem(g, K_SLOTS)
        for jj in range(pg):
            page_copy(g, jj, slot).start()

    def wait_group(g):
        slot = lax.rem(g, K_SLOTS)
        for jj in range(pg):
            page_copy(g, jj, slot).wait()

    @pl.when(seq == 0)
    def _():
        for g in range(K_SLOTS):
            start_group(g)

    lane_t = lax.broadcasted_iota(I32, qt_ref.shape, 1)
    qcol = jnp.sum(jnp.where(lane_t == seq, qt_ref[...], 0.0), axis=1, keepdims=True)
    qb_scr[...] = jnp.broadcast_to(qcol, qb_scr.shape)

    @pl.when(b == 0)
    def _():
        qi = lax.broadcasted_iota(I32, (blk, blk), 0)
        ki = lax.broadcasted_iota(I32, (blk, blk), 1)
        for hh in range(2):
            h = 2 * p + hh
            far = rb_ref[REL_BUCKETS - 1, h]
            for which in range(2):
                bk = bkt_ref[which]
                tab = jnp.zeros((blk, blk), F32)
                for k in range(REL_BUCKETS):
                    tab = jnp.where(bk == k, (rb_ref[k, h] - far) * LOG2E, tab)
                if which == 0:
                    tab = jnp.where(qi >= ki, tab, NEG_INF)
                bias_scr[hh, which] = tab
        i16 = lax.broadcasted_iota(I32, (MASK_ROWS, s), 0)
        ind16 = jnp.where(i16 == lax.broadcasted_iota(I32, (MASK_ROWS, s), 1) // blk, 1.0, 0.0).astype(BF16)
        for hh in range(2):
            o0 = hd * (1 - hh)
            kaug_scr[hh, o0:o0 + MASK_ROWS, :] = ind16
            kaug_scr[hh, o0 + MASK_ROWS:o0 + hd, :] = jnp.zeros((hd - MASK_ROWS, s), BF16)
            vb_scr[hh, o0:o0 + hd, :] = jnp.ones((hd, s), BF16)

    def prepare():
        n_i = lax.broadcasted_iota(I32, (SUBLANES, s), 0)
        j_i = lax.broadcasted_iota(I32, (SUBLANES, s), 1) // blk
        ind = jnp.where(n_i == j_i, 1.0, 0.0).astype(BF16)
        ktb = kt_ref[...]
        km8 = lax.dot_general(ind, ktb, _NT, preferred_element_type=F32) * (1.0 / blk)
        vtb = vt_ref[...]
        for hh in range(2):
            own = slice(hd * hh, hd * (hh + 1))
            kaug_scr[hh, own, :] = ktb[own]
            vb_scr[hh, own, :] = vtb[own]
        qsb = q_ref[...]
        lane_q = lax.broadcasted_iota(I32, (1, LANES), 1)

        for hh in range(2):
            head_lanes = jnp.where((lane_q >= hd * hh) & (lane_q < hd * (hh + 1)), 1.0, 0.0).astype(BF16)
            qm = qsb * head_lanes
            qm_scr[hh] = qm
            c0 = min(MOBA_TOPK + 1, nb) * blk
            for q0, q1, ranked in ((0, c0, False), (c0, s, True)):
                if q1 == q0:
                    continue
                gt = lax.dot_general(km8.astype(BF16), qm[q0:q1], _NT, preferred_element_type=F32)
                n_p = lax.broadcasted_iota(I32, gt.shape, 0)
                j_p = (lax.broadcasted_iota(I32, gt.shape, 1) + q0) // blk
                past = n_p < j_p
                g = jnp.where(past, gt, NEG_INF)
                keep = past & (g > 0.5 * NEG_INF * (hd ** -0.5 * LOG2E))
                if ranked:
                    cnt = jnp.zeros(gt.shape, I32)
                    for m in range(nb - 1):
                        gm = g[m:m + 1, :]
                        cnt = cnt + jnp.where((gm > g) | ((gm == g) & (n_p > m)), 1, 0)
                    keep = keep & (cnt < MOBA_TOPK)
                mpad_scr[hh, :, q0:q1] = jnp.where(keep | (n_p >= j_p), 0.0, NEG_INF)

    lane = lax.broadcasted_iota(I32, (blk, LANES), 1)

    def attend(j, hh, par):
        rows = slice(j * blk, (j + 1) * blk)
        qa = qm_scr[hh, rows, :]
        if j > 0:
            o0 = hd * (1 - hh)
            pieces = [mpad_scr[hh, :, rows], jnp.zeros((LANES - o0 - SUBLANES, blk), F32)]
            if o0:
                pieces.insert(0, jnp.zeros((o0, blk), F32))
            slab = jnp.concatenate(pieces, axis=0)
            qa = qa + slab.T.astype(BF16)
        macc = None
        for n in range(j + 1):
            sc = jnp.dot(qa, kaug_scr[hh, :, n * blk:(n + 1) * blk],
                         preferred_element_type=F32)
            if n == j:
                sc = sc + bias_scr[hh, 0]
            elif n == j - 1:
                sc = sc + bias_scr[hh, 1]
            lg_scr[par, n] = sc
            mx = jnp.maximum(sc[:, 0:LANES], sc[:, LANES:2 * LANES])
            macc = mx if macc is None else jnp.maximum(macc, mx)
        mrow = jnp.max(macc, axis=1, keepdims=True)
        acc = None
        for n in range(j + 1):
            pn = jnp.exp2(lg_scr[par, n] - mrow).astype(BF16)
            pv = lax.dot_general(pn, vb_scr[hh, :, n * blk:(n + 1) * blk], _NT,
                                 preferred_element_type=F32)
            acc = pv if acc is None else acc + pv
        return acc

    def finish(j, accs):
        rows = slice(j * blk, (j + 1) * blk)
        first = lane < hd
        num = jnp.where(first, accs[0], accs[1])
        den = pltpu.roll(jnp.where(first, accs[1], accs[0]), hd, 1)
        o_ref[rows, :] = (num / den * _silu(ga_ref[rows, :])).astype(o_ref.dtype)

    sub8 = lax.broadcasted_iota(I32, (SUBLANES, LANES), 0)
    tiles = hd // SUBLANES

    def head_sums(parts):
        assert len(parts) == SUBLANES
        for k in (4, 2, 1):
            low = (sub8 % (2 * k)) < k
            nxt = []
            for a in range(k):
                first, second = parts[a], parts[a + k]
                lo = jnp.where(low, first, second)
                hi = jnp.where(low, second, first)
                if k == SUBLANES // 2:
                    moved = pltpu.roll(hi, k, 0)
                else:
                    moved = jnp.where(low, pltpu.roll(hi, SUBLANES - k, 0), pltpu.roll(hi, k, 0))
                nxt.append(lo + moved)
            parts = nxt
        return parts[0]

    def score_group(sg, slot):
        def trip(t, carry):
            for qq in range(SCORE_PAGES // SHARE_PAGES):
                p0 = t * SCORE_PAGES + qq * SHARE_PAGES
                pages = [kbuf.at[slot, p0 + pp] for pp in range(SHARE_PAGES)]
                part = [[] for _ in range(SHARE_PAGES)]
                for h in range(N_HEADS):
                    qh = [qb_scr[(h * tiles + i) * SUBLANES:(h * tiles + i + 1) * SUBLANES, :]
                          for i in range(tiles)]
                    for pp in range(SHARE_PAGES):
                        acc = None
                        for i in range(tiles):
                            r0 = (h * tiles + i) * SUBLANES
                            term = pages[pp][r0:r0 + SUBLANES, :] * qh[i]
                            acc = term if acc is None else acc + term
                        part[pp].append(acc)
                logits = [head_sums(part[pp]) for pp in range(SHARE_PAGES)]
                for bb in range(SHARE_PAGES // 2):
                    n = sg * (pg // 2) + p0 // 2 + bb
                    s_ref[n, :, 0:LANES] = logits[2 * bb]
                    s_ref[n, :, LANES:2 * LANES] = logits[2 * bb + 1]
                    bsum = jnp.sum(logits[2 * bb] + logits[2 * bb + 1], axis=1, keepdims=True)
                    g_scr[n] = jnp.broadcast_to(bsum, (N_HEADS, LANES))
            return carry

        lax.fori_loop(0, pg // SCORE_PAGES, trip, 0)

    items = [(0, hh) for hh in range(2)] + [(j, hh) for j in reversed(range(1, nb)) for hh in range(2)]
    work = sum(j + 1 for j, _ in items)
    segments = [[] for _ in range(gps)]
    done = 0
    for item in items:
        segments[min(gps - 1, done * gps // work)].append(item)
        done += item[0] + 1

    accs = {}
    count = 0
    for sg in range(gps):
        g = seq * gps + sg
        wait_group(g)
        score_group(sg, lax.rem(g, K_SLOTS))

        start_group(lax.rem(g + K_SLOTS, total))

        if sg == 0:
            prepare()
        for j, hh in segments[sg]:
            accs.setdefault(j, []).append(attend(j, hh, count % 2))
            count += 1
            if len(accs[j]) == 2:
                finish(j, accs.pop(j))

    @pl.when(seq == n_seq - 1)
    def _():
        for g in range(K_SLOTS):
            wait_group(g)

    gate = g_scr[...] * (1.0 / MOBA_BLOCK)
    nidx = lax.broadcasted_iota(I32, gate.shape, 0)
    for r in range(MOBA_TOPK):
        mx = jnp.max(gate, axis=0)
        am = jnp.min(jnp.where(gate == mx[None], nidx, gate.shape[0]), axis=0)
        idx_ref[r] = am
        val_ref[r] = mx
        gate = jnp.where(nidx == am[None], -jnp.inf, gate)


def _attn_and_score(q, kt, vt, ga, rel_bias, pt_flat, q_dec, kcache_t, n_seq, n_pages):
    b, s, w = q.shape
    hd = w // N_HEADS
    blk = MOBA_BLOCK
    nb = s // blk
    npair = w // LANES
    nblk = n_pages // 2
    assert nb <= SUBLANES and LANES == 2 * hd
    assert n_seq == npair * b and n_pages % PAGES_PER_GROUP == 0
    assert (n_seq * (n_pages // PAGES_PER_GROUP)) % K_SLOTS == 0
    dq = jnp.arange(blk, dtype=I32)
    d_own = dq[:, None] - dq[None, :]
    bkt = jnp.stack([_t5_bucket(d_own), _t5_bucket(d_own + blk)])
    per_seq = lambda p, i, pt: (p * b + i, 0, 0, 0)
    grid_spec = pltpu.PrefetchScalarGridSpec(
        num_scalar_prefetch=1,
        grid=(npair, b),
        in_specs=[pl.BlockSpec(memory_space=pltpu.SMEM),
                  pl.BlockSpec((None, s, LANES), lambda p, i, pt: (i, 0, p)),
                  pl.BlockSpec((None, LANES, s), lambda p, i, pt: (i, p, 0)),
                  pl.BlockSpec((None, LANES, s), lambda p, i, pt: (i, p, 0)),
                  pl.BlockSpec((None, s, LANES), lambda p, i, pt: (i, 0, p)),
                  pl.BlockSpec((2, blk, blk), lambda p, i, pt: (0, 0, 0)),
                  pl.BlockSpec((w, LANES), lambda p, i, pt: (0, 0)),
                  pl.BlockSpec(memory_space=pl.ANY)],
        out_specs=[pl.BlockSpec((None, s, LANES), lambda p, i, pt: (i, 0, p)),
                   pl.BlockSpec((None, nblk, N_HEADS, MOBA_BLOCK), per_seq),
                   pl.BlockSpec((None, MOBA_TOPK, N_HEADS, LANES), per_seq),
                   pl.BlockSpec((None, MOBA_TOPK, N_HEADS, LANES), per_seq)],
        scratch_shapes=[pltpu.VMEM((2, 2, blk, blk), F32),
                        pltpu.VMEM((2, LANES, s), BF16),
                        pltpu.VMEM((2, LANES, s), BF16),
                        pltpu.VMEM((2, SUBLANES, s), F32),
                        pltpu.VMEM((2, s, LANES), BF16),
                        pltpu.VMEM((2, nb, blk, blk), F32),
                        pltpu.VMEM((K_SLOTS, PAGES_PER_GROUP, w, LANES), F32),
                        pltpu.SemaphoreType.DMA((K_SLOTS,)),
                        pltpu.VMEM((w, LANES), F32),
                        pltpu.VMEM((nblk, N_HEADS, LANES), F32)])
    return pl.pallas_call(
        functools.partial(_attn_score_kernel, hd=hd, nb=nb, n_pages=n_pages, n_seq=n_seq),
        out_shape=[jax.ShapeDtypeStruct((b, s, w), BF16),
                   jax.ShapeDtypeStruct((n_seq, nblk, N_HEADS, MOBA_BLOCK), F32),
                   jax.ShapeDtypeStruct((n_seq, MOBA_TOPK, N_HEADS, LANES), I32),
                   jax.ShapeDtypeStruct((n_seq, MOBA_TOPK, N_HEADS, LANES), F32)],
        grid_spec=grid_spec,
        compiler_params=_params("arbitrary", "arbitrary"),
        name="attn_score",
    )(pt_flat, rel_bias, q, kt, vt, ga, bkt, q_dec, kcache_t)


def _outproj_kernel(x_ref, mod_ref, ml_ref, ma_ref, wo_ref, gf_ref, y_ref, *, d, w):
    acc = jnp.dot(ml_ref[...].astype(BF16), wo_ref[0:w, :], preferred_element_type=F32)
    acc = acc + jnp.dot(ma_ref[...].astype(BF16), wo_ref[w:2 * w, :], preferred_element_type=F32)
    out = x_ref[...] + mod_ref[:, 2 * d:3 * d] * acc
    ms = jnp.mean(out * out, axis=-1, keepdims=True)
    y_ref[...] = out * lax.rsqrt(ms + RMS_EPS) * gf_ref[...]


def _outproj_prompt(x, mod3, mix_l, mix_a, w_out, g_final):
    b, s, d = x.shape
    w = mix_l.shape[-1]
    tm = OUT_TILE
    return pl.pallas_call(
        functools.partial(_outproj_kernel, d=d, w=w),
        out_shape=jax.ShapeDtypeStruct((b, s, d), F32),
        grid=(b, s // tm),
        in_specs=[pl.BlockSpec((None, tm, d), lambda i, t: (i, t, 0)),
                  pl.BlockSpec((None, 1, 3 * d), lambda i, t: (i, 0, 0)),
                  pl.BlockSpec((None, tm, w), lambda i, t: (i, t, 0)),
                  pl.BlockSpec((None, tm, w), lambda i, t: (i, t, 0)),
                  pl.BlockSpec((2 * w, d), lambda i, t: (0, 0)),
                  pl.BlockSpec((1, d), lambda i, t: (0, 0))],
        out_specs=pl.BlockSpec((None, tm, d), lambda i, t: (i, t, 0)),
        compiler_params=_params("arbitrary", "arbitrary"),
        name="outproj_prompt",
    )(x, mod3, mix_l, mix_a, w_out, g_final)


def _sample_proj_kernel(x_ref, mod_ref, g_ref, win_ref, wqkt_ref, cbuf_ref, h0_ref, cw_ref, cb_ref,
                        wg_ref, bg_ref, ap_ref,
                        qt_ref, kt_ref, k_ref, v_ref, h_ref, cnew_ref, ml_ref, ga_ref, *, d, w):
    n = x_ref.shape[0]
    xm = _modulated_norm(x_ref[...], mod_ref[...], g_ref[...], d).astype(BF16)
    pr = jnp.dot(xm, win_ref[...], preferred_element_type=F32)
    u = pr[:, 0:w]
    k_ref[...] = pr[:, 3 * w:4 * w]
    v_ref[...] = pr[:, 4 * w:5 * w]
    ga_ref[...] = pr[:, 5 * w:6 * w]
    xpad = jnp.concatenate([xm, jnp.zeros((LANES - n, d), BF16)], axis=0)
    pt = lax.dot_general(wqkt_ref[2 * w:4 * w, :], xpad, (((1,), (1,)), ((), ())),
                         preferred_element_type=F32)
    qt_ref[...] = pt[0:w]
    kt_ref[...] = pt[w:2 * w]
    cw = cw_ref[...]
    u_conv = cb_ref[...] + (cbuf_ref[0] * cw[0:1] + cbuf_ref[1] * cw[1:2]
                            + cbuf_ref[2] * cw[2:3] + u * cw[3:4])
    a, bx = _lru_gates(u_conv, wg_ref, bg_ref, ap_ref, w)
    h = a * h0_ref[...] + bx
    h_ref[...] = h
    cnew_ref[0] = cbuf_ref[1]
    cnew_ref[1] = cbuf_ref[2]
    cnew_ref[2] = u
    ml_ref[...] = h * _silu(pr[:, w:2 * w])


def _sample_proj(x, mod, g_norm, w_in_bf, w_in_t, cbuf, h0, conv_w, conv_b, wg, bg, a_param):
    n, d = x.shape
    w = h0.shape[1]
    row = jax.ShapeDtypeStruct((n, w), F32)
    col = jax.ShapeDtypeStruct((w, LANES), F32)
    return pl.pallas_call(
        functools.partial(_sample_proj_kernel, d=d, w=w),
        out_shape=[col, col, row, row, row, jax.ShapeDtypeStruct((CONV_W - 1, n, w), F32), row, row],
        compiler_params=pltpu.CompilerParams(vmem_limit_bytes=VMEM_LIMIT_BYTES),
        name="sample_proj",
    )(x, mod, g_norm, w_in_bf, w_in_t, cbuf, h0, conv_w, conv_b, wg, bg, a_param)


def _decode_kernel(pt_ref, ix_ref, s_ref, idxv_ref, valv_ref, qt_ref, kt_ref, vnew_ref, rbt_ref,
                   bkt_ref, vc_ref, y_ref, vbuf, sem, *, n_pages, n_seq, hd):
    b = pl.program_id(0)
    nsel = MOBA_TOPK
    blk = MOBA_BLOCK
    nb = n_pages // 2
    grp = DEC_SEQS
    n_steps = n_seq // grp

    def tile_copy(step, gi, r, h, pp, slot):
        sq = step * grp + gi
        page = pt_ref[sq * n_pages + 2 * ix_ref[(sq * nsel + r) * N_HEADS + h] + pp]
        return pltpu.make_async_copy(
            vc_ref.at[page, h], vbuf.at[slot, gi, h, :, pl.ds((2 * r + pp) * LANES, LANES)], sem.at[slot])

    def for_tiles(step, fn):
        slot = lax.rem(step, 2)
        for gi in range(grp):
            for r in range(nsel):
                for h in range(N_HEADS):
                    for pp in range(2):
                        fn(tile_copy(step, gi, r, h, pp, slot))

    @pl.when(b == 0)
    def _():
        for_tiles(0, lambda cp: cp.start())

    for_tiles(lax.rem(b + 1, n_steps), lambda cp: cp.start())

    sub = lax.broadcasted_iota(I32, (N_HEADS, blk), 0)
    rbt = rbt_ref[...]
    bias_last = jnp.zeros((N_HEADS, blk), F32)
    for k in range(REL_BUCKETS):
        bias_last = jnp.where(bkt_ref[...] == k, rbt[:, k:k + 1], bias_last)
    bias_far = rbt[:, REL_BUCKETS - 1:REL_BUCKETS]
    lane = lax.broadcasted_iota(I32, (N_HEADS, LANES), 1)
    qk = jnp.sum((qt_ref[...] * kt_ref[...]).reshape(N_HEADS, hd, LANES), axis=1)

    softmax = []
    for gi in range(grp):
        sq = b * grp + gi
        logits = []
        for r in range(nsel):
            lr = jnp.zeros((N_HEADS, blk), F32)
            for h in range(N_HEADS):
                n = ix_ref[(sq * nsel + r) * N_HEADS + h]
                lr = jnp.where(sub == h, s_ref[gi, n], lr)
            idv = idxv_ref[gi, r][:, 0:1]
            bias = jnp.where(idv == nb - 1, bias_last, bias_far)
            valid = valv_ref[gi, r][:, 0:1] > 0.5 * NEG_INF
            logits.append(jnp.where(valid, lr * (hd ** -0.5) + bias, NEG_INF))

        own = jnp.sum(jnp.where(lane == sq, qk, 0.0), axis=1, keepdims=True)
        l_own = own * (hd ** -0.5) + rbt[:, 0:1]
        m = l_own
        for lr in logits:
            m = jnp.maximum(m, jnp.max(lr, axis=1, keepdims=True))
        p_own = jnp.exp(l_own - m)
        den = p_own
        probs = []
        for lr in logits:
            pr = jnp.exp(lr - m)
            den = den + jnp.sum(pr, axis=1, keepdims=True)
            probs.append(pr)
        softmax.append((jnp.concatenate(probs, axis=1), p_own, den))

    for_tiles(b, lambda cp: cp.wait())
    slot = lax.rem(b, 2)
    for gi in range(grp):
        pcat, p_own, den = softmax[gi]
        rows = []
        for h in range(N_HEADS):
            vt = vbuf[slot, gi, h].astype(BF16)
            rows.append(lax.dot_general(pcat[h:h + 1, :].astype(BF16), vt, (((1,), (1,)), ((), ())),
                                        preferred_element_type=F32))
        acc = jnp.concatenate(rows, axis=0)
        y_ref[gi] = (acc + p_own * vnew_ref[gi]) / den

    @pl.when(b == n_steps - 1)
    def _():
        for_tiles(0, lambda cp: cp.wait())


def _decode(pt_flat, idx_flat, s_all, idxv, valv, qt, kt, v_new, rbt, bkt_last, vcache_t, n_seq, n_pages):
    w = qt.shape[0]
    hd = w // N_HEADS
    nb = n_pages // 2
    grp = DEC_SEQS
    assert n_seq % (2 * grp) == 0
    zero4 = lambda b, pt, ix: (b, 0, 0, 0)
    grid_spec = pltpu.PrefetchScalarGridSpec(
        num_scalar_prefetch=2,
        grid=(n_seq // grp,),
        in_specs=[pl.BlockSpec((grp, nb, N_HEADS, MOBA_BLOCK), zero4),
                  pl.BlockSpec((grp, MOBA_TOPK, N_HEADS, LANES), zero4),
                  pl.BlockSpec((grp, MOBA_TOPK, N_HEADS, LANES), zero4),
                  pl.BlockSpec((w, LANES), lambda b, pt, ix: (0, 0)),
                  pl.BlockSpec((w, LANES), lambda b, pt, ix: (0, 0)),
                  pl.BlockSpec((grp, N_HEADS, hd), lambda b, pt, ix: (b, 0, 0)),
                  pl.BlockSpec((N_HEADS, REL_BUCKETS), lambda b, pt, ix: (0, 0)),
                  pl.BlockSpec((1, MOBA_BLOCK), lambda b, pt, ix: (0, 0)),
                  pl.BlockSpec(memory_space=pl.ANY)],
        out_specs=pl.BlockSpec((grp, N_HEADS, hd), lambda b, pt, ix: (b, 0, 0)),
        scratch_shapes=[pltpu.VMEM((2, grp, N_HEADS, hd, MOBA_TOPK * MOBA_BLOCK), F32),
                        pltpu.SemaphoreType.DMA((2,))])
    return pl.pallas_call(
        functools.partial(_decode_kernel, n_pages=n_pages, n_seq=n_seq, hd=hd),
        out_shape=jax.ShapeDtypeStruct((n_seq, N_HEADS, hd), F32),
        grid_spec=grid_spec,
        compiler_params=_params("arbitrary"),
        name="decode_attn",
    )(pt_flat, idx_flat, s_all, idxv, valv, qt, kt, v_new, rbt, bkt_last, vcache_t)


def _outproj_sample_kernel(x_ref, mod_ref, ml_ref, ya_ref, ga_ref, wo_ref, gf_ref, y_ref, *, d, w):
    ma = ya_ref[...] * _silu(ga_ref[...])
    acc = jnp.dot(ml_ref[...].astype(BF16), wo_ref[0:w, :], preferred_element_type=F32)
    acc = acc + jnp.dot(ma.astype(BF16), wo_ref[w:2 * w, :], preferred_element_type=F32)
    out = x_ref[...] + mod_ref[:, 2 * d:3 * d] * acc
    ms = jnp.mean(out * out, axis=-1, keepdims=True)
    y_ref[...] = out * lax.rsqrt(ms + RMS_EPS) * gf_ref[...]


def _outproj_sample(x, mod, mix_l, y_att, ga, w_out, g_final):
    n, d = x.shape
    w = mix_l.shape[1]
    return pl.pallas_call(
        functools.partial(_outproj_sample_kernel, d=d, w=w),
        out_shape=jax.ShapeDtypeStruct((n, d), F32),
        compiler_params=pltpu.CompilerParams(vmem_limit_bytes=VMEM_LIMIT_BYTES),
        name="outproj_sample",
    )(x, mod, mix_l, y_att, ga, w_out, g_final)


def _block_diag(wb):
    n, c, dd = wb.shape
    return jnp.einsum("ncd,nm->ncmd", wb, jnp.eye(n, dtype=wb.dtype)).reshape(n * c, n * dd)


def kernel(x_prompt, x_sample, cache_k, cache_v, state_lru_h, state_lru_conv, page_table, c_prompt, c_sample, w_ada, b_ada, g_norm, w_in, conv_w, conv_b, w_rgate, b_rgate, w_igate, b_igate, lru_a_param, rel_bias, w_out, g_final):
    bp, s, d = x_prompt.shape
    ns = x_sample.shape[0]
    w = state_lru_h.shape[1]
    hd = w // N_HEADS
    n_pages = page_table.shape[1]
    page = cache_k.shape[1]
    past = n_pages * page
    assert x_sample.shape[1] == 1 and 2 * page == MOBA_BLOCK and past % MOBA_BLOCK == 0
    assert s % MOBA_BLOCK == 0 and s % ROW_TILE == 0 and s % OUT_TILE == 0 and s % LRU_CHUNK == 0
    assert w % LANES == 0 and ns <= LANES
    assert MOBA_BLOCK + 1 >= REL_MAX_DIST

    w_in_bf, w_in_t = _wprep(w_in, w)
    w_out_bf = w_out.astype(BF16)
    wg = jnp.concatenate([_block_diag(w_rgate), _block_diag(w_igate)], axis=1).astype(BF16)
    bg = jnp.concatenate([b_rgate, b_igate]).reshape(1, 2 * w)
    g_norm2 = g_norm.reshape(1, d)
    g_final2 = g_final.reshape(1, d)
    conv_b2 = conv_b.reshape(1, w)
    a_param2 = lru_a_param.reshape(1, w)

    mod = _adaln(jnp.concatenate([c_prompt, c_sample], axis=0), w_ada, b_ada)
    mod_p = mod[0:bp].reshape(bp, 1, 3 * d)
    mod_s = mod[bp:bp + ns]

    xs = x_sample.reshape(ns, d)
    cbuf = state_lru_conv.transpose(1, 0, 2)
    qt, ktn, k_s, v_s, h_s, cnew, mix_ls, ga_s = _sample_proj(
        xs, mod_s, g_norm2, w_in_bf, w_in_t, cbuf, state_lru_h, conv_w, conv_b2, wg, bg, a_param2)
    pt_flat = page_table.reshape(-1)
    n_phys = cache_k.shape[0]
    kc_t = cache_k.transpose(0, 2, 3, 1).reshape(n_phys, w, page)
    vc_t = cache_v.transpose(0, 2, 3, 1)

    u, gl, q, ga, kt, vt, ktb, vtb = _inproj(x_prompt, mod_p, g_norm2, w_in_bf, w_in_t)
    mix_l, h_last, tail = _lru_prompt(u, gl, conv_w, conv_b2, wg, bg, a_param2)
    mix_a, s_all, idxv, valv = _attn_and_score(q, ktb, vtb, ga, rel_bias, pt_flat, qt, kc_t, ns, n_pages)
    y_prompt = _outproj_prompt(x_prompt, mod_p, mix_l, mix_a, w_out_bf, g_final2)
    k_prompt = kt.reshape(bp, N_HEADS, hd, s).transpose(0, 3, 1, 2)
    v_prompt = vt.reshape(bp, N_HEADS, hd, s).transpose(0, 3, 1, 2)
    lru_h_prompt = h_last[:, 0, :]
    lru_conv_prompt = tail[:, SUBLANES - (CONV_W - 1):, :]

    idx_flat = idxv[:, :, :, 0].reshape(-1)
    nb = n_pages // 2
    pos = (nb - 1) * MOBA_BLOCK + jnp.arange(MOBA_BLOCK, dtype=I32)
    bkt_last = _t5_bucket(past - pos).reshape(1, MOBA_BLOCK)
    y_att = _decode(pt_flat, idx_flat, s_all, idxv, valv, qt, ktn, v_s.reshape(ns, N_HEADS, hd),
                    rel_bias.T, bkt_last, vc_t, ns, n_pages)
    y_s = _outproj_sample(xs, mod_s, mix_ls, y_att.reshape(ns, w), ga_s, w_out_bf, g_final2)

    return (y_prompt, y_s.reshape(ns, 1, d), k_prompt, v_prompt, lru_h_prompt, lru_conv_prompt,
            k_s.reshape(ns, 1, N_HEADS, hd), v_s.reshape(ns, 1, N_HEADS, hd), h_s,
            cnew.transpose(1, 0, 2))
```

```python
import functools
import math

import jax
import jax.numpy as jnp
from jax import lax
from jax.experimental import pallas as pl
from jax.experimental.pallas import tpu as pltpu

F32 = jnp.float32
BF16 = jnp.bfloat16
I32 = jnp.int32

N_HEADS = 8
LRU_BLOCKS = 8
CONV_W = 4
LRU_C = 8.0
MOBA_BLOCK = 256
MOBA_TOPK = 3
REL_BUCKETS = 32
REL_MAX_DIST = 128
RMS_EPS = 1e-6
NEG_INF = -1e30
HIGHEST = lax.Precision.HIGHEST
LOG2E = math.log2(math.e)

LANES = 128
SUBLANES = 8
VMEM_LIMIT_BYTES = 56 * 1024 * 1024

ROW_TILE = 1024
LRU_CHUNK = 512
PAGES_PER_GROUP = 64
K_SLOTS = 2
SCORE_PAGES = 16
SHARE_PAGES = 4
MASK_ROWS = 16
DEC_SEQS = 8


def _sigmoid(x):
    return 0.5 * jnp.tanh(0.5 * x) + 0.5


def _silu(x):
    return x * _sigmoid(x)


def _softplus(z):
    return jnp.maximum(z, 0.0) + jnp.log1p(jnp.exp(-jnp.abs(z)))


def _t5_bucket(dist):
    n = jnp.maximum(dist, 0)
    max_exact = REL_BUCKETS // 2
    nf = jnp.maximum(n, 1).astype(F32)
    pos = jnp.log(nf / max_exact) / math.log(REL_MAX_DIST / max_exact) * (REL_BUCKETS - max_exact)
    large = max_exact + jnp.where(pos >= 0, jnp.floor(pos), jnp.ceil(pos)).astype(I32)
    large = jnp.minimum(large, REL_BUCKETS - 1)
    return jnp.where(n < max_exact, n, large)


def _params(*sem):
    return pltpu.CompilerParams(dimension_semantics=sem, vmem_limit_bytes=VMEM_LIMIT_BYTES)


def _adaln_kernel(c_ref, w_ref, b_ref, o_ref):
    s = _silu(c_ref[...]).astype(BF16)
    o_ref[...] = jnp.dot(s, w_ref[...].astype(BF16), preferred_element_type=F32) + b_ref[...]


def _adaln(c, w_ada, b_ada):
    n, d = c.shape
    return pl.pallas_call(
        _adaln_kernel,
        out_shape=jax.ShapeDtypeStruct((n, 3 * d), F32),
        grid=(3,),
        in_specs=[pl.BlockSpec((n, d), lambda j: (0, 0)),
                  pl.BlockSpec((d, d), lambda j: (0, j)),
                  pl.BlockSpec((1, d), lambda j: (0, j))],
        out_specs=pl.BlockSpec((n, d), lambda j: (0, j)),
        compiler_params=_params("arbitrary"),
        name="adaln",
    )(c, w_ada, b_ada.reshape(1, 3 * d))


def _wprep_kernel(w_ref, wb_ref, wt_ref):
    x = w_ref[...]
    wb_ref[...] = x.astype(BF16)
    wt_ref[...] = x.T.astype(BF16)


def _wprep(w_in, cols):
    d, n = w_in.shape
    return pl.pallas_call(
        _wprep_kernel,
        out_shape=[jax.ShapeDtypeStruct((d, n), BF16), jax.ShapeDtypeStruct((n, d), BF16)],
        grid=(n // cols,),
        in_specs=[pl.BlockSpec((d, cols), lambda j: (0, j))],
        out_specs=[pl.BlockSpec((d, cols), lambda j: (0, j)), pl.BlockSpec((cols, d), lambda j: (j, 0))],
        compiler_params=_params("arbitrary"),
        name="wprep",
    )(w_in)


def _modulated_norm(x, mod, g, d):
    ms = jnp.mean(x * x, axis=-1, keepdims=True)
    xn = x * lax.rsqrt(ms + RMS_EPS) * g
    return xn * (1.0 + mod[:, d:2 * d]) + mod[:, 0:d]


def _inproj_kernel(x_ref, mod_ref, g_ref, win_ref, wkt_ref, wvt_ref,
                   u_ref, gl_ref, q_ref, ga_ref, kt_ref, vt_ref, ktb_ref, vtb_ref, *, d, w):
    xm = _modulated_norm(x_ref[...], mod_ref[...], g_ref[...], d).astype(BF16)
    pr = jnp.dot(xm, win_ref[:, 0:3 * w], preferred_element_type=F32)
    u_ref[...] = pr[:, 0:w]
    gl_ref[...] = pr[:, w:2 * w]
    q_ref[...] = (pr[:, 2 * w:3 * w] * ((w // N_HEADS) ** -0.5 * LOG2E)).astype(BF16)
    ga_ref[...] = jnp.dot(xm, win_ref[:, 5 * w:6 * w], preferred_element_type=F32)
    nt = (((1,), (1,)), ((), ()))
    kt = lax.dot_general(wkt_ref[...], xm, nt, preferred_element_type=F32)
    vt = lax.dot_general(wvt_ref[...], xm, nt, preferred_element_type=F32)
    kt_ref[...] = kt
    vt_ref[...] = vt
    ktb_ref[...] = kt.astype(BF16)
    vtb_ref[...] = vt.astype(BF16)


def _inproj(x, mod3, g_norm, w_in_bf, w_in_t):
    b, s, d = x.shape
    w = w_in_bf.shape[1] // 6
    tm = ROW_TILE
    row = pl.BlockSpec((None, tm, w), lambda i, t: (i, t, 0))
    col = pl.BlockSpec((None, w, tm), lambda i, t: (i, 0, t))
    return pl.pallas_call(
        functools.partial(_inproj_kernel, d=d, w=w),
        out_shape=[jax.ShapeDtypeStruct((b, s, w), F32)] * 2 + [jax.ShapeDtypeStruct((b, s, w), BF16)]
        + [jax.ShapeDtypeStruct((b, s, w), F32)] + [jax.ShapeDtypeStruct((b, w, s), F32)] * 2
        + [jax.ShapeDtypeStruct((b, w, s), BF16)] * 2,
        grid=(b, s // tm),
        in_specs=[pl.BlockSpec((None, tm, d), lambda i, t: (i, t, 0)),
                  pl.BlockSpec((None, 1, 3 * d), lambda i, t: (i, 0, 0)),
                  pl.BlockSpec((1, d), lambda i, t: (0, 0)),
                  pl.BlockSpec((d, 6 * w), lambda i, t: (0, 0)),
                  pl.BlockSpec((w, d), lambda i, t: (3, 0)),
                  pl.BlockSpec((w, d), lambda i, t: (4, 0))],
        out_specs=[row, row, row, row, col, col, col, col],
        compiler_params=_params("arbitrary", "arbitrary"),
        name="inproj",
    )(x, mod3, g_norm, w_in_bf, w_in_t, w_in_t)


def _lru_gates(u_conv, wg_ref, bg_ref, ap_ref, w):
    g2 = jnp.dot(u_conv.astype(BF16), wg_ref[...], preferred_element_type=F32) + bg_ref[...]
    r = _sigmoid(g2[:, 0:w])
    i = _sigmoid(g2[:, w:2 * w])
    log_a = (-LRU_C * r) * _softplus(-ap_ref[...])
    a = jnp.exp(log_a)
    z = -jnp.tanh(log_a) * (a * a + 1.0)
    root = jnp.where(z > 0.0, z * lax.rsqrt(z), 0.0)
    bx = root * (i * u_conv)
    return a, bx


def _lru_out_kernel(u_ref, gl_ref, cw_ref, cb_ref, wg_ref, bg_ref, ap_ref,
                    x_ref, mod_ref, ma_ref, wo_ref, gf_ref,
                    y_ref, hl_ref, tail_ref, a_scr, b_scr, hs_scr, h_scr, tail_scr, *, tc, w, d):
    c = pl.program_id(1)

    @pl.when(c == 0)
    def _():
        h_scr[...] = jnp.zeros_like(h_scr)
        tail_scr[...] = jnp.zeros_like(tail_scr)

    u = u_ref[...]
    cw = cw_ref[...]
    tail = tail_scr[...]
    u1, u2, u3 = (tail[SUBLANES - k:SUBLANES - k + 1] for k in (1, 2, 3))

    def shift1(z, first):
        return jnp.concatenate([first, z[0:tc - 1]], axis=0)

    z = shift1(u * cw[0:1], u1 * cw[0:1])
    z = shift1(u * cw[1:2] + z, u1 * cw[1:2] + u2 * cw[0:1])
    z = shift1(u * cw[2:3] + z, u1 * cw[2:3] + u2 * cw[1:2] + u3 * cw[0:1])
    u_conv = cb_ref[...] + (u * cw[3:4] + z)
    tail_scr[...] = u[tc - SUBLANES:tc]
    a, bx = _lru_gates(u_conv, wg_ref, bg_ref, ap_ref, w)
    a_scr[...] = a
    b_scr[...] = bx

    row = lax.broadcasted_iota(I32, (SUBLANES, w), 0)

    def tile(t, h):
        r0 = pl.multiple_of(t * SUBLANES, SUBLANES)
        at = a_scr[pl.ds(r0, SUBLANES), :]
        bt = b_scr[pl.ds(r0, SUBLANES), :]
        for dd in (1, 2, 4):
            keep = row >= dd
            bt = jnp.where(keep, at * pltpu.roll(bt, dd, 0) + bt, bt)
            at = jnp.where(keep, at * pltpu.roll(at, dd, 0), at)
        hs = at * h + bt
        hs_scr[pl.ds(r0, SUBLANES), :] = hs
        return jnp.broadcast_to(hs[SUBLANES - 1:SUBLANES, :], (SUBLANES, w))

    h = lax.fori_loop(0, tc // SUBLANES, tile, h_scr[...], unroll=4)
    h_scr[...] = h
    hl_ref[...] = h
    tail_ref[...] = u[tc - SUBLANES:tc]
    mix_l = (hs_scr[...] * _silu(gl_ref[...])).astype(BF16)
    acc = jnp.dot(mix_l, wo_ref[0:w, :], preferred_element_type=F32)
    acc = acc + jnp.dot(ma_ref[...], wo_ref[w:2 * w, :], preferred_element_type=F32)
    out = x_ref[...] + mod_ref[:, 2 * d:3 * d] * acc
    ms = jnp.mean(out * out, axis=-1, keepdims=True)
    y_ref[...] = out * lax.rsqrt(ms + RMS_EPS) * gf_ref[...]


def _lru_outproj(u, gl, conv_w, conv_b, wg, bg, a_param, x, mod3, mix_a, w_out, g_final):
    b, s, w = u.shape
    d = x.shape[-1]
    tc = LRU_CHUNK
    const = lambda shape: pl.BlockSpec(shape, lambda i, t: (0,) * len(shape))
    return pl.pallas_call(
        functools.partial(_lru_out_kernel, tc=tc, w=w, d=d),
        out_shape=[jax.ShapeDtypeStruct((b, s, d), F32),
                   jax.ShapeDtypeStruct((b, SUBLANES, w), F32),
                   jax.ShapeDtypeStruct((b, SUBLANES, w), F32)],
        grid=(b, s // tc),
        in_specs=[pl.BlockSpec((None, tc, w), lambda i, t: (i, t, 0)),
                  pl.BlockSpec((None, tc, w), lambda i, t: (i, t, 0)),
                  const((CONV_W, w)), const((1, w)), const((w, 2 * w)), const((1, 2 * w)),
                  const((1, w)),
                  pl.BlockSpec((None, tc, d), lambda i, t: (i, t, 0)),
                  pl.BlockSpec((None, 1, 3 * d), lambda i, t: (i, 0, 0)),
                  pl.BlockSpec((None, tc, w), lambda i, t: (i, t, 0)),
                  const((2 * w, d)), const((1, d))],
        out_specs=[pl.BlockSpec((None, tc, d), lambda i, t: (i, t, 0)),
                   pl.BlockSpec((None, SUBLANES, w), lambda i, t: (i, 0, 0)),
                   pl.BlockSpec((None, SUBLANES, w), lambda i, t: (i, 0, 0))],
        scratch_shapes=[pltpu.VMEM((tc, w), F32), pltpu.VMEM((tc, w), F32),
                        pltpu.VMEM((tc, w), F32), pltpu.VMEM((SUBLANES, w), F32),
                        pltpu.VMEM((SUBLANES, w), F32)],
        compiler_params=_params("arbitrary", "arbitrary"),
        name="lru_outproj",
    )(u, gl, conv_w, conv_b, wg, bg, a_param, x, mod3, mix_a, w_out, g_final)


_NT = (((1,), (1,)), ((), ()))


def _attn_score_kernel(pt_ref, rb_ref, q_ref, kt_ref, vt_ref, ga_ref, bkt_ref, qt_ref, kc_ref,
                       o_ref, s_ref, idx_ref, val_ref,
                       bias_scr, kaug_scr, vb_scr, mpad_scr, qm_scr, lg_scr, kbuf, sem, qb_scr, g_scr,
                       *, hd, nb, n_pages, n_seq):
    p = pl.program_id(0)
    b = pl.program_id(1)
    blk = MOBA_BLOCK
    s = nb * blk
    pg = PAGES_PER_GROUP
    gps = n_pages // pg
    total = n_seq * gps
    seq = p * pl.num_programs(1) + b

    def page_copy(g, jj, slot):
        return pltpu.make_async_copy(kc_ref.at[pt_ref[g * pg + jj]], kbuf.at[slot, jj], sem.at[slot])

    def start_group(g):
        slot = lax.rem(g, K_SLOTS)
        for jj in range(pg):
            page_copy(g, jj, slot).start()

    def wait_group(g):
        slot = lax.rem(g, K_SLOTS)
        for jj in range(pg):
            page_copy(g, jj, slot).wait()

    @pl.when(seq == 0)
    def _():
        for g in range(K_SLOTS):
            start_group(g)

    lane_t = lax.broadcasted_iota(I32, qt_ref.shape, 1)
    qcol = jnp.sum(jnp.where(lane_t == seq, qt_ref[...], 0.0), axis=1, keepdims=True)
    qb_scr[...] = jnp.broadcast_to(qcol, qb_scr.shape)

    @pl.when(b == 0)
    def _():
        qi = lax.broadcasted_iota(I32, (blk, blk), 0)
        ki = lax.broadcasted_iota(I32, (blk, blk), 1)
        for hh in range(2):
            h = 2 * p + hh
            far = rb_ref[REL_BUCKETS - 1, h]
            for which in range(2):
                bk = bkt_ref[which]
                tab = jnp.zeros((blk, blk), F32)
                for k in range(REL_BUCKETS):
                    tab = jnp.where(bk == k, (rb_ref[k, h] - far) * LOG2E, tab)
                if which == 0:
                    tab = jnp.where(qi >= ki, tab, NEG_INF)
                bias_scr[hh, which] = tab
        i16 = lax.broadcasted_iota(I32, (MASK_ROWS, s), 0)
        ind16 = jnp.where(i16 == lax.broadcasted_iota(I32, (MASK_ROWS, s), 1) // blk, 1.0, 0.0).astype(BF16)
        for hh in range(2):
            o0 = hd * (1 - hh)
            kaug_scr[hh, o0:o0 + MASK_ROWS, :] = ind16
            kaug_scr[hh, o0 + MASK_ROWS:o0 + hd, :] = jnp.zeros((hd - MASK_ROWS, s), BF16)
            vb_scr[hh, o0:o0 + hd, :] = jnp.ones((hd, s), BF16)

    def prepare():
        n_i = lax.broadcasted_iota(I32, (SUBLANES, s), 0)
        j_i = lax.broadcasted_iota(I32, (SUBLANES, s), 1) // blk
        ind = jnp.where(n_i == j_i, 1.0, 0.0).astype(BF16)
        ktb = kt_ref[...]
        km8 = lax.dot_general(ind, ktb, _NT, preferred_element_type=F32) * (1.0 / blk)
        vtb = vt_ref[...]
        for hh in range(2):
            own = slice(hd * hh, hd * (hh + 1))
            kaug_scr[hh, own, :] = ktb[own]
            vb_scr[hh, own, :] = vtb[own]
        qsb = q_ref[...]
        lane_q = lax.broadcasted_iota(I32, (1, LANES), 1)

        for hh in range(2):
            head_lanes = jnp.where((lane_q >= hd * hh) & (lane_q < hd * (hh + 1)), 1.0, 0.0).astype(BF16)
            qm = qsb * head_lanes
            qm_scr[hh] = qm
            c0 = min(MOBA_TOPK + 1, nb) * blk
            for q0, q1, ranked in ((0, c0, False), (c0, s, True)):
                if q1 == q0:
                    continue
                gt = lax.dot_general(km8.astype(BF16), qm[q0:q1], _NT, preferred_element_type=F32)
                n_p = lax.broadcasted_iota(I32, gt.shape, 0)
                j_p = (lax.broadcasted_iota(I32, gt.shape, 1) + q0) // blk
                past = n_p < j_p
                g = jnp.where(past, gt, NEG_INF)
                keep = past & (g > 0.5 * NEG_INF * (hd ** -0.5 * LOG2E))
                if ranked:
                    cnt = jnp.zeros(gt.shape, I32)
                    for m in range(nb - 1):
                        gm = g[m:m + 1, :]
                        cnt = cnt + jnp.where((gm > g) | ((gm == g) & (n_p > m)), 1, 0)
                    keep = keep & (cnt < MOBA_TOPK)
                mpad_scr[hh, :, q0:q1] = jnp.where(keep | (n_p >= j_p), 0.0, NEG_INF)

    lane = lax.broadcasted_iota(I32, (blk, LANES), 1)

    def attend(j, hh, par):
        rows = slice(j * blk, (j + 1) * blk)
        qa = qm_scr[hh, rows, :]
        if j > 0:
            o0 = hd * (1 - hh)
            pieces = [mpad_scr[hh, :, rows], jnp.zeros((LANES - o0 - SUBLANES, blk), F32)]
            if o0:
                pieces.insert(0, jnp.zeros((o0, blk), F32))
            slab = jnp.concatenate(pieces, axis=0)
            qa = qa + slab.T.astype(BF16)
        macc = None
        for n in range(j + 1):
            sc = jnp.dot(qa, kaug_scr[hh, :, n * blk:(n + 1) * blk],
                         preferred_element_type=F32)
            if n == j:
                sc = sc + bias_scr[hh, 0]
            elif n == j - 1:
                sc = sc + bias_scr[hh, 1]
            lg_scr[par, n] = sc
            mx = jnp.maximum(sc[:, 0:LANES], sc[:, LANES:2 * LANES])
            macc = mx if macc is None else jnp.maximum(macc, mx)
        mrow = jnp.max(macc, axis=1, keepdims=True)
        acc = None
        for n in range(j + 1):
            pn = jnp.exp2(lg_scr[par, n] - mrow).astype(BF16)
            pv = lax.dot_general(pn, vb_scr[hh, :, n * blk:(n + 1) * blk], _NT,
                                 preferred_element_type=F32)
            acc = pv if acc is None else acc + pv
        return acc

    def finish(j, accs):
        rows = slice(j * blk, (j + 1) * blk)
        first = lane < hd
        num = jnp.where(first, accs[0], accs[1])
        den = pltpu.roll(jnp.where(first, accs[1], accs[0]), hd, 1)
        o_ref[rows, :] = (num / den * _silu(ga_ref[rows, :])).astype(o_ref.dtype)

    sub8 = lax.broadcasted_iota(I32, (SUBLANES, LANES), 0)
    tiles = hd // SUBLANES

    def head_sums(parts):
        assert len(parts) == SUBLANES
        for k in (4, 2, 1):
            low = (sub8 % (2 * k)) < k
            nxt = []
            for a in range(k):
                first, second = parts[a], parts[a + k]
                lo = jnp.where(low, first, second)
                hi = jnp.where(low, second, first)
                if k == SUBLANES // 2:
                    moved = pltpu.roll(hi, k, 0)
                else:
                    moved = jnp.where(low, pltpu.roll(hi, SUBLANES - k, 0), pltpu.roll(hi, k, 0))
                nxt.append(lo + moved)
            parts = nxt
        return parts[0]

    def score_group(sg, slot):
        def trip(t, carry):
            for qq in range(SCORE_PAGES // SHARE_PAGES):
                p0 = t * SCORE_PAGES + qq * SHARE_PAGES
                pages = [kbuf.at[slot, p0 + pp] for pp in range(SHARE_PAGES)]
                part = [[] for _ in range(SHARE_PAGES)]
                for h in range(N_HEADS):
                    qh = [qb_scr[(h * tiles + i) * SUBLANES:(h * tiles + i + 1) * SUBLANES, :]
                          for i in range(tiles)]
                    for pp in range(SHARE_PAGES):
                        acc = None
                        for i in range(tiles):
                            r0 = (h * tiles + i) * SUBLANES
                            term = pages[pp][r0:r0 + SUBLANES, :] * qh[i]
                            acc = term if acc is None else acc + term
                        part[pp].append(acc)
                logits = [head_sums(part[pp]) for pp in range(SHARE_PAGES)]
                for bb in range(SHARE_PAGES // 2):
                    n = sg * (pg // 2) + p0 // 2 + bb
                    s_ref[n, :, 0:LANES] = logits[2 * bb]
                    s_ref[n, :, LANES:2 * LANES] = logits[2 * bb + 1]
                    bsum = jnp.sum(logits[2 * bb] + logits[2 * bb + 1], axis=1, keepdims=True)
                    g_scr[n] = jnp.broadcast_to(bsum, (N_HEADS, LANES))
            return carry

        lax.fori_loop(0, pg // SCORE_PAGES, trip, 0)

    items = [(0, hh) for hh in range(2)] + [(j, hh) for j in reversed(range(1, nb)) for hh in range(2)]
    work = sum(j + 1 for j, _ in items)
    segments = [[] for _ in range(gps)]
    done = 0
    for item in items:
        segments[min(gps - 1, done * gps // work)].append(item)
        done += item[0] + 1

    accs = {}
    count = 0
    for sg in range(gps):
        g = seq * gps + sg
        wait_group(g)
        score_group(sg, lax.rem(g, K_SLOTS))

        start_group(lax.rem(g + K_SLOTS, total))

        if sg == 0:
            prepare()
        for j, hh in segments[sg]:
            accs.setdefault(j, []).append(attend(j, hh, count % 2))
            count += 1
            if len(accs[j]) == 2:
                finish(j, accs.pop(j))

    @pl.when(seq == n_seq - 1)
    def _():
        for g in range(K_SLOTS):
            wait_group(g)

    gate = g_scr[...] * (1.0 / MOBA_BLOCK)
    nidx = lax.broadcasted_iota(I32, gate.shape, 0)
    for r in range(MOBA_TOPK):
        mx = jnp.max(gate, axis=0)
        am = jnp.min(jnp.where(gate == mx[None], nidx, gate.shape[0]), axis=0)
        idx_ref[r] = am
        val_ref[r] = mx
        gate = jnp.where(nidx == am[None], -jnp.inf, gate)


def _attn_and_score(q, kt, vt, ga, rel_bias, pt_flat, q_dec, kcache_t, n_seq, n_pages):
    b, s, w = q.shape
    hd = w // N_HEADS
    blk = MOBA_BLOCK
    nb = s // blk
    npair = w // LANES
    nblk = n_pages // 2
    assert nb <= SUBLANES and LANES == 2 * hd
    assert n_seq == npair * b and n_pages % PAGES_PER_GROUP == 0
    assert (n_seq * (n_pages // PAGES_PER_GROUP)) % K_SLOTS == 0
    dq = jnp.arange(blk, dtype=I32)
    d_own = dq[:, None] - dq[None, :]
    bkt = jnp.stack([_t5_bucket(d_own), _t5_bucket(d_own + blk)])
    per_seq = lambda p, i, pt: (p * b + i, 0, 0, 0)
    grid_spec = pltpu.PrefetchScalarGridSpec(
        num_scalar_prefetch=1,
        grid=(npair, b),
        in_specs=[pl.BlockSpec(memory_space=pltpu.SMEM),
                  pl.BlockSpec((None, s, LANES), lambda p, i, pt: (i, 0, p)),
                  pl.BlockSpec((None, LANES, s), lambda p, i, pt: (i, p, 0)),
                  pl.BlockSpec((None, LANES, s), lambda p, i, pt: (i, p, 0)),
                  pl.BlockSpec((None, s, LANES), lambda p, i, pt: (i, 0, p)),
                  pl.BlockSpec((2, blk, blk), lambda p, i, pt: (0, 0, 0)),
                  pl.BlockSpec((w, LANES), lambda p, i, pt: (0, 0)),
                  pl.BlockSpec(memory_space=pl.ANY)],
        out_specs=[pl.BlockSpec((None, s, LANES), lambda p, i, pt: (i, 0, p)),
                   pl.BlockSpec((None, nblk, N_HEADS, MOBA_BLOCK), per_seq),
                   pl.BlockSpec((None, MOBA_TOPK, N_HEADS, LANES), per_seq),
                   pl.BlockSpec((None, MOBA_TOPK, N_HEADS, LANES), per_seq)],
        scratch_shapes=[pltpu.VMEM((2, 2, blk, blk), F32),
                        pltpu.VMEM((2, LANES, s), BF16),
                        pltpu.VMEM((2, LANES, s), BF16),
                        pltpu.VMEM((2, SUBLANES, s), F32),
                        pltpu.VMEM((2, s, LANES), BF16),
                        pltpu.VMEM((2, nb, blk, blk), F32),
                        pltpu.VMEM((K_SLOTS, PAGES_PER_GROUP, w, LANES), F32),
                        pltpu.SemaphoreType.DMA((K_SLOTS,)),
                        pltpu.VMEM((w, LANES), F32),
                        pltpu.VMEM((nblk, N_HEADS, LANES), F32)])
    return pl.pallas_call(
        functools.partial(_attn_score_kernel, hd=hd, nb=nb, n_pages=n_pages, n_seq=n_seq),
        out_shape=[jax.ShapeDtypeStruct((b, s, w), BF16),
                   jax.ShapeDtypeStruct((n_seq, nblk, N_HEADS, MOBA_BLOCK), F32),
                   jax.ShapeDtypeStruct((n_seq, MOBA_TOPK, N_HEADS, LANES), I32),
                   jax.ShapeDtypeStruct((n_seq, MOBA_TOPK, N_HEADS, LANES), F32)],
        grid_spec=grid_spec,
        compiler_params=_params("arbitrary", "arbitrary"),
        name="attn_score",
    )(pt_flat, rel_bias, q, kt, vt, ga, bkt, q_dec, kcache_t)


def _sample_proj_kernel(x_ref, mod_ref, g_ref, win_ref, wqkt_ref, cbuf_ref, h0_ref, cw_ref, cb_ref,
                        wg_ref, bg_ref, ap_ref,
                        qt_ref, kt_ref, k_ref, v_ref, h_ref, cnew_ref, ml_ref, ga_ref, *, d, w):
    n = x_ref.shape[0]
    xm = _modulated_norm(x_ref[...], mod_ref[...], g_ref[...], d).astype(BF16)
    pr = jnp.dot(xm, win_ref[...], preferred_element_type=F32)
    u = pr[:, 0:w]
    k_ref[...] = pr[:, 3 * w:4 * w]
    v_ref[...] = pr[:, 4 * w:5 * w]
    ga_ref[...] = pr[:, 5 * w:6 * w]
    xpad = jnp.concatenate([xm, jnp.zeros((LANES - n, d), BF16)], axis=0)
    pt = lax.dot_general(wqkt_ref[2 * w:4 * w, :], xpad, (((1,), (1,)), ((), ())),
                         preferred_element_type=F32)
    qt_ref[...] = pt[0:w]
    kt_ref[...] = pt[w:2 * w]
    cw = cw_ref[...]
    u_conv = cb_ref[...] + (cbuf_ref[0] * cw[0:1] + cbuf_ref[1] * cw[1:2]
                            + cbuf_ref[2] * cw[2:3] + u * cw[3:4])
    a, bx = _lru_gates(u_conv, wg_ref, bg_ref, ap_ref, w)
    h = a * h0_ref[...] + bx
    h_ref[...] = h
    cnew_ref[0] = cbuf_ref[1]
    cnew_ref[1] = cbuf_ref[2]
    cnew_ref[2] = u
    ml_ref[...] = h * _silu(pr[:, w:2 * w])


def _sample_proj(x, mod, g_norm, w_in_bf, w_in_t, cbuf, h0, conv_w, conv_b, wg, bg, a_param):
    n, d = x.shape
    w = h0.shape[1]
    row = jax.ShapeDtypeStruct((n, w), F32)
    col = jax.ShapeDtypeStruct((w, LANES), F32)
    return pl.pallas_call(
        functools.partial(_sample_proj_kernel, d=d, w=w),
        out_shape=[col, col, row, row, row, jax.ShapeDtypeStruct((CONV_W - 1, n, w), F32), row, row],
        compiler_params=pltpu.CompilerParams(vmem_limit_bytes=VMEM_LIMIT_BYTES),
        name="sample_proj",
    )(x, mod, g_norm, w_in_bf, w_in_t, cbuf, h0, conv_w, conv_b, wg, bg, a_param)


def _decode_kernel(pt_ref, ix_ref, s_ref, idxv_ref, valv_ref, qt_ref, kt_ref, vnew_ref, rbt_ref,
                   bkt_ref, vc_ref, y_ref, vbuf, sem, *, n_pages, n_seq, hd):
    b = pl.program_id(0)
    nsel = MOBA_TOPK
    blk = MOBA_BLOCK
    nb = n_pages // 2
    grp = DEC_SEQS
    n_steps = n_seq // grp

    def tile_copy(step, gi, r, h, pp, slot):
        sq = step * grp + gi
        page = pt_ref[sq * n_pages + 2 * ix_ref[(sq * nsel + r) * N_HEADS + h] + pp]
        return pltpu.make_async_copy(
            vc_ref.at[page, h], vbuf.at[slot, gi, h, :, pl.ds((2 * r + pp) * LANES, LANES)], sem.at[slot])

    def for_tiles(step, fn):
        slot = lax.rem(step, 2)
        for gi in range(grp):
            for r in range(nsel):
                for h in range(N_HEADS):
                    for pp in range(2):
                        fn(tile_copy(step, gi, r, h, pp, slot))

    @pl.when(b == 0)
    def _():
        for_tiles(0, lambda cp: cp.start())

    for_tiles(lax.rem(b + 1, n_steps), lambda cp: cp.start())

    sub = lax.broadcasted_iota(I32, (N_HEADS, blk), 0)
    rbt = rbt_ref[...]
    bias_last = jnp.zeros((N_HEADS, blk), F32)
    for k in range(REL_BUCKETS):
        bias_last = jnp.where(bkt_ref[...] == k, rbt[:, k:k + 1], bias_last)
    bias_far = rbt[:, REL_BUCKETS - 1:REL_BUCKETS]
    lane = lax.broadcasted_iota(I32, (N_HEADS, LANES), 1)
    qk = jnp.sum((qt_ref[...] * kt_ref[...]).reshape(N_HEADS, hd, LANES), axis=1)

    softmax = []
    for gi in range(grp):
        sq = b * grp + gi
        logits = []
        for r in range(nsel):
            lr = jnp.zeros((N_HEADS, blk), F32)
            for h in range(N_HEADS):
                n = ix_ref[(sq * nsel + r) * N_HEADS + h]
                lr = jnp.where(sub == h, s_ref[gi, n], lr)
            idv = idxv_ref[gi, r][:, 0:1]
            bias = jnp.where(idv == nb - 1, bias_last, bias_far)
            valid = valv_ref[gi, r][:, 0:1] > 0.5 * NEG_INF
            logits.append(jnp.where(valid, lr * (hd ** -0.5) + bias, NEG_INF))

        own = jnp.sum(jnp.where(lane == sq, qk, 0.0), axis=1, keepdims=True)
        l_own = own * (hd ** -0.5) + rbt[:, 0:1]
        m = l_own
        for lr in logits:
            m = jnp.maximum(m, jnp.max(lr, axis=1, keepdims=True))
        p_own = jnp.exp(l_own - m)
        den = p_own
        probs = []
        for lr in logits:
            pr = jnp.exp(lr - m)
            den = den + jnp.sum(pr, axis=1, keepdims=True)
            probs.append(pr)
        softmax.append((jnp.concatenate(probs, axis=1), p_own, den))

    for_tiles(b, lambda cp: cp.wait())
    slot = lax.rem(b, 2)
    for gi in range(grp):
        pcat, p_own, den = softmax[gi]
        rows = []
        for h in range(N_HEADS):
            vt = vbuf[slot, gi, h].astype(BF16)
            rows.append(lax.dot_general(pcat[h:h + 1, :].astype(BF16), vt, (((1,), (1,)), ((), ())),
                                        preferred_element_type=F32))
        acc = jnp.concatenate(rows, axis=0)
        y_ref[gi] = (acc + p_own * vnew_ref[gi]) / den

    @pl.when(b == n_steps - 1)
    def _():
        for_tiles(0, lambda cp: cp.wait())


def _decode(pt_flat, idx_flat, s_all, idxv, valv, qt, kt, v_new, rbt, bkt_last, vcache_t, n_seq, n_pages):
    w = qt.shape[0]
    hd = w // N_HEADS
    nb = n_pages // 2
    grp = DEC_SEQS
    assert n_seq % (2 * grp) == 0
    zero4 = lambda b, pt, ix: (b, 0, 0, 0)
    grid_spec = pltpu.PrefetchScalarGridSpec(
        num_scalar_prefetch=2,
        grid=(n_seq // grp,),
        in_specs=[pl.BlockSpec((grp, nb, N_HEADS, MOBA_BLOCK), zero4),
                  pl.BlockSpec((grp, MOBA_TOPK, N_HEADS, LANES), zero4),
                  pl.BlockSpec((grp, MOBA_TOPK, N_HEADS, LANES), zero4),
                  pl.BlockSpec((w, LANES), lambda b, pt, ix: (0, 0)),
                  pl.BlockSpec((w, LANES), lambda b, pt, ix: (0, 0)),
                  pl.BlockSpec((grp, N_HEADS, hd), lambda b, pt, ix: (b, 0, 0)),
                  pl.BlockSpec((N_HEADS, REL_BUCKETS), lambda b, pt, ix: (0, 0)),
                  pl.BlockSpec((1, MOBA_BLOCK), lambda b, pt, ix: (0, 0)),
                  pl.BlockSpec(memory_space=pl.ANY)],
        out_specs=pl.BlockSpec((grp, N_HEADS, hd), lambda b, pt, ix: (b, 0, 0)),
        scratch_shapes=[pltpu.VMEM((2, grp, N_HEADS, hd, MOBA_TOPK * MOBA_BLOCK), F32),
                        pltpu.SemaphoreType.DMA((2,))])
    return pl.pallas_call(
        functools.partial(_decode_kernel, n_pages=n_pages, n_seq=n_seq, hd=hd),
        out_shape=jax.ShapeDtypeStruct((n_seq, N_HEADS, hd), F32),
        grid_spec=grid_spec,
        compiler_params=_params("arbitrary"),
        name="decode_attn",
    )(pt_flat, idx_flat, s_all, idxv, valv, qt, kt, v_new, rbt, bkt_last, vcache_t)


def _outproj_sample_kernel(x_ref, mod_ref, ml_ref, ya_ref, ga_ref, wo_ref, gf_ref, y_ref, *, d, w):
    ma = ya_ref[...] * _silu(ga_ref[...])
    acc = jnp.dot(ml_ref[...].astype(BF16), wo_ref[0:w, :], preferred_element_type=F32)
    acc = acc + jnp.dot(ma.astype(BF16), wo_ref[w:2 * w, :], preferred_element_type=F32)
    out = x_ref[...] + mod_ref[:, 2 * d:3 * d] * acc
    ms = jnp.mean(out * out, axis=-1, keepdims=True)
    y_ref[...] = out * lax.rsqrt(ms + RMS_EPS) * gf_ref[...]


def _outproj_sample(x, mod, mix_l, y_att, ga, w_out, g_final):
    n, d = x.shape
    w = mix_l.shape[1]
    return pl.pallas_call(
        functools.partial(_outproj_sample_kernel, d=d, w=w),
        out_shape=jax.ShapeDtypeStruct((n, d), F32),
        compiler_params=pltpu.CompilerParams(vmem_limit_bytes=VMEM_LIMIT_BYTES),
        name="outproj_sample",
    )(x, mod, mix_l, y_att, ga, w_out, g_final)


def _block_diag(wb):
    n, c, dd = wb.shape
    return jnp.einsum("ncd,nm->ncmd", wb, jnp.eye(n, dtype=wb.dtype)).reshape(n * c, n * dd)


def kernel(x_prompt, x_sample, cache_k, cache_v, state_lru_h, state_lru_conv, page_table, c_prompt, c_sample, w_ada, b_ada, g_norm, w_in, conv_w, conv_b, w_rgate, b_rgate, w_igate, b_igate, lru_a_param, rel_bias, w_out, g_final):
    bp, s, d = x_prompt.shape
    ns = x_sample.shape[0]
    w = state_lru_h.shape[1]
    hd = w // N_HEADS
    n_pages = page_table.shape[1]
    page = cache_k.shape[1]
    past = n_pages * page
    assert x_sample.shape[1] == 1 and 2 * page == MOBA_BLOCK and past % MOBA_BLOCK == 0
    assert s % MOBA_BLOCK == 0 and s % ROW_TILE == 0 and s % LRU_CHUNK == 0
    assert w % LANES == 0 and ns <= LANES
    assert MOBA_BLOCK + 1 >= REL_MAX_DIST

    w_in_bf, w_in_t = _wprep(w_in, w)
    w_out_bf = w_out.astype(BF16)
    wg = jnp.concatenate([_block_diag(w_rgate), _block_diag(w_igate)], axis=1).astype(BF16)
    bg = jnp.concatenate([b_rgate, b_igate]).reshape(1, 2 * w)
    g_norm2 = g_norm.reshape(1, d)
    g_final2 = g_final.reshape(1, d)
    conv_b2 = conv_b.reshape(1, w)
    a_param2 = lru_a_param.reshape(1, w)

    mod = _adaln(jnp.concatenate([c_prompt, c_sample], axis=0), w_ada, b_ada)
    mod_p = mod[0:bp].reshape(bp, 1, 3 * d)
    mod_s = mod[bp:bp + ns]

    xs = x_sample.reshape(ns, d)
    cbuf = state_lru_conv.transpose(1, 0, 2)
    qt, ktn, k_s, v_s, h_s, cnew, mix_ls, ga_s = _sample_proj(
        xs, mod_s, g_norm2, w_in_bf, w_in_t, cbuf, state_lru_h, conv_w, conv_b2, wg, bg, a_param2)
    pt_flat = page_table.reshape(-1)
    n_phys = cache_k.shape[0]
    kc_t = cache_k.transpose(0, 2, 3, 1).reshape(n_phys, w, page)
    vc_t = cache_v.transpose(0, 2, 3, 1)

    u, gl, q, ga, kt, vt, ktb, vtb = _inproj(x_prompt, mod_p, g_norm2, w_in_bf, w_in_t)
    mix_a, s_all, idxv, valv = _attn_and_score(q, ktb, vtb, ga, rel_bias, pt_flat, qt, kc_t, ns, n_pages)
    y_prompt, h_last, tail = _lru_outproj(u, gl, conv_w, conv_b2, wg, bg, a_param2,
                                          x_prompt, mod_p, mix_a, w_out_bf, g_final2)
    k_prompt = kt.reshape(bp, N_HEADS, hd, s).transpose(0, 3, 1, 2)
    v_prompt = vt.reshape(bp, N_HEADS, hd, s).transpose(0, 3, 1, 2)
    lru_h_prompt = h_last[:, 0, :]
    lru_conv_prompt = tail[:, SUBLANES - (CONV_W - 1):, :]

    idx_flat = idxv[:, :, :, 0].reshape(-1)
    nb = n_pages // 2
    pos = (nb - 1) * MOBA_BLOCK + jnp.arange(MOBA_BLOCK, dtype=I32)
    bkt_last = _t5_bucket(past - pos).reshape(1, MOBA_BLOCK)
    y_att = _decode(pt_flat, idx_flat, s_all, idxv, valv, qt, ktn, v_s.reshape(ns, N_HEADS, hd),
                    rel_bias.T, bkt_last, vc_t, ns, n_pages)
    y_s = _outproj_sample(xs, mod_s, mix_ls, y_att.reshape(ns, w), ga_s, w_out_bf, g_final2)

    return (y_prompt, y_s.reshape(ns, 1, d), k_prompt, v_prompt, lru_h_prompt, lru_conv_prompt,
            k_s.reshape(ns, 1, N_HEADS, hd), v_s.reshape(ns, 1, N_HEADS, hd), h_s,
            cnew.transpose(1, 0, 2))
```

```python
import functools
import math

import jax
import jax.numpy as jnp
from jax import lax
from jax.experimental import pallas as pl
from jax.experimental.pallas import tpu as pltpu

F32 = jnp.float32
BF16 = jnp.bfloat16
I32 = jnp.int32

N_HEADS = 8
LRU_BLOCKS = 8
CONV_W = 4
LRU_C = 8.0
MOBA_BLOCK = 256
MOBA_TOPK = 3
REL_BUCKETS = 32
REL_MAX_DIST = 128
RMS_EPS = 1e-6
NEG_INF = -1e30
HIGHEST = lax.Precision.HIGHEST
LOG2E = math.log2(math.e)

LANES = 128
SUBLANES = 8
VMEM_LIMIT_BYTES = 56 * 1024 * 1024

ROW_TILE = 1024
LRU_CHUNK = 1024
PAGES_PER_GROUP = 64
K_SLOTS = 2
SCORE_PAGES = 16
SHARE_PAGES = 4
MASK_ROWS = 16
DEC_SEQS = 4


def _sigmoid(x):
    return 0.5 * jnp.tanh(0.5 * x) + 0.5


def _silu(x):
    return x * _sigmoid(x)


def _softplus(z):
    return jnp.maximum(z, 0.0) + jnp.log1p(jnp.exp(-jnp.abs(z)))


def _t5_bucket(dist):
    n = jnp.maximum(dist, 0)
    max_exact = REL_BUCKETS // 2
    nf = jnp.maximum(n, 1).astype(F32)
    pos = jnp.log(nf / max_exact) / math.log(REL_MAX_DIST / max_exact) * (REL_BUCKETS - max_exact)
    large = max_exact + jnp.where(pos >= 0, jnp.floor(pos), jnp.ceil(pos)).astype(I32)
    large = jnp.minimum(large, REL_BUCKETS - 1)
    return jnp.where(n < max_exact, n, large)


def _params(*sem):
    return pltpu.CompilerParams(dimension_semantics=sem, vmem_limit_bytes=VMEM_LIMIT_BYTES)


def _adaln_kernel(c_ref, w_ref, b_ref, o_ref):
    s = _silu(c_ref[...]).astype(BF16)
    o_ref[...] = jnp.dot(s, w_ref[...].astype(BF16), preferred_element_type=F32) + b_ref[...]


def _adaln(c, w_ada, b_ada):
    n, d = c.shape
    return pl.pallas_call(
        _adaln_kernel,
        out_shape=jax.ShapeDtypeStruct((n, 3 * d), F32),
        grid=(3,),
        in_specs=[pl.BlockSpec((n, d), lambda j: (0, 0)),
                  pl.BlockSpec((d, d), lambda j: (0, j)),
                  pl.BlockSpec((1, d), lambda j: (0, j))],
        out_specs=pl.BlockSpec((n, d), lambda j: (0, j)),
        compiler_params=_params("arbitrary"),
        name="adaln",
    )(c, w_ada, b_ada.reshape(1, 3 * d))


def _wprep_kernel(w_ref, wb_ref, wt_ref):
    x = w_ref[...]
    wb_ref[...] = x.astype(BF16)
    wt_ref[...] = x.T.astype(BF16)


def _wprep(w_in, cols):
    d, n = w_in.shape
    return pl.pallas_call(
        _wprep_kernel,
        out_shape=[jax.ShapeDtypeStruct((d, n), BF16), jax.ShapeDtypeStruct((n, d), BF16)],
        grid=(n // cols,),
        in_specs=[pl.BlockSpec((d, cols), lambda j: (0, j))],
        out_specs=[pl.BlockSpec((d, cols), lambda j: (0, j)), pl.BlockSpec((cols, d), lambda j: (j, 0))],
        compiler_params=_params("arbitrary"),
        name="wprep",
    )(w_in)


def _modulated_norm(x, mod, g, d):
    ms = jnp.mean(x * x, axis=-1, keepdims=True)
    xn = x * lax.rsqrt(ms + RMS_EPS) * g
    return xn * (1.0 + mod[:, d:2 * d]) + mod[:, 0:d]


def _inproj_kernel(x_ref, mod_ref, g_ref, win_ref, wkt_ref, wvt_ref,
                   u_ref, gl_ref, q_ref, ga_ref, kt_ref, vt_ref, ktb_ref, vtb_ref, *, d, w):
    xm = _modulated_norm(x_ref[...], mod_ref[...], g_ref[...], d).astype(BF16)
    pr = jnp.dot(xm, win_ref[:, 0:3 * w], preferred_element_type=F32)
    u_ref[...] = pr[:, 0:w]
    gl_ref[...] = pr[:, w:2 * w]
    q_ref[...] = (pr[:, 2 * w:3 * w] * ((w // N_HEADS) ** -0.5 * LOG2E)).astype(BF16)
    ga_ref[...] = jnp.dot(xm, win_ref[:, 5 * w:6 * w], preferred_element_type=F32)
    nt = (((1,), (1,)), ((), ()))
    kt = lax.dot_general(wkt_ref[...], xm, nt, preferred_element_type=F32)
    vt = lax.dot_general(wvt_ref[...], xm, nt, preferred_element_type=F32)
    kt_ref[...] = kt
    vt_ref[...] = vt
    ktb_ref[...] = kt.astype(BF16)
    vtb_ref[...] = vt.astype(BF16)


def _inproj(x, mod3, g_norm, w_in_bf, w_in_t):
    b, s, d = x.shape
    w = w_in_bf.shape[1] // 6
    tm = ROW_TILE
    row = pl.BlockSpec((None, tm, w), lambda i, t: (i, t, 0))
    col = pl.BlockSpec((None, w, tm), lambda i, t: (i, 0, t))
    return pl.pallas_call(
        functools.partial(_inproj_kernel, d=d, w=w),
        out_shape=[jax.ShapeDtypeStruct((b, s, w), F32)] * 2 + [jax.ShapeDtypeStruct((b, s, w), BF16)]
        + [jax.ShapeDtypeStruct((b, s, w), F32)] + [jax.ShapeDtypeStruct((b, w, s), F32)] * 2
        + [jax.ShapeDtypeStruct((b, w, s), BF16)] * 2,
        grid=(b, s // tm),
        in_specs=[pl.BlockSpec((None, tm, d), lambda i, t: (i, t, 0)),
                  pl.BlockSpec((None, 1, 3 * d), lambda i, t: (i, 0, 0)),
                  pl.BlockSpec((1, d), lambda i, t: (0, 0)),
                  pl.BlockSpec((d, 6 * w), lambda i, t: (0, 0)),
                  pl.BlockSpec((w, d), lambda i, t: (3, 0)),
                  pl.BlockSpec((w, d), lambda i, t: (4, 0))],
        out_specs=[row, row, row, row, col, col, col, col],
        compiler_params=_params("arbitrary", "arbitrary"),
        name="inproj",
    )(x, mod3, g_norm, w_in_bf, w_in_t, w_in_t)


def _lru_gates(u_conv, wg_ref, bg_ref, ap_ref, w):
    g2 = jnp.dot(u_conv.astype(BF16), wg_ref[...], preferred_element_type=F32) + bg_ref[...]
    r = _sigmoid(g2[:, 0:w])
    i = _sigmoid(g2[:, w:2 * w])
    log_a = (-LRU_C * r) * _softplus(-ap_ref[...])
    a = jnp.exp(log_a)
    z = -jnp.tanh(log_a) * (a * a + 1.0)
    root = jnp.where(z > 0.0, z * lax.rsqrt(z), 0.0)
    bx = root * (i * u_conv)
    return a, bx


def _lru_out_kernel(u_ref, gl_ref, cw_ref, cb_ref, wg_ref, bg_ref, ap_ref,
                    x_ref, mod_ref, ma_ref, wo_ref, gf_ref,
                    y_ref, hl_ref, tail_ref, a_scr, b_scr, hs_scr, h_scr, tail_scr, *, tc, w, d):
    c = pl.program_id(1)

    @pl.when(c == 0)
    def _():
        h_scr[...] = jnp.zeros_like(h_scr)
        tail_scr[...] = jnp.zeros_like(tail_scr)

    u = u_ref[...]
    cw = cw_ref[...]
    tail = tail_scr[...]
    u1, u2, u3 = (tail[SUBLANES - k:SUBLANES - k + 1] for k in (1, 2, 3))

    def shift1(z, first):
        return jnp.concatenate([first, z[0:tc - 1]], axis=0)

    z = shift1(u * cw[0:1], u1 * cw[0:1])
    z = shift1(u * cw[1:2] + z, u1 * cw[1:2] + u2 * cw[0:1])
    z = shift1(u * cw[2:3] + z, u1 * cw[2:3] + u2 * cw[1:2] + u3 * cw[0:1])
    u_conv = cb_ref[...] + (u * cw[3:4] + z)
    tail_scr[...] = u[tc - SUBLANES:tc]
    a, bx = _lru_gates(u_conv, wg_ref, bg_ref, ap_ref, w)
    a_scr[...] = a
    b_scr[...] = bx

    row = lax.broadcasted_iota(I32, (SUBLANES, w), 0)

    def tile(t, h):
        r0 = pl.multiple_of(t * SUBLANES, SUBLANES)
        at = a_scr[pl.ds(r0, SUBLANES), :]
        bt = b_scr[pl.ds(r0, SUBLANES), :]
        for dd in (1, 2, 4):
            keep = row >= dd
            bt = jnp.where(keep, at * pltpu.roll(bt, dd, 0) + bt, bt)
            at = jnp.where(keep, at * pltpu.roll(at, dd, 0), at)
        hs = at * h + bt
        hs_scr[pl.ds(r0, SUBLANES), :] = hs
        return jnp.broadcast_to(hs[SUBLANES - 1:SUBLANES, :], (SUBLANES, w))

    h = lax.fori_loop(0, tc // SUBLANES, tile, h_scr[...], unroll=4)
    h_scr[...] = h
    hl_ref[...] = h
    tail_ref[...] = u[tc - SUBLANES:tc]
    mix_l = (hs_scr[...] * _silu(gl_ref[...])).astype(BF16)
    acc = jnp.dot(mix_l, wo_ref[0:w, :], preferred_element_type=F32)
    acc = acc + jnp.dot(ma_ref[...], wo_ref[w:2 * w, :], preferred_element_type=F32)
    out = x_ref[...] + mod_ref[:, 2 * d:3 * d] * acc
    ms = jnp.mean(out * out, axis=-1, keepdims=True)
    y_ref[...] = out * lax.rsqrt(ms + RMS_EPS) * gf_ref[...]


def _lru_outproj(u, gl, conv_w, conv_b, wg, bg, a_param, x, mod3, mix_a, w_out, g_final):
    b, s, w = u.shape
    d = x.shape[-1]
    tc = LRU_CHUNK
    const = lambda shape: pl.BlockSpec(shape, lambda i, t: (0,) * len(shape))
    return pl.pallas_call(
        functools.partial(_lru_out_kernel, tc=tc, w=w, d=d),
        out_shape=[jax.ShapeDtypeStruct((b, s, d), F32),
                   jax.ShapeDtypeStruct((b, SUBLANES, w), F32),
                   jax.ShapeDtypeStruct((b, SUBLANES, w), F32)],
        grid=(b, s // tc),
        in_specs=[pl.BlockSpec((None, tc, w), lambda i, t: (i, t, 0)),
                  pl.BlockSpec((None, tc, w), lambda i, t: (i, t, 0)),
                  const((CONV_W, w)), const((1, w)), const((w, 2 * w)), const((1, 2 * w)),
                  const((1, w)),
                  pl.BlockSpec((None, tc, d), lambda i, t: (i, t, 0)),
                  pl.BlockSpec((None, 1, 3 * d), lambda i, t: (i, 0, 0)),
                  pl.BlockSpec((None, tc, w), lambda i, t: (i, t, 0)),
                  const((2 * w, d)), const((1, d))],
        out_specs=[pl.BlockSpec((None, tc, d), lambda i, t: (i, t, 0)),
                   pl.BlockSpec((None, SUBLANES, w), lambda i, t: (i, 0, 0)),
                   pl.BlockSpec((None, SUBLANES, w), lambda i, t: (i, 0, 0))],
        scratch_shapes=[pltpu.VMEM((tc, w), F32), pltpu.VMEM((tc, w), F32),
                        pltpu.VMEM((tc, w), F32), pltpu.VMEM((SUBLANES, w), F32),
                        pltpu.VMEM((SUBLANES, w), F32)],
        compiler_params=_params("arbitrary", "arbitrary"),
        name="lru_outproj",
    )(u, gl, conv_w, conv_b, wg, bg, a_param, x, mod3, mix_a, w_out, g_final)


_NT = (((1,), (1,)), ((), ()))


def _attn_score_kernel(pt_ref, rb_ref, q_ref, kt_ref, vt_ref, ga_ref, bkt_ref, qt_ref, kc_ref,
                       o_ref, s_ref, idx_ref, val_ref,
                       bias_scr, kaug_scr, vb_scr, mpad_scr, qm_scr, lg_scr, kbuf, sem, qb_scr, g_scr,
                       *, hd, nb, n_pages, n_seq):
    p = pl.program_id(0)
    b = pl.program_id(1)
    blk = MOBA_BLOCK
    s = nb * blk
    pg = PAGES_PER_GROUP
    gps = n_pages // pg
    total = n_seq * gps
    seq = p * pl.num_programs(1) + b

    def page_copy(g, jj, slot):
        return pltpu.make_async_copy(kc_ref.at[pt_ref[g * pg + jj]], kbuf.at[slot, jj], sem.at[slot])

    def start_group(g):
        slot = lax.rem(g, K_SLOTS)
        for jj in range(pg):
            page_copy(g, jj, slot).start()

    def wait_group(g):
        slot = lax.rem(g, K_SLOTS)
        for jj in range(pg):
            page_copy(g, jj, slot).wait()

    @pl.when(seq == 0)
    def _():
        for g in range(K_SLOTS):
            start_group(g)

    lane_t = lax.broadcasted_iota(I32, qt_ref.shape, 1)
    qcol = jnp.sum(jnp.where(lane_t == seq, qt_ref[...], 0.0), axis=1, keepdims=True)
    qb_scr[...] = jnp.broadcast_to(qcol, qb_scr.shape)

    @pl.when(b == 0)
    def _():
        qi = lax.broadcasted_iota(I32, (blk, blk), 0)
        ki = lax.broadcasted_iota(I32, (blk, blk), 1)
        for hh in range(2):
            h = 2 * p + hh
            far = rb_ref[REL_BUCKETS - 1, h]
            for which in range(2):
                bk = bkt_ref[which]
                tab = jnp.zeros((blk, blk), F32)
                for k in range(REL_BUCKETS):
                    tab = jnp.where(bk == k, (rb_ref[k, h] - far) * LOG2E, tab)
                if which == 0:
                    tab = jnp.where(qi >= ki, tab, NEG_INF)
                bias_scr[hh, which] = tab
        i16 = lax.broadcasted_iota(I32, (MASK_ROWS, s), 0)
        ind16 = jnp.where(i16 == lax.broadcasted_iota(I32, (MASK_ROWS, s), 1) // blk, 1.0, 0.0).astype(BF16)
        for hh in range(2):
            o0 = hd * (1 - hh)
            kaug_scr[hh, o0:o0 + MASK_ROWS, :] = ind16
            kaug_scr[hh, o0 + MASK_ROWS:o0 + hd, :] = jnp.zeros((hd - MASK_ROWS, s), BF16)
            vb_scr[hh, o0:o0 + hd, :] = jnp.ones((hd, s), BF16)

    def prepare():
        n_i = lax.broadcasted_iota(I32, (SUBLANES, s), 0)
        j_i = lax.broadcasted_iota(I32, (SUBLANES, s), 1) // blk
        ind = jnp.where(n_i == j_i, 1.0, 0.0).astype(BF16)
        ktb = kt_ref[...]
        km8 = lax.dot_general(ind, ktb, _NT, preferred_element_type=F32) * (1.0 / blk)
        vtb = vt_ref[...]
        for hh in range(2):
            own = slice(hd * hh, hd * (hh + 1))
            kaug_scr[hh, own, :] = ktb[own]
            vb_scr[hh, own, :] = vtb[own]
        qsb = q_ref[...]
        lane_q = lax.broadcasted_iota(I32, (1, LANES), 1)

        for hh in range(2):
            head_lanes = jnp.where((lane_q >= hd * hh) & (lane_q < hd * (hh + 1)), 1.0, 0.0).astype(BF16)
            qm = qsb * head_lanes
            qm_scr[hh] = qm
            c0 = min(MOBA_TOPK + 1, nb) * blk
            for q0, q1, ranked in ((0, c0, False), (c0, s, True)):
                if q1 == q0:
                    continue
                gt = lax.dot_general(km8.astype(BF16), qm[q0:q1], _NT, preferred_element_type=F32)
                n_p = lax.broadcasted_iota(I32, gt.shape, 0)
                j_p = (lax.broadcasted_iota(I32, gt.shape, 1) + q0) // blk
                past = n_p < j_p
                g = jnp.where(past, gt, NEG_INF)
                keep = past & (g > 0.5 * NEG_INF * (hd ** -0.5 * LOG2E))
                if ranked:
                    cnt = jnp.zeros(gt.shape, I32)
                    for m in range(nb - 1):
                        gm = g[m:m + 1, :]
                        cnt = cnt + jnp.where((gm > g) | ((gm == g) & (n_p > m)), 1, 0)
                    keep = keep & (cnt < MOBA_TOPK)
                mpad_scr[hh, :, q0:q1] = jnp.where(keep | (n_p >= j_p), 0.0, NEG_INF)

    lane = lax.broadcasted_iota(I32, (blk, LANES), 1)

    def attend(j, hh, par):
        rows = slice(j * blk, (j + 1) * blk)
        qa = qm_scr[hh, rows, :]
        if j > 0:
            o0 = hd * (1 - hh)
            pieces = [mpad_scr[hh, :, rows], jnp.zeros((LANES - o0 - SUBLANES, blk), F32)]
            if o0:
                pieces.insert(0, jnp.zeros((o0, blk), F32))
            slab = jnp.concatenate(pieces, axis=0)
            qa = qa + slab.T.astype(BF16)
        macc = None
        for n in range(j + 1):
            sc = jnp.dot(qa, kaug_scr[hh, :, n * blk:(n + 1) * blk],
                         preferred_element_type=F32)
            if n == j:
                sc = sc + bias_scr[hh, 0]
            elif n == j - 1:
                sc = sc + bias_scr[hh, 1]
            lg_scr[par, n] = sc
            mx = jnp.maximum(sc[:, 0:LANES], sc[:, LANES:2 * LANES])
            macc = mx if macc is None else jnp.maximum(macc, mx)
        mrow = jnp.max(macc, axis=1, keepdims=True)
        acc = None
        for n in range(j + 1):
            pn = jnp.exp2(lg_scr[par, n] - mrow).astype(BF16)
            pv = lax.dot_general(pn, vb_scr[hh, :, n * blk:(n + 1) * blk], _NT,
                                 preferred_element_type=F32)
            acc = pv if acc is None else acc + pv
        return acc

    def finish(j, accs):
        rows = slice(j * blk, (j + 1) * blk)
        first = lane < hd
        num = jnp.where(first, accs[0], accs[1])
        den = pltpu.roll(jnp.where(first, accs[1], accs[0]), hd, 1)
        o_ref[rows, :] = (num / den * _silu(ga_ref[rows, :])).astype(o_ref.dtype)

    sub8 = lax.broadcasted_iota(I32, (SUBLANES, LANES), 0)
    tiles = hd // SUBLANES

    def head_sums(parts):
        assert len(parts) == SUBLANES
        for k in (4, 2, 1):
            low = (sub8 % (2 * k)) < k
            nxt = []
            for a in range(k):
                first, second = parts[a], parts[a + k]
                lo = jnp.where(low, first, second)
                hi = jnp.where(low, second, first)
                if k == SUBLANES // 2:
                    moved = pltpu.roll(hi, k, 0)
                else:
                    moved = jnp.where(low, pltpu.roll(hi, SUBLANES - k, 0), pltpu.roll(hi, k, 0))
                nxt.append(lo + moved)
            parts = nxt
        return parts[0]

    def score_group(sg, slot):
        def trip(t, carry):
            for qq in range(SCORE_PAGES // SHARE_PAGES):
                p0 = t * SCORE_PAGES + qq * SHARE_PAGES
                pages = [kbuf.at[slot, p0 + pp] for pp in range(SHARE_PAGES)]
                part = [[] for _ in range(SHARE_PAGES)]
                for h in range(N_HEADS):
                    qh = [qb_scr[(h * tiles + i) * SUBLANES:(h * tiles + i + 1) * SUBLANES, :]
                          for i in range(tiles)]
                    for pp in range(SHARE_PAGES):
                        acc = None
                        for i in range(tiles):
                            r0 = (h * tiles + i) * SUBLANES
                            term = pages[pp][r0:r0 + SUBLANES, :] * qh[i]
                            acc = term if acc is None else acc + term
                        part[pp].append(acc)
                logits = [head_sums(part[pp]) for pp in range(SHARE_PAGES)]
                for bb in range(SHARE_PAGES // 2):
                    n = sg * (pg // 2) + p0 // 2 + bb
                    s_ref[n, :, 0:LANES] = logits[2 * bb]
                    s_ref[n, :, LANES:2 * LANES] = logits[2 * bb + 1]
                    bsum = jnp.sum(logits[2 * bb] + logits[2 * bb + 1], axis=1, keepdims=True)
                    g_scr[n] = jnp.broadcast_to(bsum, (N_HEADS, LANES))
            return carry

        lax.fori_loop(0, pg // SCORE_PAGES, trip, 0)

    items = [(0, hh) for hh in range(2)] + [(j, hh) for j in reversed(range(1, nb)) for hh in range(2)]
    work = sum(j + 1 for j, _ in items)
    segments = [[] for _ in range(gps)]
    done = 0
    for item in items:
        segments[min(gps - 1, done * gps // work)].append(item)
        done += item[0] + 1

    accs = {}
    count = 0
    for sg in range(gps):
        g = seq * gps + sg
        wait_group(g)
        score_group(sg, lax.rem(g, K_SLOTS))

        start_group(lax.rem(g + K_SLOTS, total))

        if sg == 0:
            prepare()
        for j, hh in segments[sg]:
            accs.setdefault(j, []).append(attend(j, hh, count % 2))
            count += 1
            if len(accs[j]) == 2:
                finish(j, accs.pop(j))

    @pl.when(seq == n_seq - 1)
    def _():
        for g in range(K_SLOTS):
            wait_group(g)

    gate = g_scr[...] * (1.0 / MOBA_BLOCK)
    nidx = lax.broadcasted_iota(I32, gate.shape, 0)
    for r in range(MOBA_TOPK):
        mx = jnp.max(gate, axis=0)
        am = jnp.min(jnp.where(gate == mx[None], nidx, gate.shape[0]), axis=0)
        idx_ref[r] = am
        val_ref[r] = mx
        gate = jnp.where(nidx == am[None], -jnp.inf, gate)


def _attn_and_score(q, kt, vt, ga, rel_bias, pt_flat, q_dec, kcache_t, n_seq, n_pages):
    b, s, w = q.shape
    hd = w // N_HEADS
    blk = MOBA_BLOCK
    nb = s // blk
    npair = w // LANES
    nblk = n_pages // 2
    assert nb <= SUBLANES and LANES == 2 * hd
    assert n_seq == npair * b and n_pages % PAGES_PER_GROUP == 0
    assert (n_seq * (n_pages // PAGES_PER_GROUP)) % K_SLOTS == 0
    dq = jnp.arange(blk, dtype=I32)
    d_own = dq[:, None] - dq[None, :]
    bkt = jnp.stack([_t5_bucket(d_own), _t5_bucket(d_own + blk)])
    per_seq = lambda p, i, pt: (p * b + i, 0, 0, 0)
    grid_spec = pltpu.PrefetchScalarGridSpec(
        num_scalar_prefetch=1,
        grid=(npair, b),
        in_specs=[pl.BlockSpec(memory_space=pltpu.SMEM),
                  pl.BlockSpec((None, s, LANES), lambda p, i, pt: (i, 0, p)),
                  pl.BlockSpec((None, LANES, s), lambda p, i, pt: (i, p, 0)),
                  pl.BlockSpec((None, LANES, s), lambda p, i, pt: (i, p, 0)),
                  pl.BlockSpec((None, s, LANES), lambda p, i, pt: (i, 0, p)),
                  pl.BlockSpec((2, blk, blk), lambda p, i, pt: (0, 0, 0)),
                  pl.BlockSpec((w, LANES), lambda p, i, pt: (0, 0)),
                  pl.BlockSpec(memory_space=pl.ANY)],
        out_specs=[pl.BlockSpec((None, s, LANES), lambda p, i, pt: (i, 0, p)),
                   pl.BlockSpec((None, nblk, N_HEADS, MOBA_BLOCK), per_seq),
                   pl.BlockSpec((None, MOBA_TOPK, N_HEADS, LANES), per_seq),
                   pl.BlockSpec((None, MOBA_TOPK, N_HEADS, LANES), per_seq)],
        scratch_shapes=[pltpu.VMEM((2, 2, blk, blk), F32),
                        pltpu.VMEM((2, LANES, s), BF16),
                        pltpu.VMEM((2, LANES, s), BF16),
                        pltpu.VMEM((2, SUBLANES, s), F32),
                        pltpu.VMEM((2, s, LANES), BF16),
                        pltpu.VMEM((2, nb, blk, blk), F32),
                        pltpu.VMEM((K_SLOTS, PAGES_PER_GROUP, w, LANES), F32),
                        pltpu.SemaphoreType.DMA((K_SLOTS,)),
                        pltpu.VMEM((w, LANES), F32),
                        pltpu.VMEM((nblk, N_HEADS, LANES), F32)])
    return pl.pallas_call(
        functools.partial(_attn_score_kernel, hd=hd, nb=nb, n_pages=n_pages, n_seq=n_seq),
        out_shape=[jax.ShapeDtypeStruct((b, s, w), BF16),
                   jax.ShapeDtypeStruct((n_seq, nblk, N_HEADS, MOBA_BLOCK), F32),
                   jax.ShapeDtypeStruct((n_seq, MOBA_TOPK, N_HEADS, LANES), I32),
                   jax.ShapeDtypeStruct((n_seq, MOBA_TOPK, N_HEADS, LANES), F32)],
        grid_spec=grid_spec,
        compiler_params=_params("arbitrary", "arbitrary"),
        name="attn_score",
    )(pt_flat, rel_bias, q, kt, vt, ga, bkt, q_dec, kcache_t)


def _sample_proj_kernel(x_ref, mod_ref, g_ref, win_ref, wqkt_ref, cbuf_ref, h0_ref, cw_ref, cb_ref,
                        wg_ref, bg_ref, ap_ref,
                        qt_ref, kt_ref, k_ref, v_ref, h_ref, cnew_ref, ml_ref, ga_ref, *, d, w):
    n = x_ref.shape[0]
    xm = _modulated_norm(x_ref[...], mod_ref[...], g_ref[...], d).astype(BF16)
    pr = jnp.dot(xm, win_ref[...], preferred_element_type=F32)
    u = pr[:, 0:w]
    k_ref[...] = pr[:, 3 * w:4 * w]
    v_ref[...] = pr[:, 4 * w:5 * w]
    ga_ref[...] = pr[:, 5 * w:6 * w]
    xpad = jnp.concatenate([xm, jnp.zeros((LANES - n, d), BF16)], axis=0)
    pt = lax.dot_general(wqkt_ref[2 * w:4 * w, :], xpad, (((1,), (1,)), ((), ())),
                         preferred_element_type=F32)
    qt_ref[...] = pt[0:w]
    kt_ref[...] = pt[w:2 * w]
    cw = cw_ref[...]
    u_conv = cb_ref[...] + (cbuf_ref[0] * cw[0:1] + cbuf_ref[1] * cw[1:2]
                            + cbuf_ref[2] * cw[2:3] + u * cw[3:4])
    a, bx = _lru_gates(u_conv, wg_ref, bg_ref, ap_ref, w)
    h = a * h0_ref[...] + bx
    h_ref[...] = h
    cnew_ref[0] = cbuf_ref[1]
    cnew_ref[1] = cbuf_ref[2]
    cnew_ref[2] = u
    ml_ref[...] = h * _silu(pr[:, w:2 * w])


def _sample_proj(x, mod, g_norm, w_in_bf, w_in_t, cbuf, h0, conv_w, conv_b, wg, bg, a_param):
    n, d = x.shape
    w = h0.shape[1]
    row = jax.ShapeDtypeStruct((n, w), F32)
    col = jax.ShapeDtypeStruct((w, LANES), F32)
    return pl.pallas_call(
        functools.partial(_sample_proj_kernel, d=d, w=w),
        out_shape=[col, col, row, row, row, jax.ShapeDtypeStruct((CONV_W - 1, n, w), F32), row, row],
        compiler_params=pltpu.CompilerParams(vmem_limit_bytes=VMEM_LIMIT_BYTES),
        name="sample_proj",
    )(x, mod, g_norm, w_in_bf, w_in_t, cbuf, h0, conv_w, conv_b, wg, bg, a_param)


def _decode_kernel(pt_ref, ix_ref, s_ref, idxv_ref, valv_ref, qt_ref, kt_ref, vnew_ref, rbt_ref,
                   bkt_ref, vc_ref, y_ref, vbuf, sem, *, n_pages, n_seq, hd):
    b = pl.program_id(0)
    nsel = MOBA_TOPK
    blk = MOBA_BLOCK
    nb = n_pages // 2
    grp = DEC_SEQS
    n_steps = n_seq // grp

    def tile_copy(step, gi, r, h, pp, slot):
        sq = step * grp + gi
        page = pt_ref[sq * n_pages + 2 * ix_ref[(sq * nsel + r) * N_HEADS + h] + pp]
        return pltpu.make_async_copy(
            vc_ref.at[page, h], vbuf.at[slot, gi, h, :, pl.ds((2 * r + pp) * LANES, LANES)], sem.at[slot])

    def for_tiles(step, fn):
        slot = lax.rem(step, 2)
        for gi in range(grp):
            for r in range(nsel):
                for h in range(N_HEADS):
                    for pp in range(2):
                        fn(tile_copy(step, gi, r, h, pp, slot))

    @pl.when(b == 0)
    def _():
        for_tiles(0, lambda cp: cp.start())

    for_tiles(lax.rem(b + 1, n_steps), lambda cp: cp.start())

    sub = lax.broadcasted_iota(I32, (N_HEADS, blk), 0)
    rbt = rbt_ref[...]
    bias_last = jnp.zeros((N_HEADS, blk), F32)
    for k in range(REL_BUCKETS):
        bias_last = jnp.where(bkt_ref[...] == k, rbt[:, k:k + 1], bias_last)
    bias_far = rbt[:, REL_BUCKETS - 1:REL_BUCKETS]
    lane = lax.broadcasted_iota(I32, (N_HEADS, LANES), 1)
    qk = jnp.sum((qt_ref[...] * kt_ref[...]).reshape(N_HEADS, hd, LANES), axis=1)

    softmax = []
    for gi in range(grp):
        sq = b * grp + gi
        logits = []
        for r in range(nsel):
            lr = jnp.zeros((N_HEADS, blk), F32)
            for h in range(N_HEADS):
                n = ix_ref[(sq * nsel + r) * N_HEADS + h]
                lr = jnp.where(sub == h, s_ref[gi, n], lr)
            idv = idxv_ref[gi, r][:, 0:1]
            bias = jnp.where(idv == nb - 1, bias_last, bias_far)
            valid = valv_ref[gi, r][:, 0:1] > 0.5 * NEG_INF
            logits.append(jnp.where(valid, lr * (hd ** -0.5) + bias, NEG_INF))

        own = jnp.sum(jnp.where(lane == sq, qk, 0.0), axis=1, keepdims=True)
        l_own = own * (hd ** -0.5) + rbt[:, 0:1]
        m = l_own
        for lr in logits:
            m = jnp.maximum(m, jnp.max(lr, axis=1, keepdims=True))
        p_own = jnp.exp(l_own - m)
        den = p_own
        probs = []
        for lr in logits:
            pr = jnp.exp(lr - m)
            den = den + jnp.sum(pr, axis=1, keepdims=True)
            probs.append(pr)
        softmax.append((jnp.concatenate(probs, axis=1), p_own, den))

    for_tiles(b, lambda cp: cp.wait())
    slot = lax.rem(b, 2)
    for gi in range(grp):
        pcat, p_own, den = softmax[gi]
        rows = []
        for h in range(N_HEADS):
            vt = vbuf[slot, gi, h].astype(BF16)
            rows.append(lax.dot_general(pcat[h:h + 1, :].astype(BF16), vt, (((1,), (1,)), ((), ())),
                                        preferred_element_type=F32))
        acc = jnp.concatenate(rows, axis=0)
        y_ref[gi] = (acc + p_own * vnew_ref[gi]) / den

    @pl.when(b == n_steps - 1)
    def _():
        for_tiles(0, lambda cp: cp.wait())


def _decode(pt_flat, idx_flat, s_all, idxv, valv, qt, kt, v_new, rbt, bkt_last, vcache_t, n_seq, n_pages):
    w = qt.shape[0]
    hd = w // N_HEADS
    nb = n_pages // 2
    grp = DEC_SEQS
    assert n_seq % (2 * grp) == 0
    zero4 = lambda b, pt, ix: (b, 0, 0, 0)
    grid_spec = pltpu.PrefetchScalarGridSpec(
        num_scalar_prefetch=2,
        grid=(n_seq // grp,),
        in_specs=[pl.BlockSpec((grp, nb, N_HEADS, MOBA_BLOCK), zero4),
                  pl.BlockSpec((grp, MOBA_TOPK, N_HEADS, LANES), zero4),
                  pl.BlockSpec((grp, MOBA_TOPK, N_HEADS, LANES), zero4),
                  pl.BlockSpec((w, LANES), lambda b, pt, ix: (0, 0)),
                  pl.BlockSpec((w, LANES), lambda b, pt, ix: (0, 0)),
                  pl.BlockSpec((grp, N_HEADS, hd), lambda b, pt, ix: (b, 0, 0)),
                  pl.BlockSpec((N_HEADS, REL_BUCKETS), lambda b, pt, ix: (0, 0)),
                  pl.BlockSpec((1, MOBA_BLOCK), lambda b, pt, ix: (0, 0)),
                  pl.BlockSpec(memory_space=pl.ANY)],
        out_specs=pl.BlockSpec((grp, N_HEADS, hd), lambda b, pt, ix: (b, 0, 0)),
        scratch_shapes=[pltpu.VMEM((2, grp, N_HEADS, hd, MOBA_TOPK * MOBA_BLOCK), F32),
                        pltpu.SemaphoreType.DMA((2,))])
    return pl.pallas_call(
        functools.partial(_decode_kernel, n_pages=n_pages, n_seq=n_seq, hd=hd),
        out_shape=jax.ShapeDtypeStruct((n_seq, N_HEADS, hd), F32),
        grid_spec=grid_spec,
        compiler_params=_params("arbitrary"),
        name="decode_attn",
    )(pt_flat, idx_flat, s_all, idxv, valv, qt, kt, v_new, rbt, bkt_last, vcache_t)


def _outproj_sample_kernel(x_ref, mod_ref, ml_ref, ya_ref, ga_ref, wo_ref, gf_ref, y_ref, *, d, w):
    ma = ya_ref[...] * _silu(ga_ref[...])
    acc = jnp.dot(ml_ref[...].astype(BF16), wo_ref[0:w, :], preferred_element_type=F32)
    acc = acc + jnp.dot(ma.astype(BF16), wo_ref[w:2 * w, :], preferred_element_type=F32)
    out = x_ref[...] + mod_ref[:, 2 * d:3 * d] * acc
    ms = jnp.mean(out * out, axis=-1, keepdims=True)
    y_ref[...] = out * lax.rsqrt(ms + RMS_EPS) * gf_ref[...]


def _outproj_sample(x, mod, mix_l, y_att, ga, w_out, g_final):
    n, d = x.shape
    w = mix_l.shape[1]
    return pl.pallas_call(
        functools.partial(_outproj_sample_kernel, d=d, w=w),
        out_shape=jax.ShapeDtypeStruct((n, d), F32),
        compiler_params=pltpu.CompilerParams(vmem_limit_bytes=VMEM_LIMIT_BYTES),
        name="outproj_sample",
    )(x, mod, mix_l, y_att, ga, w_out, g_final)


def _block_diag(wb):
    n, c, dd = wb.shape
    return jnp.einsum("ncd,nm->ncmd", wb, jnp.eye(n, dtype=wb.dtype)).reshape(n * c, n * dd)


def kernel(x_prompt, x_sample, cache_k, cache_v, state_lru_h, state_lru_conv, page_table, c_prompt, c_sample, w_ada, b_ada, g_norm, w_in, conv_w, conv_b, w_rgate, b_rgate, w_igate, b_igate, lru_a_param, rel_bias, w_out, g_final):
    bp, s, d = x_prompt.shape
    ns = x_sample.shape[0]
    w = state_lru_h.shape[1]
    hd = w // N_HEADS
    n_pages = page_table.shape[1]
    page = cache_k.shape[1]
    past = n_pages * page
    assert x_sample.shape[1] == 1 and 2 * page == MOBA_BLOCK and past % MOBA_BLOCK == 0
    assert s % MOBA_BLOCK == 0 and s % ROW_TILE == 0 and s % LRU_CHUNK == 0
    assert w % LANES == 0 and ns <= LANES
    assert MOBA_BLOCK + 1 >= REL_MAX_DIST

    w_in_bf, w_in_t = _wprep(w_in, w)
    w_out_bf = w_out.astype(BF16)
    wg = jnp.concatenate([_block_diag(w_rgate), _block_diag(w_igate)], axis=1).astype(BF16)
    bg = jnp.concatenate([b_rgate, b_igate]).reshape(1, 2 * w)
    g_norm2 = g_norm.reshape(1, d)
    g_final2 = g_final.reshape(1, d)
    conv_b2 = conv_b.reshape(1, w)
    a_param2 = lru_a_param.reshape(1, w)

    mod = _adaln(jnp.concatenate([c_prompt, c_sample], axis=0), w_ada, b_ada)
    mod_p = mod[0:bp].reshape(bp, 1, 3 * d)
    mod_s = mod[bp:bp + ns]

    xs = x_sample.reshape(ns, d)
    cbuf = state_lru_conv.transpose(1, 0, 2)
    qt, ktn, k_s, v_s, h_s, cnew, mix_ls, ga_s = _sample_proj(
        xs, mod_s, g_norm2, w_in_bf, w_in_t, cbuf, state_lru_h, conv_w, conv_b2, wg, bg, a_param2)
    pt_flat = page_table.reshape(-1)
    n_phys = cache_k.shape[0]
    kc_t = cache_k.transpose(0, 2, 3, 1).reshape(n_phys, w, page)
    vc_t = cache_v.transpose(0, 2, 3, 1)

    u, gl, q, ga, kt, vt, ktb, vtb = _inproj(x_prompt, mod_p, g_norm2, w_in_bf, w_in_t)
    mix_a, s_all, idxv, valv = _attn_and_score(q, ktb, vtb, ga, rel_bias, pt_flat, qt, kc_t, ns, n_pages)
    y_prompt, h_last, tail = _lru_outproj(u, gl, conv_w, conv_b2, wg, bg, a_param2,
                                          x_prompt, mod_p, mix_a, w_out_bf, g_final2)
    k_prompt = kt.reshape(bp, N_HEADS, hd, s).transpose(0, 3, 1, 2)
    v_prompt = vt.reshape(bp, N_HEADS, hd, s).transpose(0, 3, 1, 2)
    lru_h_prompt = h_last[:, 0, :]
    lru_conv_prompt = tail[:, SUBLANES - (CONV_W - 1):, :]

    idx_flat = idxv[:, :, :, 0].reshape(-1)
    nb = n_pages // 2
    pos = (nb - 1) * MOBA_BLOCK + jnp.arange(MOBA_BLOCK, dtype=I32)
    bkt_last = _t5_bucket(past - pos).reshape(1, MOBA_BLOCK)
    y_att = _decode(pt_flat, idx_flat, s_all, idxv, valv, qt, ktn, v_s.reshape(ns, N_HEADS, hd),
                    rel_bias.T, bkt_last, vc_t, ns, n_pages)
    y_s = _outproj_sample(xs, mod_s, mix_ls, y_att.reshape(ns, w), ga_s, w_out_bf, g_final2)

    return (y_prompt, y_s.reshape(ns, 1, d), k_prompt, v_prompt, lru_h_prompt, lru_conv_prompt,
            k_s.reshape(ns, 1, N_HEADS, hd), v_s.reshape(ns, 1, N_HEADS, hd), h_s,
            cnew.transpose(1, 0, 2))
```

```python
import functools
import math

import jax
import jax.numpy as jnp
from jax import lax
from jax.experimental import pallas as pl
from jax.experimental.pallas import tpu as pltpu

F32 = jnp.float32
BF16 = jnp.bfloat16
I32 = jnp.int32

N_HEADS = 8
LRU_BLOCKS = 8
CONV_W = 4
LRU_C = 8.0
MOBA_BLOCK = 256
MOBA_TOPK = 3
REL_BUCKETS = 32
REL_MAX_DIST = 128
RMS_EPS = 1e-6
NEG_INF = -1e30
HIGHEST = lax.Precision.HIGHEST
LOG2E = math.log2(math.e)

LANES = 128
SUBLANES = 8
VMEM_LIMIT_BYTES = 56 * 1024 * 1024

ROW_TILE = 1024
LRU_CHUNK = 1024
PAGES_PER_GROUP = 64
K_SLOTS = 2
SCORE_PAGES = 16
SHARE_PAGES = 4
MASK_ROWS = 16
DEC_SEQS = 4


def _sigmoid(x):
    return 0.5 * jnp.tanh(0.5 * x) + 0.5


def _silu(x):
    return x * _sigmoid(x)


def _softplus(z):
    return jnp.maximum(z, 0.0) + jnp.log1p(jnp.exp(-jnp.abs(z)))


def _t5_bucket(dist):
    n = jnp.maximum(dist, 0)
    max_exact = REL_BUCKETS // 2
    nf = jnp.maximum(n, 1).astype(F32)
    pos = jnp.log(nf / max_exact) / math.log(REL_MAX_DIST / max_exact) * (REL_BUCKETS - max_exact)
    large = max_exact + jnp.where(pos >= 0, jnp.floor(pos), jnp.ceil(pos)).astype(I32)
    large = jnp.minimum(large, REL_BUCKETS - 1)
    return jnp.where(n < max_exact, n, large)


def _params(*sem):
    return pltpu.CompilerParams(dimension_semantics=sem, vmem_limit_bytes=VMEM_LIMIT_BYTES)


def _adaln_kernel(c_ref, w_ref, b_ref, o_ref):
    s = _silu(c_ref[...]).astype(BF16)
    o_ref[...] = jnp.dot(s, w_ref[...].astype(BF16), preferred_element_type=F32) + b_ref[...]


def _adaln(c, w_ada, b_ada):
    n, d = c.shape
    return pl.pallas_call(
        _adaln_kernel,
        out_shape=jax.ShapeDtypeStruct((n, 3 * d), F32),
        grid=(3,),
        in_specs=[pl.BlockSpec((n, d), lambda j: (0, 0)),
                  pl.BlockSpec((d, d), lambda j: (0, j)),
                  pl.BlockSpec((1, d), lambda j: (0, j))],
        out_specs=pl.BlockSpec((n, d), lambda j: (0, j)),
        compiler_params=_params("arbitrary"),
        name="adaln",
    )(c, w_ada, b_ada.reshape(1, 3 * d))


def _wprep_kernel(w_ref, wb_ref, wt_ref):
    x = w_ref[...]
    wb_ref[...] = x.astype(BF16)
    wt_ref[...] = x.T.astype(BF16)


def _wprep(w_in, cols):
    d, n = w_in.shape
    return pl.pallas_call(
        _wprep_kernel,
        out_shape=[jax.ShapeDtypeStruct((d, n), BF16), jax.ShapeDtypeStruct((n, d), BF16)],
        grid=(n // cols,),
        in_specs=[pl.BlockSpec((d, cols), lambda j: (0, j))],
        out_specs=[pl.BlockSpec((d, cols), lambda j: (0, j)), pl.BlockSpec((cols, d), lambda j: (j, 0))],
        compiler_params=_params("arbitrary"),
        name="wprep",
    )(w_in)


def _modulated_norm(x, mod, g, d):
    ms = jnp.mean(x * x, axis=-1, keepdims=True)
    xn = x * lax.rsqrt(ms + RMS_EPS) * g
    return xn * (1.0 + mod[:, d:2 * d]) + mod[:, 0:d]


def _inproj_kernel(x_ref, mod_ref, g_ref, win_ref, wkt_ref, wvt_ref,
                   u_ref, gl_ref, q_ref, ga_ref, kt_ref, vt_ref, ktb_ref, vtb_ref, *, d, w):
    xm = _modulated_norm(x_ref[...], mod_ref[...], g_ref[...], d).astype(BF16)
    pr = jnp.dot(xm, win_ref[:, 0:3 * w], preferred_element_type=F32)
    u_ref[...] = pr[:, 0:w]
    gl_ref[...] = pr[:, w:2 * w]
    q_ref[...] = (pr[:, 2 * w:3 * w] * ((w // N_HEADS) ** -0.5 * LOG2E)).astype(BF16)
    ga_ref[...] = jnp.dot(xm, win_ref[:, 5 * w:6 * w], preferred_element_type=F32)
    nt = (((1,), (1,)), ((), ()))
    kt = lax.dot_general(wkt_ref[...], xm, nt, preferred_element_type=F32)
    vt = lax.dot_general(wvt_ref[...], xm, nt, preferred_element_type=F32)
    kt_ref[...] = kt
    vt_ref[...] = vt
    ktb_ref[...] = kt.astype(BF16)
    vtb_ref[...] = vt.astype(BF16)


def _inproj(x, mod3, g_norm, w_in_bf, w_in_t):
    b, s, d = x.shape
    w = w_in_bf.shape[1] // 6
    tm = ROW_TILE
    row = pl.BlockSpec((None, tm, w), lambda i, t: (i, t, 0))
    col = pl.BlockSpec((None, w, tm), lambda i, t: (i, 0, t))
    return pl.pallas_call(
        functools.partial(_inproj_kernel, d=d, w=w),
        out_shape=[jax.ShapeDtypeStruct((b, s, w), F32)] * 2 + [jax.ShapeDtypeStruct((b, s, w), BF16)]
        + [jax.ShapeDtypeStruct((b, s, w), F32)] + [jax.ShapeDtypeStruct((b, w, s), F32)] * 2
        + [jax.ShapeDtypeStruct((b, w, s), BF16)] * 2,
        grid=(b, s // tm),
        in_specs=[pl.BlockSpec((None, tm, d), lambda i, t: (i, t, 0)),
                  pl.BlockSpec((None, 1, 3 * d), lambda i, t: (i, 0, 0)),
                  pl.BlockSpec((1, d), lambda i, t: (0, 0)),
                  pl.BlockSpec((d, 6 * w), lambda i, t: (0, 0)),
                  pl.BlockSpec((w, d), lambda i, t: (3, 0)),
                  pl.BlockSpec((w, d), lambda i, t: (4, 0))],
        out_specs=[row, row, row, row, col, col, col, col],
        compiler_params=_params("arbitrary", "arbitrary"),
        name="inproj",
    )(x, mod3, g_norm, w_in_bf, w_in_t, w_in_t)


def _lru_gates(u_conv, wg_ref, bg_ref, ap_ref, w):
    g2 = jnp.dot(u_conv.astype(BF16), wg_ref[...], preferred_element_type=F32) + bg_ref[...]
    r = _sigmoid(g2[:, 0:w])
    i = _sigmoid(g2[:, w:2 * w])
    log_a = (-LRU_C * r) * _softplus(-ap_ref[...])
    a = jnp.exp(log_a)
    z = -jnp.tanh(log_a) * (a * a + 1.0)
    root = jnp.where(z > 0.0, z * lax.rsqrt(z), 0.0)
    bx = root * (i * u_conv)
    return a, bx


def _lru_out_kernel(u_ref, gl_ref, cw_ref, cb_ref, wg_ref, bg_ref, ap_ref,
                    x_ref, mod_ref, ma_ref, wo_ref, gf_ref,
                    y_ref, hl_ref, tail_ref, a_scr, b_scr, hs_scr, h_scr, tail_scr, *, tc, w, d):
    c = pl.program_id(1)

    @pl.when(c == 0)
    def _():
        h_scr[...] = jnp.zeros_like(h_scr)
        tail_scr[...] = jnp.zeros_like(tail_scr)

    u = u_ref[...]
    cw = cw_ref[...]
    tail = tail_scr[...]
    u1, u2, u3 = (tail[SUBLANES - k:SUBLANES - k + 1] for k in (1, 2, 3))

    def shift1(z, first):
        return jnp.concatenate([first, z[0:tc - 1]], axis=0)

    z = shift1(u * cw[0:1], u1 * cw[0:1])
    z = shift1(u * cw[1:2] + z, u1 * cw[1:2] + u2 * cw[0:1])
    z = shift1(u * cw[2:3] + z, u1 * cw[2:3] + u2 * cw[1:2] + u3 * cw[0:1])
    u_conv = cb_ref[...] + (u * cw[3:4] + z)
    tail_scr[...] = u[tc - SUBLANES:tc]
    a, bx = _lru_gates(u_conv, wg_ref, bg_ref, ap_ref, w)
    a_scr[...] = a
    b_scr[...] = bx

    row = lax.broadcasted_iota(I32, (SUBLANES, w), 0)

    def tile(t, h):
        r0 = pl.multiple_of(t * SUBLANES, SUBLANES)
        at = a_scr[pl.ds(r0, SUBLANES), :]
        bt = b_scr[pl.ds(r0, SUBLANES), :]
        for dd in (1, 2, 4):
            keep = row >= dd
            bt = jnp.where(keep, at * pltpu.roll(bt, dd, 0) + bt, bt)
            at = jnp.where(keep, at * pltpu.roll(at, dd, 0), at)
        hs = at * h + bt
        hs_scr[pl.ds(r0, SUBLANES), :] = hs
        return jnp.broadcast_to(hs[SUBLANES - 1:SUBLANES, :], (SUBLANES, w))

    h = lax.fori_loop(0, tc // SUBLANES, tile, h_scr[...], unroll=4)
    h_scr[...] = h
    hl_ref[...] = h
    tail_ref[...] = u[tc - SUBLANES:tc]
    mix_l = (hs_scr[...] * _silu(gl_ref[...])).astype(BF16)
    acc = jnp.dot(mix_l, wo_ref[0:w, :].astype(BF16), preferred_element_type=F32)
    acc = acc + jnp.dot(ma_ref[...], wo_ref[w:2 * w, :].astype(BF16), preferred_element_type=F32)
    out = x_ref[...] + mod_ref[:, 2 * d:3 * d] * acc
    ms = jnp.mean(out * out, axis=-1, keepdims=True)
    y_ref[...] = out * lax.rsqrt(ms + RMS_EPS) * gf_ref[...]


def _lru_outproj(u, gl, conv_w, conv_b, wg, bg, a_param, x, mod3, mix_a, w_out, g_final):
    b, s, w = u.shape
    d = x.shape[-1]
    tc = LRU_CHUNK
    const = lambda shape: pl.BlockSpec(shape, lambda i, t: (0,) * len(shape))
    return pl.pallas_call(
        functools.partial(_lru_out_kernel, tc=tc, w=w, d=d),
        out_shape=[jax.ShapeDtypeStruct((b, s, d), F32),
                   jax.ShapeDtypeStruct((b, SUBLANES, w), F32),
                   jax.ShapeDtypeStruct((b, SUBLANES, w), F32)],
        grid=(b, s // tc),
        in_specs=[pl.BlockSpec((None, tc, w), lambda i, t: (i, t, 0)),
                  pl.BlockSpec((None, tc, w), lambda i, t: (i, t, 0)),
                  const((CONV_W, w)), const((1, w)), const((w, 2 * w)), const((1, 2 * w)),
                  const((1, w)),
                  pl.BlockSpec((None, tc, d), lambda i, t: (i, t, 0)),
                  pl.BlockSpec((None, 1, 3 * d), lambda i, t: (i, 0, 0)),
                  pl.BlockSpec((None, tc, w), lambda i, t: (i, t, 0)),
                  const((2 * w, d)), const((1, d))],
        out_specs=[pl.BlockSpec((None, tc, d), lambda i, t: (i, t, 0)),
                   pl.BlockSpec((None, SUBLANES, w), lambda i, t: (i, 0, 0)),
                   pl.BlockSpec((None, SUBLANES, w), lambda i, t: (i, 0, 0))],
        scratch_shapes=[pltpu.VMEM((tc, w), F32), pltpu.VMEM((tc, w), F32),
                        pltpu.VMEM((tc, w), F32), pltpu.VMEM((SUBLANES, w), F32),
                        pltpu.VMEM((SUBLANES, w), F32)],
        compiler_params=_params("arbitrary", "arbitrary"),
        name="lru_outproj",
    )(u, gl, conv_w, conv_b, wg, bg, a_param, x, mod3, mix_a, w_out, g_final)


_NT = (((1,), (1,)), ((), ()))


def _attn_score_kernel(pt_ref, rb_ref, q_ref, kt_ref, vt_ref, ga_ref, bkt_ref, qt_ref, kc_ref,
                       o_ref, s_ref, idx_ref, val_ref,
                       bias_scr, kaug_scr, vb_scr, mpad_scr, qm_scr, lg_scr, kbuf, sem, qb_scr, g_scr,
                       *, hd, nb, n_pages, n_seq):
    p = pl.program_id(0)
    b = pl.program_id(1)
    blk = MOBA_BLOCK
    s = nb * blk
    pg = PAGES_PER_GROUP
    gps = n_pages // pg
    total = n_seq * gps
    seq = p * pl.num_programs(1) + b

    def page_copy(g, jj, slot):
        return pltpu.make_async_copy(kc_ref.at[pt_ref[g * pg + jj]], kbuf.at[slot, jj], sem.at[slot])

    def start_group(g):
        slot = lax.rem(g, K_SLOTS)
        for jj in range(pg):
            page_copy(g, jj, slot).start()

    def wait_group(g):
        slot = lax.rem(g, K_SLOTS)
        for jj in range(pg):
            page_copy(g, jj, slot).wait()

    @pl.when(seq == 0)
    def _():
        for g in range(K_SLOTS):
            start_group(g)

    lane_t = lax.broadcasted_iota(I32, qt_ref.shape, 1)
    qcol = jnp.sum(jnp.where(lane_t == seq, qt_ref[...], 0.0), axis=1, keepdims=True)
    qb_scr[...] = jnp.broadcast_to(qcol, qb_scr.shape)

    @pl.when(b == 0)
    def _():
        qi = lax.broadcasted_iota(I32, (blk, blk), 0)
        ki = lax.broadcasted_iota(I32, (blk, blk), 1)
        for hh in range(2):
            h = 2 * p + hh
            far = rb_ref[REL_BUCKETS - 1, h]
            for which in range(2):
                bk = bkt_ref[which]
                tab = jnp.zeros((blk, blk), F32)
                for k in range(REL_BUCKETS):
                    tab = jnp.where(bk == k, (rb_ref[k, h] - far) * LOG2E, tab)
                if which == 0:
                    tab = jnp.where(qi >= ki, tab, NEG_INF)
                bias_scr[hh, which] = tab
        i16 = lax.broadcasted_iota(I32, (MASK_ROWS, s), 0)
        ind16 = jnp.where(i16 == lax.broadcasted_iota(I32, (MASK_ROWS, s), 1) // blk, 1.0, 0.0).astype(BF16)
        for hh in range(2):
            o0 = hd * (1 - hh)
            kaug_scr[hh, o0:o0 + MASK_ROWS, :] = ind16
            kaug_scr[hh, o0 + MASK_ROWS:o0 + hd, :] = jnp.zeros((hd - MASK_ROWS, s), BF16)
            vb_scr[hh, o0:o0 + hd, :] = jnp.ones((hd, s), BF16)

    def prepare():
        n_i = lax.broadcasted_iota(I32, (SUBLANES, s), 0)
        j_i = lax.broadcasted_iota(I32, (SUBLANES, s), 1) // blk
        ind = jnp.where(n_i == j_i, 1.0, 0.0).astype(BF16)
        ktb = kt_ref[...]
        km8 = lax.dot_general(ind, ktb, _NT, preferred_element_type=F32) * (1.0 / blk)
        vtb = vt_ref[...]
        for hh in range(2):
            own = slice(hd * hh, hd * (hh + 1))
            kaug_scr[hh, own, :] = ktb[own]
            vb_scr[hh, own, :] = vtb[own]
        qsb = q_ref[...]
        lane_q = lax.broadcasted_iota(I32, (1, LANES), 1)

        for hh in range(2):
            head_lanes = jnp.where((lane_q >= hd * hh) & (lane_q < hd * (hh + 1)), 1.0, 0.0).astype(BF16)
            qm = qsb * head_lanes
            qm_scr[hh] = qm
            c0 = min(MOBA_TOPK + 1, nb) * blk
            for q0, q1, ranked in ((0, c0, False), (c0, s, True)):
                if q1 == q0:
                    continue
                gt = lax.dot_general(km8.astype(BF16), qm[q0:q1], _NT, preferred_element_type=F32)
                n_p = lax.broadcasted_iota(I32, gt.shape, 0)
                j_p = (lax.broadcasted_iota(I32, gt.shape, 1) + q0) // blk
                past = n_p < j_p
                g = jnp.where(past, gt, NEG_INF)
                keep = past & (g > 0.5 * NEG_INF * (hd ** -0.5 * LOG2E))
                if ranked:
                    cnt = jnp.zeros(gt.shape, I32)
                    for m in range(nb - 1):
                        gm = g[m:m + 1, :]
                        cnt = cnt + jnp.where((gm > g) | ((gm == g) & (n_p > m)), 1, 0)
                    keep = keep & (cnt < MOBA_TOPK)
                mpad_scr[hh, :, q0:q1] = jnp.where(keep | (n_p >= j_p), 0.0, NEG_INF)

    lane = lax.broadcasted_iota(I32, (blk, LANES), 1)

    def attend(j, hh, par):
        rows = slice(j * blk, (j + 1) * blk)
        qa = qm_scr[hh, rows, :]
        if j > 0:
            o0 = hd * (1 - hh)
            pieces = [mpad_scr[hh, :, rows], jnp.zeros((LANES - o0 - SUBLANES, blk), F32)]
            if o0:
                pieces.insert(0, jnp.zeros((o0, blk), F32))
            slab = jnp.concatenate(pieces, axis=0)
            qa = qa + slab.T.astype(BF16)
        macc = None
        for n in range(j + 1):
            sc = jnp.dot(qa, kaug_scr[hh, :, n * blk:(n + 1) * blk],
                         preferred_element_type=F32)
            if n == j:
                sc = sc + bias_scr[hh, 0]
            elif n == j - 1:
                sc = sc + bias_scr[hh, 1]
            lg_scr[par, n] = sc
            mx = jnp.maximum(sc[:, 0:LANES], sc[:, LANES:2 * LANES])
            macc = mx if macc is None else jnp.maximum(macc, mx)
        mrow = jnp.max(macc, axis=1, keepdims=True)
        acc = None
        for n in range(j + 1):
            pn = jnp.exp2(lg_scr[par, n] - mrow).astype(BF16)
            pv = lax.dot_general(pn, vb_scr[hh, :, n * blk:(n + 1) * blk], _NT,
                                 preferred_element_type=F32)
            acc = pv if acc is None else acc + pv
        return acc

    def finish(j, accs):
        rows = slice(j * blk, (j + 1) * blk)
        first = lane < hd
        num = jnp.where(first, accs[0], accs[1])
        den = pltpu.roll(jnp.where(first, accs[1], accs[0]), hd, 1)
        o_ref[rows, :] = (num / den * _silu(ga_ref[rows, :])).astype(o_ref.dtype)

    sub8 = lax.broadcasted_iota(I32, (SUBLANES, LANES), 0)
    tiles = hd // SUBLANES

    def head_sums(parts):
        assert len(parts) == SUBLANES
        for k in (4, 2, 1):
            low = (sub8 % (2 * k)) < k
            nxt = []
            for a in range(k):
                first, second = parts[a], parts[a + k]
                lo = jnp.where(low, first, second)
                hi = jnp.where(low, second, first)
                if k == SUBLANES // 2:
                    moved = pltpu.roll(hi, k, 0)
                else:
                    moved = jnp.where(low, pltpu.roll(hi, SUBLANES - k, 0), pltpu.roll(hi, k, 0))
                nxt.append(lo + moved)
            parts = nxt
        return parts[0]

    def score_group(sg, slot):
        def trip(t, carry):
            for qq in range(SCORE_PAGES // SHARE_PAGES):
                p0 = t * SCORE_PAGES + qq * SHARE_PAGES
                pages = [kbuf.at[slot, p0 + pp] for pp in range(SHARE_PAGES)]
                part = [[] for _ in range(SHARE_PAGES)]
                for h in range(N_HEADS):
                    qh = [qb_scr[(h * tiles + i) * SUBLANES:(h * tiles + i + 1) * SUBLANES, :]
                          for i in range(tiles)]
                    for pp in range(SHARE_PAGES):
                        acc = None
                        for i in range(tiles):
                            r0 = (h * tiles + i) * SUBLANES
                            term = pages[pp][r0:r0 + SUBLANES, :] * qh[i]
                            acc = term if acc is None else acc + term
                        part[pp].append(acc)
                logits = [head_sums(part[pp]) for pp in range(SHARE_PAGES)]
                for bb in range(SHARE_PAGES // 2):
                    n = sg * (pg // 2) + p0 // 2 + bb
                    s_ref[n, :, 0:LANES] = logits[2 * bb]
                    s_ref[n, :, LANES:2 * LANES] = logits[2 * bb + 1]
                    bsum = jnp.sum(logits[2 * bb] + logits[2 * bb + 1], axis=1, keepdims=True)
                    g_scr[n] = jnp.broadcast_to(bsum, (N_HEADS, LANES))
            return carry

        lax.fori_loop(0, pg // SCORE_PAGES, trip, 0)

    items = [(0, hh) for hh in range(2)] + [(j, hh) for j in reversed(range(1, nb)) for hh in range(2)]
    work = sum(j + 1 for j, _ in items)
    segments = [[] for _ in range(gps)]
    done = 0
    for item in items:
        segments[min(gps - 1, done * gps // work)].append(item)
        done += item[0] + 1

    accs = {}
    count = 0
    for sg in range(gps):
        g = seq * gps + sg
        wait_group(g)
        score_group(sg, lax.rem(g, K_SLOTS))

        start_group(lax.rem(g + K_SLOTS, total))

        if sg == 0:
            prepare()
        for j, hh in segments[sg]:
            accs.setdefault(j, []).append(attend(j, hh, count % 2))
            count += 1
            if len(accs[j]) == 2:
                finish(j, accs.pop(j))

    @pl.when(seq == n_seq - 1)
    def _():
        for g in range(K_SLOTS):
            wait_group(g)

    gate = g_scr[...] * (1.0 / MOBA_BLOCK)
    nidx = lax.broadcasted_iota(I32, gate.shape, 0)
    for r in range(MOBA_TOPK):
        mx = jnp.max(gate, axis=0)
        am = jnp.min(jnp.where(gate == mx[None], nidx, gate.shape[0]), axis=0)
        idx_ref[r] = am
        val_ref[r] = mx
        gate = jnp.where(nidx == am[None], -jnp.inf, gate)


def _attn_and_score(q, kt, vt, ga, rel_bias, pt_flat, q_dec, kcache_t, n_seq, n_pages):
    b, s, w = q.shape
    hd = w // N_HEADS
    blk = MOBA_BLOCK
    nb = s // blk
    npair = w // LANES
    nblk = n_pages // 2
    assert nb <= SUBLANES and LANES == 2 * hd
    assert n_seq == npair * b and n_pages % PAGES_PER_GROUP == 0
    assert (n_seq * (n_pages // PAGES_PER_GROUP)) % K_SLOTS == 0
    dq = jnp.arange(blk, dtype=I32)
    d_own = dq[:, None] - dq[None, :]
    bkt = jnp.stack([_t5_bucket(d_own), _t5_bucket(d_own + blk)])
    per_seq = lambda p, i, pt: (p * b + i, 0, 0, 0)
    grid_spec = pltpu.PrefetchScalarGridSpec(
        num_scalar_prefetch=1,
        grid=(npair, b),
        in_specs=[pl.BlockSpec(memory_space=pltpu.SMEM),
                  pl.BlockSpec((None, s, LANES), lambda p, i, pt: (i, 0, p)),
                  pl.BlockSpec((None, LANES, s), lambda p, i, pt: (i, p, 0)),
                  pl.BlockSpec((None, LANES, s), lambda p, i, pt: (i, p, 0)),
                  pl.BlockSpec((None, s, LANES), lambda p, i, pt: (i, 0, p)),
                  pl.BlockSpec((2, blk, blk), lambda p, i, pt: (0, 0, 0)),
                  pl.BlockSpec((w, LANES), lambda p, i, pt: (0, 0)),
                  pl.BlockSpec(memory_space=pl.ANY)],
        out_specs=[pl.BlockSpec((None, s, LANES), lambda p, i, pt: (i, 0, p)),
                   pl.BlockSpec((None, nblk, N_HEADS, MOBA_BLOCK), per_seq),
                   pl.BlockSpec((None, MOBA_TOPK, N_HEADS, LANES), per_seq),
                   pl.BlockSpec((None, MOBA_TOPK, N_HEADS, LANES), per_seq)],
        scratch_shapes=[pltpu.VMEM((2, 2, blk, blk), F32),
                        pltpu.VMEM((2, LANES, s), BF16),
                        pltpu.VMEM((2, LANES, s), BF16),
                        pltpu.VMEM((2, SUBLANES, s), F32),
                        pltpu.VMEM((2, s, LANES), BF16),
                        pltpu.VMEM((2, nb, blk, blk), F32),
                        pltpu.VMEM((K_SLOTS, PAGES_PER_GROUP, w, LANES), F32),
                        pltpu.SemaphoreType.DMA((K_SLOTS,)),
                        pltpu.VMEM((w, LANES), F32),
                        pltpu.VMEM((nblk, N_HEADS, LANES), F32)])
    return pl.pallas_call(
        functools.partial(_attn_score_kernel, hd=hd, nb=nb, n_pages=n_pages, n_seq=n_seq),
        out_shape=[jax.ShapeDtypeStruct((b, s, w), BF16),
                   jax.ShapeDtypeStruct((n_seq, nblk, N_HEADS, MOBA_BLOCK), F32),
                   jax.ShapeDtypeStruct((n_seq, MOBA_TOPK, N_HEADS, LANES), I32),
                   jax.ShapeDtypeStruct((n_seq, MOBA_TOPK, N_HEADS, LANES), F32)],
        grid_spec=grid_spec,
        compiler_params=_params("arbitrary", "arbitrary"),
        name="attn_score",
    )(pt_flat, rel_bias, q, kt, vt, ga, bkt, q_dec, kcache_t)


def _sample_proj_kernel(x_ref, mod_ref, g_ref, win_ref, wqkt_ref, cbuf_ref, h0_ref, cw_ref, cb_ref,
                        wg_ref, bg_ref, ap_ref,
                        qt_ref, kt_ref, k_ref, v_ref, h_ref, cnew_ref, ml_ref, ga_ref, *, d, w):
    n = x_ref.shape[0]
    xm = _modulated_norm(x_ref[...], mod_ref[...], g_ref[...], d).astype(BF16)
    pr = jnp.dot(xm, win_ref[...], preferred_element_type=F32)
    u = pr[:, 0:w]
    k_ref[...] = pr[:, 3 * w:4 * w]
    v_ref[...] = pr[:, 4 * w:5 * w]
    ga_ref[...] = pr[:, 5 * w:6 * w]
    xpad = jnp.concatenate([xm, jnp.zeros((LANES - n, d), BF16)], axis=0)
    pt = lax.dot_general(wqkt_ref[2 * w:4 * w, :], xpad, (((1,), (1,)), ((), ())),
                         preferred_element_type=F32)
    qt_ref[...] = pt[0:w]
    kt_ref[...] = pt[w:2 * w]
    cw = cw_ref[...]
    u_conv = cb_ref[...] + (cbuf_ref[0] * cw[0:1] + cbuf_ref[1] * cw[1:2]
                            + cbuf_ref[2] * cw[2:3] + u * cw[3:4])
    a, bx = _lru_gates(u_conv, wg_ref, bg_ref, ap_ref, w)
    h = a * h0_ref[...] + bx
    h_ref[...] = h
    cnew_ref[0] = cbuf_ref[1]
    cnew_ref[1] = cbuf_ref[2]
    cnew_ref[2] = u
    ml_ref[...] = h * _silu(pr[:, w:2 * w])


def _sample_proj(x, mod, g_norm, w_in_bf, w_in_t, cbuf, h0, conv_w, conv_b, wg, bg, a_param):
    n, d = x.shape
    w = h0.shape[1]
    row = jax.ShapeDtypeStruct((n, w), F32)
    col = jax.ShapeDtypeStruct((w, LANES), F32)
    return pl.pallas_call(
        functools.partial(_sample_proj_kernel, d=d, w=w),
        out_shape=[col, col, row, row, row, jax.ShapeDtypeStruct((CONV_W - 1, n, w), F32), row, row],
        compiler_params=pltpu.CompilerParams(vmem_limit_bytes=VMEM_LIMIT_BYTES),
        name="sample_proj",
    )(x, mod, g_norm, w_in_bf, w_in_t, cbuf, h0, conv_w, conv_b, wg, bg, a_param)


def _decode_kernel(pt_ref, ix_ref, s_ref, idxv_ref, valv_ref, qt_ref, kt_ref, vnew_ref, rbt_ref,
                   bkt_ref, vc_ref, y_ref, vbuf, sem, *, n_pages, n_seq, hd):
    b = pl.program_id(0)
    nsel = MOBA_TOPK
    blk = MOBA_BLOCK
    nb = n_pages // 2
    grp = DEC_SEQS
    n_steps = n_seq // grp

    def tile_copy(step, gi, r, h, pp, slot):
        sq = step * grp + gi
        page = pt_ref[sq * n_pages + 2 * ix_ref[(sq * nsel + r) * N_HEADS + h] + pp]
        return pltpu.make_async_copy(
            vc_ref.at[page, h], vbuf.at[slot, gi, h, :, pl.ds((2 * r + pp) * LANES, LANES)], sem.at[slot])

    def for_tiles(step, fn):
        slot = lax.rem(step, 2)
        for gi in range(grp):
            for r in range(nsel):
                for h in range(N_HEADS):
                    for pp in range(2):
                        fn(tile_copy(step, gi, r, h, pp, slot))

    @pl.when(b == 0)
    def _():
        for_tiles(0, lambda cp: cp.start())

    for_tiles(lax.rem(b + 1, n_steps), lambda cp: cp.start())

    sub = lax.broadcasted_iota(I32, (N_HEADS, blk), 0)
    rbt = rbt_ref[...]
    bias_last = jnp.zeros((N_HEADS, blk), F32)
    for k in range(REL_BUCKETS):
        bias_last = jnp.where(bkt_ref[...] == k, rbt[:, k:k + 1], bias_last)
    bias_far = rbt[:, REL_BUCKETS - 1:REL_BUCKETS]
    lane = lax.broadcasted_iota(I32, (N_HEADS, LANES), 1)
    qk = jnp.sum((qt_ref[...] * kt_ref[...]).reshape(N_HEADS, hd, LANES), axis=1)

    softmax = []
    for gi in range(grp):
        sq = b * grp + gi
        logits = []
        for r in range(nsel):
            lr = jnp.zeros((N_HEADS, blk), F32)
            for h in range(N_HEADS):
                n = ix_ref[(sq * nsel + r) * N_HEADS + h]
                lr = jnp.where(sub == h, s_ref[gi, n], lr)
            idv = idxv_ref[gi, r][:, 0:1]
            bias = jnp.where(idv == nb - 1, bias_last, bias_far)
            valid = valv_ref[gi, r][:, 0:1] > 0.5 * NEG_INF
            logits.append(jnp.where(valid, lr * (hd ** -0.5) + bias, NEG_INF))

        own = jnp.sum(jnp.where(lane == sq, qk, 0.0), axis=1, keepdims=True)
        l_own = own * (hd ** -0.5) + rbt[:, 0:1]
        m = l_own
        for lr in logits:
            m = jnp.maximum(m, jnp.max(lr, axis=1, keepdims=True))
        p_own = jnp.exp(l_own - m)
        den = p_own
        probs = []
        for lr in logits:
            pr = jnp.exp(lr - m)
            den = den + jnp.sum(pr, axis=1, keepdims=True)
            probs.append(pr)
        softmax.append((jnp.concatenate(probs, axis=1), p_own, den))

    for_tiles(b, lambda cp: cp.wait())
    slot = lax.rem(b, 2)
    for gi in range(grp):
        pcat, p_own, den = softmax[gi]
        rows = []
        for h in range(N_HEADS):
            vt = vbuf[slot, gi, h].astype(BF16)
            rows.append(lax.dot_general(pcat[h:h + 1, :].astype(BF16), vt, (((1,), (1,)), ((), ())),
                                        preferred_element_type=F32))
        acc = jnp.concatenate(rows, axis=0)
        y_ref[gi] = (acc + p_own * vnew_ref[gi]) / den

    @pl.when(b == n_steps - 1)
    def _():
        for_tiles(0, lambda cp: cp.wait())


def _decode(pt_flat, idx_flat, s_all, idxv, valv, qt, kt, v_new, rbt, bkt_last, vcache_t, n_seq, n_pages):
    w = qt.shape[0]
    hd = w // N_HEADS
    nb = n_pages // 2
    grp = DEC_SEQS
    assert n_seq % (2 * grp) == 0
    zero4 = lambda b, pt, ix: (b, 0, 0, 0)
    grid_spec = pltpu.PrefetchScalarGridSpec(
        num_scalar_prefetch=2,
        grid=(n_seq // grp,),
        in_specs=[pl.BlockSpec((grp, nb, N_HEADS, MOBA_BLOCK), zero4),
                  pl.BlockSpec((grp, MOBA_TOPK, N_HEADS, LANES), zero4),
                  pl.BlockSpec((grp, MOBA_TOPK, N_HEADS, LANES), zero4),
                  pl.BlockSpec((w, LANES), lambda b, pt, ix: (0, 0)),
                  pl.BlockSpec((w, LANES), lambda b, pt, ix: (0, 0)),
                  pl.BlockSpec((grp, N_HEADS, hd), lambda b, pt, ix: (b, 0, 0)),
                  pl.BlockSpec((N_HEADS, REL_BUCKETS), lambda b, pt, ix: (0, 0)),
                  pl.BlockSpec((1, MOBA_BLOCK), lambda b, pt, ix: (0, 0)),
                  pl.BlockSpec(memory_space=pl.ANY)],
        out_specs=pl.BlockSpec((grp, N_HEADS, hd), lambda b, pt, ix: (b, 0, 0)),
        scratch_shapes=[pltpu.VMEM((2, grp, N_HEADS, hd, MOBA_TOPK * MOBA_BLOCK), F32),
                        pltpu.SemaphoreType.DMA((2,))])
    return pl.pallas_call(
        functools.partial(_decode_kernel, n_pages=n_pages, n_seq=n_seq, hd=hd),
        out_shape=jax.ShapeDtypeStruct((n_seq, N_HEADS, hd), F32),
        grid_spec=grid_spec,
        compiler_params=_params("arbitrary"),
        name="decode_attn",
    )(pt_flat, idx_flat, s_all, idxv, valv, qt, kt, v_new, rbt, bkt_last, vcache_t)


def _outproj_sample_kernel(x_ref, mod_ref, ml_ref, ya_ref, ga_ref, wo_ref, gf_ref, y_ref, *, d, w):
    ma = ya_ref[...] * _silu(ga_ref[...])
    acc = jnp.dot(ml_ref[...].astype(BF16), wo_ref[0:w, :].astype(BF16), preferred_element_type=F32)
    acc = acc + jnp.dot(ma.astype(BF16), wo_ref[w:2 * w, :].astype(BF16), preferred_element_type=F32)
    out = x_ref[...] + mod_ref[:, 2 * d:3 * d] * acc
    ms = jnp.mean(out * out, axis=-1, keepdims=True)
    y_ref[...] = out * lax.rsqrt(ms + RMS_EPS) * gf_ref[...]


def _outproj_sample(x, mod, mix_l, y_att, ga, w_out, g_final):
    n, d = x.shape
    w = mix_l.shape[1]
    return pl.pallas_call(
        functools.partial(_outproj_sample_kernel, d=d, w=w),
        out_shape=jax.ShapeDtypeStruct((n, d), F32),
        compiler_params=pltpu.CompilerParams(vmem_limit_bytes=VMEM_LIMIT_BYTES),
        name="outproj_sample",
    )(x, mod, mix_l, y_att, ga, w_out, g_final)


def _block_diag(wb):
    n, c, dd = wb.shape
    return jnp.einsum("ncd,nm->ncmd", wb, jnp.eye(n, dtype=wb.dtype)).reshape(n * c, n * dd)


def kernel(x_prompt, x_sample, cache_k, cache_v, state_lru_h, state_lru_conv, page_table, c_prompt, c_sample, w_ada, b_ada, g_norm, w_in, conv_w, conv_b, w_rgate, b_rgate, w_igate, b_igate, lru_a_param, rel_bias, w_out, g_final):
    bp, s, d = x_prompt.shape
    ns = x_sample.shape[0]
    w = state_lru_h.shape[1]
    hd = w // N_HEADS
    n_pages = page_table.shape[1]
    page = cache_k.shape[1]
    past = n_pages * page
    assert x_sample.shape[1] == 1 and 2 * page == MOBA_BLOCK and past % MOBA_BLOCK == 0
    assert s % MOBA_BLOCK == 0 and s % ROW_TILE == 0 and s % LRU_CHUNK == 0
    assert w % LANES == 0 and ns <= LANES
    assert MOBA_BLOCK + 1 >= REL_MAX_DIST

    w_in_bf, w_in_t = _wprep(w_in, w)
    w_out_bf = w_out
    wg = jnp.concatenate([_block_diag(w_rgate), _block_diag(w_igate)], axis=1).astype(BF16)
    bg = jnp.concatenate([b_rgate, b_igate]).reshape(1, 2 * w)
    g_norm2 = g_norm.reshape(1, d)
    g_final2 = g_final.reshape(1, d)
    conv_b2 = conv_b.reshape(1, w)
    a_param2 = lru_a_param.reshape(1, w)

    mod = _adaln(jnp.concatenate([c_prompt, c_sample], axis=0), w_ada, b_ada)
    mod_p = mod[0:bp].reshape(bp, 1, 3 * d)
    mod_s = mod[bp:bp + ns]

    xs = x_sample.reshape(ns, d)
    cbuf = state_lru_conv.transpose(1, 0, 2)
    qt, ktn, k_s, v_s, h_s, cnew, mix_ls, ga_s = _sample_proj(
        xs, mod_s, g_norm2, w_in_bf, w_in_t, cbuf, state_lru_h, conv_w, conv_b2, wg, bg, a_param2)
    pt_flat = page_table.reshape(-1)
    n_phys = cache_k.shape[0]
    kc_t = cache_k.transpose(0, 2, 3, 1).reshape(n_phys, w, page)
    vc_t = cache_v.transpose(0, 2, 3, 1)

    u, gl, q, ga, kt, vt, ktb, vtb = _inproj(x_prompt, mod_p, g_norm2, w_in_bf, w_in_t)
    mix_a, s_all, idxv, valv = _attn_and_score(q, ktb, vtb, ga, rel_bias, pt_flat, qt, kc_t, ns, n_pages)
    y_prompt, h_last, tail = _lru_outproj(u, gl, conv_w, conv_b2, wg, bg, a_param2,
                                          x_prompt, mod_p, mix_a, w_out_bf, g_final2)
    k_prompt = kt.reshape(bp, N_HEADS, hd, s).transpose(0, 3, 1, 2)
    v_prompt = vt.reshape(bp, N_HEADS, hd, s).transpose(0, 3, 1, 2)
    lru_h_prompt = h_last[:, 0, :]
    lru_conv_prompt = tail[:, SUBLANES - (CONV_W - 1):, :]

    idx_flat = idxv[:, :, :, 0].reshape(-1)
    nb = n_pages // 2
    pos = (nb - 1) * MOBA_BLOCK + jnp.arange(MOBA_BLOCK, dtype=I32)
    bkt_last = _t5_bucket(past - pos).reshape(1, MOBA_BLOCK)
    y_att = _decode(pt_flat, idx_flat, s_all, idxv, valv, qt, ktn, v_s.reshape(ns, N_HEADS, hd),
                    rel_bias.T, bkt_last, vc_t, ns, n_pages)
    y_s = _outproj_sample(xs, mod_s, mix_ls, y_att.reshape(ns, w), ga_s, w_out_bf, g_final2)

    return (y_prompt, y_s.reshape(ns, 1, d), k_prompt, v_prompt, lru_h_prompt, lru_conv_prompt,
            k_s.reshape(ns, 1, N_HEADS, hd), v_s.reshape(ns, 1, N_HEADS, hd), h_s,
            cnew.transpose(1, 0, 2))
```
